```python
import math
import jax
import jax.numpy as jnp
from jax import lax
import numpy as np

D_MODEL = 1024
BATCH = 8
SEQ = 2048
DEPTH = 4
DEC_BATCH = 128
DEC_SEQ = 8
PAST_LEN = 16384
PAGE_SIZE = 128

BRANCH_WIDTH = D_MODEL // 2
N_BRANCH = 4
GDN_DK = 128
GDN_DV = 128
GDN_HEADS = BRANCH_WIDTH // GDN_DV
GDN_QKV = GDN_HEADS * (2 * GDN_DK + GDN_DV)
GDN_CONV = 4
GDN_CHUNK = 64
RET_DV = 128
RET_DK = RET_DV // 2
RET_HEADS = BRANCH_WIDTH // RET_DV
RET_CHUNK = 128
ROPE_BASE = 10000.0
POOL_WINDOWS = (2, 4, 8, 16)
POOL_WIDTH = BRANCH_WIDTH
POOL_GROUP = POOL_WIDTH // len(POOL_WINDOWS)
POOL_BUF = max(POOL_WINDOWS) - 1
S5_WIDTH = BRANCH_WIDTH
S5_GROUP = 16
S5_GROUPS = S5_WIDTH // S5_GROUP
S5_STATE = 64
MEM_LEN = 256
XA_HEADS = 4
XA_DH = D_MODEL // XA_HEADS
N_EXPERTS = 16
N_EXPERT_GROUPS = 4
EXPERTS_PER_GROUP = N_EXPERTS // N_EXPERT_GROUPS
TOP_K = 2
D_EXPERT = D_MODEL // 2
DN_ALPHA = (2.0 * DEPTH) ** 0.25
DN_BETA = (8.0 * DEPTH) ** -0.25
LN_EPS = 1e-5
RMS_EPS = 1e-6
IN_SPLITS = (GDN_QKV, GDN_HEADS * GDN_DV, GDN_HEADS, GDN_HEADS,
             RET_HEADS * RET_DK, RET_HEADS * RET_DK, RET_HEADS * RET_DV, RET_HEADS * RET_DV,
             POOL_WIDTH, S5_WIDTH, N_BRANCH * D_MODEL)
IN_COLS = sum(IN_SPLITS)
F32 = jnp.float32

kernel_name = 'hybrid_gdn_retention_pool_s5_moe_step'


def _split(h, sizes):
    return jnp.split(h, np.cumsum(sizes)[:-1].tolist(), axis=-1)


def _chunk_len(length, chunk):
    return chunk if length % chunk == 0 else length


def _to_chunks(t, c):
    b, l = t.shape[:2]
    return jnp.swapaxes(t.reshape(b, l // c, c, *t.shape[2:]), 2, 3)


def _from_chunks(o):
    n, b, h, c, d = o.shape
    return jnp.transpose(o, (1, 0, 3, 2, 4)).reshape(b, n * c, h, d)


def _layer_norm(x, g, b):
    xf = x.astype(F32)
    mu = jnp.mean(xf, -1, keepdims=True)
    var = jnp.mean(jnp.square(xf - mu), -1, keepdims=True)
    return ((xf - mu) * lax.rsqrt(var + LN_EPS) * g.astype(F32) + b.astype(F32)).astype(x.dtype)


def _l2norm(t):
    t = t.astype(F32)
    return t * lax.rsqrt(jnp.sum(t * t, -1, keepdims=True) + RMS_EPS)


def _short_conv(u, buf, w):
    full = jnp.concatenate([buf.astype(u.dtype), u], axis=1)
    out = lax.conv_general_dilated(full, w[:, None, :].astype(u.dtype), (1,), 'VALID',
                                   dimension_numbers=('NWC', 'WIO', 'NWC'),
                                   feature_group_count=u.shape[-1])
    return jax.nn.silu(out), full[:, full.shape[1] - (GDN_CONV - 1):]


def _gated_delta_rule(q, k, v, g, beta, s0):
    c = _chunk_len(q.shape[1], GDN_CHUNK)
    qc, kc, vc = (_to_chunks(t, c) for t in (q, k, v))
    gc = jnp.cumsum(_to_chunks(g, c), axis=-1)
    bc = _to_chunks(beta, c)[..., None]
    causal = jnp.tril(jnp.ones((c, c), bool))
    strict = jnp.tril(jnp.ones((c, c), bool), -1)
    diff = gc[..., :, None] - gc[..., None, :]
    decay = jnp.where(causal, jnp.exp(jnp.where(causal, diff, 0.0)), 0.0)
    kb = kc * bc
    a = jnp.where(strict, jnp.einsum('bnhid,bnhjd->bnhij', kb, kc) * decay, 0.0)
    rhs = jnp.concatenate([vc * bc, kb * jnp.exp(gc)[..., None]], axis=-1)
    sol = lax.linalg.triangular_solve(a + jnp.eye(c, dtype=F32), rhs, left_side=True,
                                      lower=True, unit_diagonal=True)
    u_c, w_c = sol[..., :GDN_DV], sol[..., GDN_DV:]
    attn = jnp.where(causal, jnp.einsum('bnhid,bnhjd->bnhij', qc, kc) * decay, 0.0)
    q_dec = qc * jnp.exp(gc)[..., None]
    k_dec = kc * jnp.exp(gc[..., -1:] - gc)[..., None]
    last = jnp.exp(gc[..., -1])[..., None, None]

    def step(s, xs):
        attn_n, u_n, w_n, qd_n, kd_n, last_n = xs
        v_new = u_n - jnp.einsum('bhcd,bhde->bhce', w_n, s)
        o = jnp.einsum('bhcd,bhde->bhce', qd_n, s) + jnp.einsum('bhij,bhje->bhie', attn_n, v_new)
        return s * last_n + jnp.einsum('bhcd,bhce->bhde', kd_n, v_new), o

    xs = tuple(jnp.moveaxis(t, 1, 0) for t in (attn, u_c, w_c, q_dec, k_dec, last))
    s, o = lax.scan(step, s0.astype(F32), xs)
    return _from_chunks(o), s


def _xpos_rotate(x, pos):
    inv_freq = 1.0 / (ROPE_BASE ** jnp.linspace(0.0, 1.0, x.shape[-1] // 2, dtype=F32))
    ang = pos[:, None] * inv_freq
    cos, sin = jnp.cos(ang)[None, :, None], jnp.sin(ang)[None, :, None]
    x1, x2 = x[..., 0::2], x[..., 1::2]
    return jnp.stack([x1 * cos - x2 * sin, x2 * cos + x1 * sin], axis=-1).reshape(x.shape)


def _retention(q, k, v, s0):
    c = _chunk_len(q.shape[1], RET_CHUNK)
    qc, kc, vc = (_to_chunks(t, c) for t in (q, k, v))
    log_gamma = jnp.log(1.0 - 2.0 ** (-5.0 - jnp.arange(RET_HEADS, dtype=F32)))
    idx = jnp.arange(c, dtype=F32)
    rel = idx[:, None] - idx[None, :]
    dmask = jnp.where(rel >= 0, jnp.exp(log_gamma[:, None, None] * jnp.maximum(rel, 0.0)), 0.0)
    o_inner = jnp.einsum('bnhij,bnhje->bnhie',
                         jnp.einsum('bnhid,bnhjd->bnhij', qc, kc) * dmask, vc)
    q_dec = qc * jnp.exp(log_gamma[:, None] * (idx + 1.0))[..., None]
    k_dec = kc * jnp.exp(log_gamma[:, None] * (c - 1.0 - idx))[..., None]
    chunk_decay = jnp.exp(log_gamma * c)[:, None, None]

    def step(s, xs):
        qd, kd, vn = xs
        o = jnp.einsum('bhcd,bhde->bhce', qd, s)
        return s * chunk_decay + jnp.einsum('bhcd,bhce->bhde', kd, vn), o

    xs = tuple(jnp.moveaxis(t, 1, 0) for t in (q_dec, k_dec, vc))
    s, o_cross = lax.scan(step, s0.astype(F32), xs)
    return _from_chunks(o_cross) + _from_chunks(jnp.moveaxis(o_inner, 1, 0)), s


def _multi_pool(u, buf, pos0, w_pool, pool_scale):
    B, L, _ = u.shape
    full = jnp.concatenate([buf.astype(u.dtype), u], axis=1)
    cs = jnp.cumsum(jnp.pad(full.astype(F32), ((0, 0), (1, 0), (0, 0))), axis=1)
    n_avail = pos0 + jnp.arange(L) + 1
    means = []
    for gi, w in enumerate(POOL_WINDOWS):
        ch = slice(gi * POOL_GROUP, (gi + 1) * POOL_GROUP)
        win = cs[:, POOL_BUF + 1:, ch] - cs[:, POOL_BUF + 1 - w:POOL_BUF + 1 - w + L, ch]
        means.append(win / jnp.minimum(n_avail, w).astype(F32)[None, :, None])
    pooled = jnp.concatenate(means, axis=-1) - u.astype(F32)
    mixed = jnp.einsum('blgc,gcd->blgd', pooled.reshape(B, L, len(POOL_WINDOWS), POOL_GROUP),
                       w_pool.astype(F32))
    return mixed.reshape(B, L, POOL_WIDTH) * pool_scale.astype(F32), full[:, L:]


def _complex_affine_combine(e1, e2):
    a1r, a1i, b1r, b1i = e1
    a2r, a2i, b2r, b2i = e2
    return (a2r * a1r - a2i * a1i,
            a2r * a1i + a2i * a1r,
            a2r * b1r - a2i * b1i + b2r,
            a2r * b1i + a2i * b1r + b2i)


def _s5(u, h0_re, h0_im, a_re, a_im, log_dt, b_re, b_im, c_re, c_im, d_skip, w_glu):
    B, L, _ = u.shape
    uf = u.astype(F32)
    ug = uf.reshape(B, L, S5_GROUPS, S5_GROUP)
    a_re, a_im = a_re.astype(F32), a_im.astype(F32)
    dt = jnp.exp(log_dt.astype(F32))[:, None]
    mag = jnp.exp(a_re * dt)
    ab_re, ab_im = mag * jnp.cos(a_im * dt), mag * jnp.sin(a_im * dt)
    den = a_re * a_re + a_im * a_im
    coef_re = ((ab_re - 1.0) * a_re + ab_im * a_im) / den
    coef_im = (ab_im * a_re - (ab_re - 1.0) * a_im) / den
    b_re, b_im = b_re.astype(F32), b_im.astype(F32)
    bb_re = coef_re[..., None] * b_re - coef_im[..., None] * b_im
    bb_im = coef_re[..., None] * b_im + coef_im[..., None] * b_re
    bu_re = jnp.einsum('blgc,gpc->blgp', ug, bb_re)
    bu_im = jnp.einsum('blgc,gpc->blgp', ug, bb_im)
    acc_re, acc_im, hs_re, hs_im = lax.associative_scan(
        _complex_affine_combine,
        (jnp.broadcast_to(ab_re, bu_re.shape), jnp.broadcast_to(ab_im, bu_im.shape), bu_re, bu_im),
        axis=1)
    h0r, h0i = h0_re.astype(F32)[:, None], h0_im.astype(F32)[:, None]
    h_re = hs_re + acc_re * h0r - acc_im * h0i
    h_im = hs_im + acc_re * h0i + acc_im * h0r
    y = (jnp.einsum('blgp,gcp->blgc', h_re, c_re.astype(F32))
         - jnp.einsum('blgp,gcp->blgc', h_im, c_im.astype(F32)))
    y = y.reshape(B, L, S5_WIDTH) + d_skip.astype(F32) * uf
    act = jax.nn.gelu(y)
    return act * jax.nn.sigmoid(act @ w_glu.astype(F32)), h_re[:, -1], h_im[:, -1]


def _cross_attn(x, mem_k, mem_v, w_q, w_o):
    B, L, _ = x.shape
    q = (x @ w_q).reshape(B, L, XA_HEADS, XA_DH)
    s = jnp.einsum('blhd,bmhd->bhlm', q, mem_k.astype(q.dtype)).astype(F32) * XA_DH ** -0.5
    p = jax.nn.softmax(s, axis=-1).astype(x.dtype)
    o = jnp.einsum('bhlm,bmhd->blhd', p, mem_v.astype(x.dtype)).reshape(B, L, D_MODEL)
    return o @ w_o


def _moe(x, w_router, b_router, w1, w3, w2):
    B, L, D = x.shape
    t = x.reshape(B * L, D)
    probs = jax.nn.softmax((t @ w_router).astype(F32) + b_router.astype(F32), axis=-1)
    grouped = probs.reshape(-1, N_EXPERT_GROUPS, EXPERTS_PER_GROUP)
    _, g_idx = lax.top_k(jnp.max(grouped, axis=-1), 1)
    in_group = jnp.arange(N_EXPERT_GROUPS) == g_idx
    cand = jnp.where(in_group[..., None], grouped, -1.0).reshape(-1, N_EXPERTS)
    top_p, top_e = lax.top_k(cand, TOP_K)
    gates = top_p / jnp.sum(top_p, axis=-1, keepdims=True)
    combine = jnp.sum(jax.nn.one_hot(top_e, N_EXPERTS, dtype=F32) * gates[..., None], axis=1)
    hid = jax.nn.silu(jnp.einsum('td,edf->tef', t, w1)) * jnp.einsum('td,edf->tef', t, w3)
    out = jnp.einsum('tef,efd->td', hid * combine[..., None].astype(hid.dtype), w2)
    return out.reshape(B, L, D)


def _block(x, mem_k, mem_v, conv_buf, s_gdn, s_ret, pool_buf, h_re, h_im, pos0, lw,
           w_router, b_router):
    B, L, _ = x.shape
    dt = x.dtype
    (gdn_qkv, gdn_z, gdn_b, gdn_a, ret_q, ret_k, ret_v, ret_g,
     pool_u, s5_u, gate_logits) = _split(x @ lw['w_in'], IN_SPLITS)

    qkv, new_conv = _short_conv(gdn_qkv, conv_buf, lw['gdn_conv_w'])
    q, k, v = _split(qkv, (GDN_HEADS * GDN_DK, GDN_HEADS * GDN_DK, GDN_HEADS * GDN_DV))
    q = _l2norm(q.reshape(B, L, GDN_HEADS, GDN_DK)) * GDN_DK ** -0.5
    k = _l2norm(k.reshape(B, L, GDN_HEADS, GDN_DK))
    v = v.astype(F32).reshape(B, L, GDN_HEADS, GDN_DV)
    beta = jax.nn.sigmoid(gdn_b.astype(F32))
    g = -jnp.exp(lw['gdn_a_log'].astype(F32)) * jax.nn.softplus(
        gdn_a.astype(F32) + lw['gdn_dt_bias'].astype(F32))
    o, new_gdn = _gated_delta_rule(q, k, v, g, beta, s_gdn)
    o = o * lax.rsqrt(jnp.mean(o * o, -1, keepdims=True) + RMS_EPS) * lw['gdn_norm_w'].astype(F32)
    br_a = (o * jax.nn.silu(gdn_z.astype(F32).reshape(B, L, GDN_HEADS, GDN_DV))).reshape(B, L, BRANCH_WIDTH)

    pos = (pos0 + jnp.arange(L)).astype(F32)
    rq = _xpos_rotate(ret_q.astype(F32).reshape(B, L, RET_HEADS, RET_DK), pos) * RET_DK ** -0.5
    rk = _xpos_rotate(ret_k.astype(F32).reshape(B, L, RET_HEADS, RET_DK), pos)
    rv = ret_v.astype(F32).reshape(B, L, RET_HEADS, RET_DV)
    ro, new_ret = _retention(rq, rk, rv, s_ret)
    mu = jnp.mean(ro, -1, keepdims=True)
    ro = (ro - mu) * lax.rsqrt(jnp.mean(jnp.square(ro - mu), -1, keepdims=True) + LN_EPS)
    br_b = jax.nn.silu(ret_g.astype(F32)) * ro.reshape(B, L, BRANCH_WIDTH)

    br_c, new_pool = _multi_pool(pool_u, pool_buf, pos0, lw['pool_w'], lw['pool_scale'])

    br_d, new_h_re, new_h_im = _s5(s5_u, h_re, h_im, lw['s5_a_re'], lw['s5_a_im'], lw['s5_log_dt'],
                                   lw['s5_b_re'], lw['s5_b_im'], lw['s5_c_re'], lw['s5_c_im'],
                                   lw['s5_d'], lw['s5_w_glu'])

    branches = jnp.stack([br_a, br_b, br_c, br_d], axis=2).astype(dt)
    proj = jnp.einsum('blnc,ncd->blnd', branches, lw['w_branch'])
    gates = jax.nn.sigmoid(gate_logits.reshape(B, L, N_BRANCH, D_MODEL))
    mixed = jnp.sum(gates * proj, axis=2) @ lw['w_out']
    x = _layer_norm(DN_ALPHA * x + mixed, lw['ln_g'][0], lw['ln_b'][0])
    x = _layer_norm(DN_ALPHA * x + _cross_attn(x, mem_k, mem_v, lw['xa_w_q'], lw['xa_w_o']),
                    lw['ln_g'][1], lw['ln_b'][1])
    x = _layer_norm(DN_ALPHA * x + _moe(x, w_router, b_router, lw['moe_w1'], lw['moe_w3'], lw['moe_w2']),
                    lw['ln_g'][2], lw['ln_b'][2])
    new_states = (new_conv.astype(dt), new_gdn.astype(dt), new_ret.astype(dt),
                  new_pool.astype(dt), new_h_re.astype(dt), new_h_im.astype(dt))
    return x, new_states


def setup_inputs(seed: int = 0) -> dict:
    key = jax.random.key(seed)
    keys = iter(jax.random.split(key, 48))

    def nrm(shape, scale=1.0):
        return jax.random.normal(next(keys), shape, F32) * scale

    def unif(shape, lo, hi):
        return jax.random.uniform(next(keys), shape, F32, lo, hi)

    gdn_dt = jnp.exp(unif((DEPTH, GDN_HEADS), math.log(1e-3), math.log(1e-1)))
    return dict(
        x_prompt=nrm((BATCH, SEQ, D_MODEL)),
        x_sample=nrm((DEC_BATCH, DEC_SEQ, D_MODEL)),
        mem_prompt=nrm((BATCH, MEM_LEN, D_MODEL)),
        state_gdn_conv=nrm((DEPTH, DEC_BATCH, GDN_CONV - 1, GDN_QKV)),
        state_gdn=nrm((DEPTH, DEC_BATCH, GDN_HEADS, GDN_DK, GDN_DV), 0.1),
        state_ret=nrm((DEPTH, DEC_BATCH, RET_HEADS, RET_DK, RET_DV)),
        state_pool=nrm((DEPTH, DEC_BATCH, POOL_BUF, POOL_WIDTH)),
        state_s5_re=nrm((DEPTH, DEC_BATCH, S5_GROUPS, S5_STATE), 0.3),
        state_s5_im=nrm((DEPTH, DEC_BATCH, S5_GROUPS, S5_STATE), 0.3),
        cache_mem_k=nrm((DEPTH, DEC_BATCH, MEM_LEN, XA_HEADS, XA_DH)),
        cache_mem_v=nrm((DEPTH, DEC_BATCH, MEM_LEN, XA_HEADS, XA_DH), DN_BETA),
        w_in=nrm((DEPTH, D_MODEL, IN_COLS), D_MODEL ** -0.5),
        gdn_conv_w=nrm((DEPTH, GDN_CONV, GDN_QKV), GDN_CONV ** -0.5),
        gdn_a_log=jnp.log(unif((DEPTH, GDN_HEADS), 1.0, 16.0)),
        gdn_dt_bias=gdn_dt + jnp.log(-jnp.expm1(-gdn_dt)),
        gdn_norm_w=1.0 + nrm((DEPTH, GDN_DV), 0.02),
        pool_w=nrm((DEPTH, len(POOL_WINDOWS), POOL_GROUP, POOL_GROUP), POOL_GROUP ** -0.5),
        pool_scale=1.0 + nrm((DEPTH, POOL_WIDTH), 0.02),
        s5_a_re=-0.5 + nrm((DEPTH, S5_GROUPS, S5_STATE), 0.01),
        s5_a_im=math.pi * jnp.arange(S5_STATE, dtype=F32) + nrm((DEPTH, S5_GROUPS, S5_STATE), 0.01),
        s5_log_dt=unif((DEPTH, S5_GROUPS), math.log(1e-3), math.log(1e-1)),
        s5_b_re=nrm((DEPTH, S5_GROUPS, S5_STATE, S5_GROUP), (2 * S5_GROUP) ** -0.5),
        s5_b_im=nrm((DEPTH, S5_GROUPS, S5_STATE, S5_GROUP), (2 * S5_GROUP) ** -0.5),
        s5_c_re=nrm((DEPTH, S5_GROUPS, S5_GROUP, S5_STATE), S5_STATE ** -0.5),
        s5_c_im=nrm((DEPTH, S5_GROUPS, S5_GROUP, S5_STATE), S5_STATE ** -0.5),
        s5_d=nrm((DEPTH, S5_WIDTH), 0.5),
        s5_w_glu=nrm((DEPTH, S5_WIDTH, S5_WIDTH), S5_WIDTH ** -0.5),
        w_branch=nrm((DEPTH, N_BRANCH, BRANCH_WIDTH, D_MODEL), BRANCH_WIDTH ** -0.5 * DN_BETA),
        w_out=nrm((DEPTH, D_MODEL, D_MODEL), D_MODEL ** -0.5 * DN_BETA),
        xa_w_q=nrm((DEPTH, D_MODEL, D_MODEL), D_MODEL ** -0.5),
        xa_w_k=nrm((DEPTH, D_MODEL, D_MODEL), D_MODEL ** -0.5),
        xa_w_v=nrm((DEPTH, D_MODEL, D_MODEL), D_MODEL ** -0.5 * DN_BETA),
        xa_w_o=nrm((DEPTH, D_MODEL, D_MODEL), D_MODEL ** -0.5 * DN_BETA),
        ln_g=1.0 + nrm((DEPTH, 3, D_MODEL), 0.02),
        ln_b=nrm((DEPTH, 3, D_MODEL), 0.02),
        w_router=nrm((D_MODEL, N_EXPERTS), D_MODEL ** -0.5),
        b_router=nrm((N_EXPERTS,), 0.01),
        moe_w1=nrm((DEPTH, N_EXPERTS, D_MODEL, D_EXPERT), D_MODEL ** -0.5 * DN_BETA),
        moe_w3=nrm((DEPTH, N_EXPERTS, D_MODEL, D_EXPERT), D_MODEL ** -0.5 * DN_BETA),
        moe_w2=nrm((DEPTH, N_EXPERTS, D_EXPERT, D_MODEL), D_EXPERT ** -0.5 * DN_BETA),
    )


def reference(x_prompt, x_sample, mem_prompt, state_gdn_conv, state_gdn, state_ret, state_pool,
              state_s5_re, state_s5_im, cache_mem_k, cache_mem_v, w_in, gdn_conv_w, gdn_a_log,
              gdn_dt_bias, gdn_norm_w, pool_w, pool_scale, s5_a_re, s5_a_im, s5_log_dt, s5_b_re,
              s5_b_im, s5_c_re, s5_c_im, s5_d, s5_w_glu, w_branch, w_out, xa_w_q, xa_w_k, xa_w_v,
              xa_w_o, ln_g, ln_b, w_router, b_router, moe_w1, moe_w3, moe_w2):
    bp = x_prompt.shape[0]
    dt_p = x_prompt.dtype
    zero_states = (jnp.zeros((bp, GDN_CONV - 1, GDN_QKV), dt_p),
                   jnp.zeros((bp, GDN_HEADS, GDN_DK, GDN_DV), dt_p),
                   jnp.zeros((bp, RET_HEADS, RET_DK, RET_DV), dt_p),
                   jnp.zeros((bp, POOL_BUF, POOL_WIDTH), dt_p),
                   jnp.zeros((bp, S5_GROUPS, S5_STATE), dt_p),
                   jnp.zeros((bp, S5_GROUPS, S5_STATE), dt_p))
    yp, ys = x_prompt, x_sample
    prompt_states, sample_states = [], []
    for l in range(DEPTH):
        lw = dict(w_in=w_in[l], gdn_conv_w=gdn_conv_w[l], gdn_a_log=gdn_a_log[l],
                  gdn_dt_bias=gdn_dt_bias[l], gdn_norm_w=gdn_norm_w[l], pool_w=pool_w[l],
                  pool_scale=pool_scale[l], s5_a_re=s5_a_re[l], s5_a_im=s5_a_im[l],
                  s5_log_dt=s5_log_dt[l], s5_b_re=s5_b_re[l], s5_b_im=s5_b_im[l],
                  s5_c_re=s5_c_re[l], s5_c_im=s5_c_im[l], s5_d=s5_d[l], s5_w_glu=s5_w_glu[l],
                  w_branch=w_branch[l], w_out=w_out[l], xa_w_q=xa_w_q[l], xa_w_o=xa_w_o[l],
                  ln_g=ln_g[l], ln_b=ln_b[l], moe_w1=moe_w1[l], moe_w3=moe_w3[l], moe_w2=moe_w2[l])
        mem_k = (mem_prompt @ xa_w_k[l]).reshape(bp, MEM_LEN, XA_HEADS, XA_DH)
        mem_v = (mem_prompt @ xa_w_v[l]).reshape(bp, MEM_LEN, XA_HEADS, XA_DH)
        yp, st_p = _block(yp, mem_k, mem_v, *zero_states, 0, lw, w_router, b_router)
        prompt_states.append(st_p + (mem_k, mem_v))
        ys, st_s = _block(ys, cache_mem_k[l], cache_mem_v[l], state_gdn_conv[l], state_gdn[l],
                          state_ret[l], state_pool[l], state_s5_re[l], state_s5_im[l], PAST_LEN,
                          lw, w_router, b_router)
        sample_states.append(st_s)
    (p_conv, p_gdn, p_ret, p_pool, p_s5_re, p_s5_im, p_mem_k, p_mem_v) = [
        jnp.stack(t) for t in zip(*prompt_states)]
    (s_conv, s_gdn, s_ret, s_pool, s_s5_re, s_s5_im) = [jnp.stack(t) for t in zip(*sample_states)]
    return (yp, ys, p_conv, p_gdn, p_ret, p_pool, p_s5_re, p_s5_im, p_mem_k, p_mem_v,
            s_conv, s_gdn, s_ret, s_pool, s_s5_re, s_s5_im)
```

```python
import functools
import math

import jax
import jax.numpy as jnp
import numpy as np
from jax import lax
from jax.experimental import pallas as pl
from jax.experimental.pallas import tpu as pltpu

F32 = jnp.float32
BF16 = jnp.bfloat16

D_MODEL = 1024
DEPTH = 4
BRANCH = 512
N_BRANCH = 4
GDN_HEADS = 4
GDN_DK = 128
GDN_QKV = 1536
GDN_CONV = 4
RET_HEADS = 4
RET_DK = 64
RET_DV = 128
ROPE_BASE = 10000.0
POOL_WINDOWS = (2, 4, 8, 16)
POOL_BUF = 15
S5_GROUPS = 32
S5_GROUP = 16
S5_STATE = 64
S5_CH = S5_GROUPS * S5_STATE
MEM_LEN = 256
XA_HEADS = 4
XA_DH = 256
N_EXPERTS = 16
EXPERTS_PER_GROUP = 4
D_EXPERT = 512
DN_ALPHA = (2.0 * DEPTH) ** 0.25
LN_EPS = 1e-5
RMS_EPS = 1e-6

LANES = 128
SUBLANES = 8
VMEM_LIMIT = 52 * 1024 * 1024

COL_QKV = 0
COL_Z = 1536
COL_RET = 2048
COL_GATE = 4096
COL_POOL = 8192
COL_S5 = 8704
H_COLS = 9216


def _cparams(*sem):
    return pltpu.CompilerParams(dimension_semantics=sem, vmem_limit_bytes=VMEM_LIMIT)


def _bdot(a, b):
    return jnp.dot(a.astype(BF16), b.astype(BF16), preferred_element_type=F32)


def _bdot_nt(a, b):
    return lax.dot_general(a.astype(BF16), b.astype(BF16), (((1,), (1,)), ((), ())),
                           preferred_element_type=F32)


def _silu(x):
    return x * jax.nn.sigmoid(x)


def _layer_norm(x, g, b):
    mu = jnp.mean(x, -1, keepdims=True)
    xc = x - mu
    var = jnp.mean(xc * xc, -1, keepdims=True)
    return xc * lax.rsqrt(var + LN_EPS) * g + b


def _mm_kernel(x_ref, w_ref, o_ref, xb_ref):
    @pl.when(pl.program_id(1) == 0)
    def _():
        xb_ref[...] = x_ref[...].astype(BF16)

    o_ref[...] = jnp.dot(xb_ref[...], w_ref[0], preferred_element_type=F32)


def _matmul(x, w, layer, tm, tn):
    m, k = x.shape
    n = w.shape[-1]
    return pl.pallas_call(
        _mm_kernel,
        grid=(m // tm, n // tn),
        in_specs=[pl.BlockSpec((tm, k), lambda i, j: (i, 0)),
                  pl.BlockSpec((1, k, tn), lambda i, j: (layer, 0, j))],
        out_specs=pl.BlockSpec((tm, tn), lambda i, j: (i, j)),
        out_shape=jax.ShapeDtypeStruct((m, n), F32),
        scratch_shapes=[pltpu.VMEM((tm, k), BF16)],
        compiler_params=_cparams("parallel", "arbitrary"),
        name="token_matmul",
    )(x, w)


def _seg_ids(c, sl):
    r = lax.broadcasted_iota(jnp.int32, (c, 1), 0)
    sh = int(math.log2(sl))
    return r, r & (sl - 1), r >> sh


def _gdn_kernel(qkv_ref, z_ref, ba_ref, cs_ref, s0_ref, cw_ref, gp_ref, nw_ref,
                o_ref, s_ref, full_ref, *, ns, sl):
    c = ns * sl
    n = pl.program_id(1)

    @pl.when(n == 0)
    def _():
        full_ref[:, 5:8, :] = cs_ref[0]
        s_ref[...] = s0_ref[0]

    u3 = qkv_ref[...]
    full_ref[:, 8:8 + sl, :] = u3
    cw = cw_ref[0]
    acc = (cw[3:4] * u3 + cw[2:3] * full_ref[:, 7:7 + sl, :] + cw[1:2] * full_ref[:, 6:6 + sl, :]
           + cw[0:1] * full_ref[:, 5:5 + sl, :])
    full_ref[:, 5:8, :] = full_ref[:, 5 + sl:8 + sl, :]
    qkv = _silu(acc).reshape(c, GDN_QKV)

    ba = ba_ref[...].reshape(c, LANES)
    beta_t = jax.nn.sigmoid(ba)
    xs = ba + gp_ref[0, 1:2, :]
    softplus = jnp.maximum(xs, 0.0) + jnp.log1p(jnp.exp(-jnp.abs(xs)))
    g_t = -jnp.exp(gp_ref[0, 0:1, :]) * softplus

    r, t, seq = _seg_ids(c, sl)
    gc = g_t
    s = 1
    while s < sl:
        gc = gc + jnp.where(t >= s, pltpu.roll(gc, s, 0), 0.0)
        s *= 2
    tot = gc
    s = 1
    while s < sl:
        tot = jnp.where(t + s < sl, pltpu.roll(tot, c - s, 0), tot)
        s *= 2
    rc = tot - gc
    if c < LANES:
        gc_pad = jnp.concatenate([gc, jnp.zeros((LANES - c, LANES), F32)], axis=0)
    else:
        gc_pad = gc
    gc_t = gc_pad.T

    rr = lax.broadcasted_iota(jnp.int32, (c, c), 0)
    cc = lax.broadcasted_iota(jnp.int32, (c, c), 1)
    sh = int(math.log2(sl))
    same = (rr >> sh) == (cc >> sh)
    causal = jnp.logical_and(same, rr >= cc)
    strict = jnp.logical_and(same, rr > cc)
    eye = (rr == cc).astype(F32)
    cseq = lax.broadcasted_iota(jnp.int32, (1, c), 1) >> sh

    z = z_ref[...].reshape(c, BRANCH)
    nw = nw_ref[0]
    outs = []
    for h in range(GDN_HEADS):
        q = qkv[:, h * 128:(h + 1) * 128]
        k = qkv[:, 512 + h * 128:512 + (h + 1) * 128]
        v = qkv[:, 1024 + h * 128:1024 + (h + 1) * 128]
        q = q * lax.rsqrt(jnp.sum(q * q, -1, keepdims=True) + RMS_EPS) * (GDN_DK ** -0.5)
        k = k * lax.rsqrt(jnp.sum(k * k, -1, keepdims=True) + RMS_EPS)
        beta = beta_t[:, h:h + 1]
        gcc = gc[:, 4 + h:5 + h]
        gcr = gc_t[4 + h:5 + h, 0:c]
        rcc = rc[:, 4 + h:5 + h]
        decay = jnp.where(causal, jnp.exp(jnp.where(causal, gcc - gcr, 0.0)), 0.0)
        kb = k * beta
        a = jnp.where(strict, _bdot_nt(kb, k) * decay, 0.0)
        npow = -a
        sm = eye + npow
        if sl > 2:
            npow = _bdot(npow, npow)
            m = 2
            while 2 * m < sl:
                prod = _bdot(npow, jnp.concatenate([npow, sm], axis=1))
                npow = prod[:, :c]
                sm = sm + prod[:, c:]
                m *= 2
            sm = sm + _bdot(npow, sm)
        eg = jnp.exp(gcc)
        rhs = jnp.concatenate([v * beta, kb * eg], axis=1)
        sol = _bdot(sm, rhs)
        u_c = sol[:, :128]
        w_c = sol[:, 128:]
        attn = jnp.where(causal, _bdot_nt(q, k) * decay, 0.0)
        q_dec = q * eg
        k_dec_t = (k * jnp.exp(rcc)).T
        states = [s_ref[si, h] for si in range(ns)]
        v_parts, o_parts = [], []
        for si in range(ns):
            rows = slice(si * sl, (si + 1) * sl)
            v_parts.append(u_c[rows] - _bdot(w_c[rows], states[si]))
            o_parts.append(_bdot(q_dec[rows], states[si]))
        v_new = v_parts[0] if ns == 1 else jnp.concatenate(v_parts, axis=0)
        o = (o_parts[0] if ns == 1 else jnp.concatenate(o_parts, axis=0)) + _bdot(attn, v_new)
        for si in range(ns):
            kd = k_dec_t if ns == 1 else jnp.where(cseq == si, k_dec_t, 0.0)
            last = jnp.exp(tot[si * sl:si * sl + 1, 4 + h:5 + h])
            s_ref[si, h] = states[si] * last + _bdot(kd, v_new)
        o = o * lax.rsqrt(jnp.mean(o * o, -1, keepdims=True) + RMS_EPS) * nw
        outs.append(o * _silu(z[:, h * 128:(h + 1) * 128]))
    o_ref[...] = jnp.concatenate(outs, axis=1).reshape(ns, sl, BRANCH)


def _gdn(h3, ba3, conv_state, s0, layer_state, cw, gp, nw, layer, ns, sl):
    nseq, length, _ = h3.shape
    grid = (nseq // ns, length // sl)
    kern = functools.partial(_gdn_kernel, ns=ns, sl=sl)
    return pl.pallas_call(
        kern,
        grid=grid,
        in_specs=[
            pl.BlockSpec((ns, sl, GDN_QKV), lambda b, n: (b, n, COL_QKV // GDN_QKV)),
            pl.BlockSpec((ns, sl, BRANCH), lambda b, n: (b, n, COL_Z // BRANCH)),
            pl.BlockSpec((ns, sl, LANES), lambda b, n: (b, n, 0)),
            pl.BlockSpec((1, ns, GDN_CONV - 1, GDN_QKV), lambda b, n: (layer_state, b, 0, 0)),
            pl.BlockSpec((1, ns, GDN_HEADS, 128, 128), lambda b, n: (layer_state, b, 0, 0, 0)),
            pl.BlockSpec((1, GDN_CONV, GDN_QKV), lambda b, n: (layer, 0, 0)),
            pl.BlockSpec((1, 2, LANES), lambda b, n: (layer, 0, 0)),
            pl.BlockSpec((1, 1, LANES), lambda b, n: (layer, 0, 0)),
        ],
        out_specs=[
            pl.BlockSpec((ns, sl, BRANCH), lambda b, n: (b, n, 0)),
            pl.BlockSpec((ns, GDN_HEADS, 128, 128), lambda b, n: (b, 0, 0, 0)),
        ],
        out_shape=[jax.ShapeDtypeStruct((nseq, length, BRANCH), F32),
                   jax.ShapeDtypeStruct((nseq, GDN_HEADS, 128, 128), F32)],
        scratch_shapes=[pltpu.VMEM((ns, SUBLANES + sl, GDN_QKV), F32)],
        compiler_params=_cparams("parallel", "arbitrary"),
        name="gated_deltanet",
    )(h3, h3, ba3, conv_state, s0, cw, gp, nw)


_RET_LOG_GAMMA = tuple(math.log(1.0 - 2.0 ** (-5.0 - h)) for h in range(RET_HEADS))


def _per_head(idx, vals):
    out = jnp.full(idx.shape, vals[3], F32)
    for h in (2, 1, 0):
        out = jnp.where(idx < (h + 1) * RET_DK, vals[h], out)
    return out


def _ret_kernel(hb_ref, cos_ref, sin_ref, s0_ref, o_ref, s_ref, *, ns, sl):
    c = ns * sl
    n = pl.program_id(1)

    @pl.when(n == 0)
    def _():
        s_ref[...] = s0_ref[0]

    hb = hb_ref[...].reshape(c, 2048)
    cos = cos_ref[...]
    sin = sin_ref[...]
    rq = (hb[:, 0:256] * cos + hb[:, 512:768] * sin) * (RET_DK ** -0.5)
    rk = hb[:, 256:512] * cos + hb[:, 768:1024] * sin
    v = hb[:, 1024:1536]
    g = hb[:, 1536:2048]

    r, t, seq = _seg_ids(c, sl)
    tf = t.astype(F32)
    lane = lax.broadcasted_iota(jnp.int32, (1, 256), 1)
    lgl = _per_head(lane, _RET_LOG_GAMMA)
    q_dec = rq * jnp.exp(lgl * (tf + 1.0))
    k_dec = rk * jnp.exp(lgl * (sl - 1.0 - tf))
    k_dec_t = k_dec.T

    rr = lax.broadcasted_iota(jnp.int32, (c, c), 0)
    cc = lax.broadcasted_iota(jnp.int32, (c, c), 1)
    sh = int(math.log2(sl))
    causal = jnp.logical_and((rr >> sh) == (cc >> sh), rr >= cc)
    rel = jnp.maximum(rr - cc, 0).astype(F32)
    cseq = lax.broadcasted_iota(jnp.int32, (1, c), 1) >> sh
    rowi = lax.broadcasted_iota(jnp.int32, (256, 1), 0)
    cd_rows = jnp.exp(_per_head(rowi, _RET_LOG_GAMMA) * float(sl))

    cross = []
    for si in range(ns):
        rows = slice(si * sl, (si + 1) * sl)
        st = s_ref[si]
        parts = []
        for h in range(RET_HEADS):
            mh = jnp.logical_and(lane >= h * RET_DK, lane < (h + 1) * RET_DK)
            parts.append(_bdot(jnp.where(mh, q_dec[rows], 0.0), st))
        cross.append(parts)
        kd = k_dec_t if ns == 1 else jnp.where(cseq == si, k_dec_t, 0.0)
        res = _bdot(kd, v)
        upd = jnp.zeros((256, RET_DV), F32)
        for h in range(RET_HEADS):
            rm = jnp.logical_and(rowi >= h * RET_DK, rowi < (h + 1) * RET_DK)
            upd = upd + jnp.where(rm, res[:, h * 128:(h + 1) * 128], 0.0)
        s_ref[si] = st * cd_rows + upd

    outs = []
    for h in range(RET_HEADS):
        mh = jnp.logical_and(lane >= h * RET_DK, lane < (h + 1) * RET_DK)
        dmask = jnp.where(causal, jnp.exp(_RET_LOG_GAMMA[h] * rel), 0.0)
        sc = _bdot_nt(jnp.where(mh, rq, 0.0), rk) * dmask
        o = _bdot(sc, v[:, h * 128:(h + 1) * 128])
        oc = cross[0][h] if ns == 1 else jnp.concatenate([cross[si][h] for si in range(ns)], axis=0)
        o = o + oc
        mu = jnp.mean(o, -1, keepdims=True)
        oc2 = o - mu
        o = oc2 * lax.rsqrt(jnp.mean(oc2 * oc2, -1, keepdims=True) + LN_EPS)
        outs.append(_silu(g[:, h * 128:(h + 1) * 128]) * o)
    o_ref[...] = jnp.concatenate(outs, axis=1).reshape(ns, sl, BRANCH)


def _retention(h3, cos_tab, sin_tab, s0, layer_state, ns, sl):
    nseq, length, _ = h3.shape
    c = ns * sl
    grid = (nseq // ns, length // sl)
    kern = functools.partial(_ret_kernel, ns=ns, sl=sl)
    return pl.pallas_call(
        kern,
        grid=grid,
        in_specs=[
            pl.BlockSpec((ns, sl, 2048), lambda b, n: (b, n, COL_RET // 2048)),
            pl.BlockSpec((c, 256), lambda b, n: (n, 0)),
            pl.BlockSpec((c, 256), lambda b, n: (n, 0)),
            pl.BlockSpec((1, ns, 256, RET_DV), lambda b, n: (layer_state, b, 0, 0)),
        ],
        out_specs=[
            pl.BlockSpec((ns, sl, BRANCH), lambda b, n: (b, n, 0)),
            pl.BlockSpec((ns, 256, RET_DV), lambda b, n: (b, 0, 0)),
        ],
        out_shape=[jax.ShapeDtypeStruct((nseq, length, BRANCH), F32),
                   jax.ShapeDtypeStruct((nseq, 256, RET_DV), F32)],
        compiler_params=_cparams("parallel", "arbitrary"),
        name="retention",
    )(h3, cos_tab, sin_tab, s0)


def _pool_kernel(u_ref, st_ref, w_ref, sc_ref, o_ref, full_ref, *, ns, sl, pos0, carry):
    c = ns * sl
    n = pl.program_id(1)

    @pl.when(n == 0)
    def _():
        full_ref[:, 1:16, :] = st_ref[0]

    u3 = u_ref[...]
    full_ref[:, 16:16 + sl, :] = u3
    tpos = lax.broadcasted_iota(jnp.int32, (1, sl, 1), 1) + n * sl
    n_avail = (tpos + (pos0 + 1)).astype(F32)
    outs = []
    for gi, w in enumerate(POOL_WINDOWS):
        cols = slice(gi * 128, (gi + 1) * 128)
        ug = u3[:, :, cols]
        acc = ug
        for j in range(1, w):
            acc = acc + full_ref[:, 16 - j:16 - j + sl, cols]
        pooled = acc / jnp.minimum(n_avail, float(w)) - ug
        mixed = _bdot(pooled.reshape(c, 128), w_ref[0, gi])
        outs.append(mixed * sc_ref[0, :, cols])
    if carry:
        full_ref[:, 1:16, :] = full_ref[:, sl + 1:sl + 16, :]
    o_ref[...] = jnp.concatenate(outs, axis=1).reshape(ns, sl, BRANCH)


def _pool(h3, state, layer_state, w_pool, pool_scale, layer, ns, sl, pos0):
    nseq, length, _ = h3.shape
    grid = (nseq // ns, length // sl)
    kern = functools.partial(_pool_kernel, ns=ns, sl=sl, pos0=pos0, carry=grid[1] > 1)
    return pl.pallas_call(
        kern,
        grid=grid,
        in_specs=[
            pl.BlockSpec((ns, sl, BRANCH), lambda b, n: (b, n, COL_POOL // BRANCH)),
            pl.BlockSpec((1, ns, POOL_BUF, BRANCH), lambda b, n: (layer_state, b, 0, 0)),
            pl.BlockSpec((1, 4, 128, 128), lambda b, n: (layer, 0, 0, 0)),
            pl.BlockSpec((1, 1, BRANCH), lambda b, n: (layer, 0, 0)),
        ],
        out_specs=pl.BlockSpec((ns, sl, BRANCH), lambda b, n: (b, n, 0)),
        out_shape=jax.ShapeDtypeStruct((nseq, length, BRANCH), F32),
        scratch_shapes=[pltpu.VMEM((ns, 16 + sl, BRANCH), F32)],
        compiler_params=_cparams("parallel", "arbitrary"),
        name="multi_pool",
    )(h3, state, w_pool, pool_scale)


S5_SEQS = SUBLANES
S5_SLABS = 2 * S5_CH // LANES


def _s5_pitch(tl):
    return tl + 4


def _s5_kernel(u_ref, hr_ref, hi_ref, bd_ref, cd_ref, ar_ref, ai_ref, d_ref, glu_ref,
               o_ref, or_ref, oi_ref, bu_ref, h_ref, arb_ref, aib_ref, *, tl):
    pitch = _s5_pitch(tl)
    n = pl.program_id(1)

    @pl.when(n == 0)
    def _():
        hr = hr_ref[0]
        hi = hi_ref[0]
        for kb in range(2):
            for j in range(8):
                src = slice(kb * 1024 + j * 128, kb * 1024 + (j + 1) * 128)
                h_ref[kb * 16 + j] = hr[:, src]
                h_ref[kb * 16 + 8 + j] = hi[:, src]
        for kb in range(2):
            for j in range(8):
                src = slice(kb * 1024 + j * 128, kb * 1024 + (j + 1) * 128)
                arb_ref[kb * 8 + j] = jnp.broadcast_to(ar_ref[0, :, src], (S5_SEQS, LANES))
                aib_ref[kb * 8 + j] = jnp.broadcast_to(ai_ref[0, :, src], (S5_SEQS, LANES))

    for s in range(S5_SEQS):
        us = u_ref[s].astype(BF16)
        for kb in range(2):
            bu = jnp.dot(us[:, kb * 256:(kb + 1) * 256], bd_ref[0, kb],
                         preferred_element_type=F32)
            for j in range(16):
                bu_ref[kb * 16 + j, s * pitch:s * pitch + tl, :] = bu[:, j * 128:(j + 1) * 128]

    def step(l, carry):
        new = []
        for kb in range(2):
            for j in range(8):
                re = carry[kb * 16 + j]
                im = carry[kb * 16 + 8 + j]
                ar = arb_ref[kb * 8 + j]
                ai = aib_ref[kb * 8 + j]
                idx = pl.ds(l, S5_SEQS, stride=pitch)
                b_re = bu_ref[kb * 16 + j, idx, :]
                b_im = bu_ref[kb * 16 + 8 + j, idx, :]
                n_re = ar * re - ai * im + b_re
                n_im = ar * im + ai * re + b_im
                bu_ref[kb * 16 + j, idx, :] = n_re
                bu_ref[kb * 16 + 8 + j, idx, :] = n_im
                new.append((kb * 16 + j, n_re))
                new.append((kb * 16 + 8 + j, n_im))
        new.sort(key=lambda p: p[0])
        return tuple(p[1] for p in new)

    h0 = tuple(h_ref[i] for i in range(S5_SLABS))
    hf = lax.fori_loop(0, tl, step, h0)
    for i in range(S5_SLABS):
        h_ref[i] = hf[i]

    @pl.when(n == pl.num_programs(1) - 1)
    def _():
        for kb in range(2):
            for j in range(8):
                dst = slice(kb * 1024 + j * 128, kb * 1024 + (j + 1) * 128)
                or_ref[:, dst] = hf[kb * 16 + j]
                oi_ref[:, dst] = hf[kb * 16 + 8 + j]

    for s in range(S5_SEQS):
        ys = []
        for kb in range(2):
            hs = jnp.concatenate(
                [bu_ref[kb * 16 + j, s * pitch:s * pitch + tl, :] for j in range(16)], axis=1)
            ys.append(jnp.dot(hs.astype(BF16), cd_ref[0, kb], preferred_element_type=F32))
        y = jnp.concatenate(ys, axis=1) + d_ref[0] * u_ref[s]
        act = jax.nn.gelu(y)
        o_ref[s] = act * jax.nn.sigmoid(_bdot(act, glu_ref[0]))


def _s5(h3, h_re, h_im, layer_state, bd, cd, ar, ai, d_skip, w_glu, layer, tl):
    nseq, length, _ = h3.shape
    grid = (nseq // S5_SEQS, length // tl)
    kern = functools.partial(_s5_kernel, tl=tl)
    rows = S5_SEQS * _s5_pitch(tl)
    return pl.pallas_call(
        kern,
        grid=grid,
        in_specs=[
            pl.BlockSpec((S5_SEQS, tl, BRANCH), lambda b, n: (b, n, COL_S5 // BRANCH)),
            pl.BlockSpec((1, S5_SEQS, S5_CH), lambda b, n: (layer_state, b, 0)),
            pl.BlockSpec((1, S5_SEQS, S5_CH), lambda b, n: (layer_state, b, 0)),
            pl.BlockSpec((1, 2, 256, 2048), lambda b, n: (layer, 0, 0, 0)),
            pl.BlockSpec((1, 2, 2048, 256), lambda b, n: (layer, 0, 0, 0)),
            pl.BlockSpec((1, 1, S5_CH), lambda b, n: (layer, 0, 0)),
            pl.BlockSpec((1, 1, S5_CH), lambda b, n: (layer, 0, 0)),
            pl.BlockSpec((1, 1, BRANCH), lambda b, n: (layer, 0, 0)),
            pl.BlockSpec((1, BRANCH, BRANCH), lambda b, n: (layer, 0, 0)),
        ],
        out_specs=[
            pl.BlockSpec((S5_SEQS, tl, BRANCH), lambda b, n: (b, n, 0)),
            pl.BlockSpec((S5_SEQS, S5_CH), lambda b, n: (b, 0)),
            pl.BlockSpec((S5_SEQS, S5_CH), lambda b, n: (b, 0)),
        ],
        out_shape=[jax.ShapeDtypeStruct((nseq, length, BRANCH), F32),
                   jax.ShapeDtypeStruct((nseq, S5_CH), F32),
                   jax.ShapeDtypeStruct((nseq, S5_CH), F32)],
        scratch_shapes=[pltpu.VMEM((S5_SLABS, rows, LANES), F32),
                        pltpu.VMEM((S5_SLABS, S5_SEQS, LANES), F32),
                        pltpu.VMEM((S5_SLABS // 2, S5_SEQS, LANES), F32),
                        pltpu.VMEM((S5_SLABS // 2, S5_SEQS, LANES), F32)],
        compiler_params=_cparams("parallel", "arbitrary"),
        name="s5_ssm",
    )(h3, h_re, h_im, bd, cd, ar, ai, d_skip, w_glu)


def _merge_kernel(x_ref, ba_ref, bb_ref, bc_ref, bd_ref, g_ref, wb_ref, wo_ref, wq_ref, ln_ref,
                  x1_ref, q_ref):
    mixed = None
    for i, br in enumerate((ba_ref, bb_ref, bc_ref, bd_ref)):
        proj = jnp.dot(br[...].astype(BF16), wb_ref[0, i], preferred_element_type=F32)
        term = jax.nn.sigmoid(g_ref[:, i * D_MODEL:(i + 1) * D_MODEL]) * proj
        mixed = term if mixed is None else mixed + term
    y = jnp.dot(mixed.astype(BF16), wo_ref[0], preferred_element_type=F32)
    x1 = _layer_norm(DN_ALPHA * x_ref[...] + y, ln_ref[0, 0:1, :], ln_ref[0, 1:2, :])
    x1_ref[...] = x1
    q_ref[...] = jnp.dot(x1.astype(BF16), wq_ref[0], preferred_element_type=F32)


def _merge(x, h, brs, wb, wo, wq, ln, layer, tm):
    m = x.shape[0]
    tok = lambda w: pl.BlockSpec((tm, w), lambda i: (i, 0))
    return pl.pallas_call(
        _merge_kernel,
        grid=(m // tm,),
        in_specs=[tok(D_MODEL), tok(BRANCH), tok(BRANCH), tok(BRANCH), tok(BRANCH),
                  pl.BlockSpec((tm, 4096), lambda i: (i, COL_GATE // 4096)),
                  pl.BlockSpec((1, N_BRANCH, BRANCH, D_MODEL), lambda i: (layer, 0, 0, 0)),
                  pl.BlockSpec((1, D_MODEL, D_MODEL), lambda i: (layer, 0, 0)),
                  pl.BlockSpec((1, D_MODEL, D_MODEL), lambda i: (layer, 0, 0)),
                  pl.BlockSpec((1, 2, D_MODEL), lambda i: (layer, 0, 0))],
        out_specs=[tok(D_MODEL), tok(D_MODEL)],
        out_shape=[jax.ShapeDtypeStruct((m, D_MODEL), F32)] * 2,
        compiler_params=_cparams("parallel"),
        name="branch_merge",
    )(x, *brs, h, wb, wo, wq, ln)


def _attn_kernel(q_ref, k_ref, v_ref, o_ref):
    q = q_ref[0]
    k = k_ref[0, 0]
    v = v_ref[0, 0]
    outs = []
    for h in range(XA_HEADS):
        cols = slice(h * XA_DH, (h + 1) * XA_DH)
        s = _bdot_nt(q[:, cols], k[:, cols]) * (XA_DH ** -0.5)
        e = jnp.exp(s - jnp.max(s, -1, keepdims=True))
        p = e / jnp.sum(e, -1, keepdims=True)
        outs.append(_bdot(p, v[:, cols]))
    o_ref[0] = jnp.concatenate(outs, axis=1)


def _attention(q3, mem_k, mem_v, layer_mem, tq):
    nseq, length, _ = q3.shape
    mem = pl.BlockSpec((1, 1, MEM_LEN, D_MODEL), lambda b, i: (layer_mem, b, 0, 0))
    return pl.pallas_call(
        _attn_kernel,
        grid=(nseq, length // tq),
        in_specs=[pl.BlockSpec((1, tq, D_MODEL), lambda b, i: (b, i, 0)), mem, mem],
        out_specs=pl.BlockSpec((1, tq, D_MODEL), lambda b, i: (b, i, 0)),
        out_shape=jax.ShapeDtypeStruct((nseq, length, D_MODEL), F32),
        compiler_params=_cparams("parallel", "arbitrary"),
        name="memory_attention",
    )(q3, mem_k, mem_v)


def _post_kernel(x1_ref, o_ref, wo_ref, ln_ref, wr_ref, br_ref, x2_ref, comb_ref):
    y = jnp.dot(o_ref[...].astype(BF16), wo_ref[0], preferred_element_type=F32)
    x2 = _layer_norm(DN_ALPHA * x1_ref[...] + y, ln_ref[0, 0:1, :], ln_ref[0, 1:2, :])
    x2_ref[...] = x2
    logits = jnp.dot(x2.astype(BF16), wr_ref[...], preferred_element_type=F32) + br_ref[...]
    e = jnp.exp(logits - jnp.max(logits, -1, keepdims=True))
    p = e / jnp.sum(e, -1, keepdims=True)
    lane = lax.broadcasted_iota(jnp.int32, p.shape, 1)
    lanef = lane.astype(F32)
    grp = lane >> 2
    best = None
    for gidx in range(N_EXPERTS // EXPERTS_PER_GROUP):
        gm = jnp.max(jnp.where(grp == gidx, p, -1.0), -1, keepdims=True)
        if best is None:
            best, gi = gm, jnp.zeros(gm.shape, jnp.int32)
        else:
            upd = gm > best
            gi = jnp.where(upd, gidx, gi)
            best = jnp.where(upd, gm, best)
    cand = jnp.where(grp == gi, p, -1.0)
    m1 = jnp.max(cand, -1, keepdims=True)
    e1 = jnp.min(jnp.where(cand == m1, lanef, 1e9), -1, keepdims=True)
    cand2 = jnp.where(lanef == e1, -2.0, cand)
    m2 = jnp.max(cand2, -1, keepdims=True)
    e2 = jnp.min(jnp.where(cand2 == m2, lanef, 1e9), -1, keepdims=True)
    tot = m1 + m2
    comb_ref[...] = jnp.where(lanef == e1, m1 / tot, 0.0) + jnp.where(lanef == e2, m2 / tot, 0.0)


def _post(x1, o, wo, ln, wr, br, layer, tm):
    m = x1.shape[0]
    tok = lambda w: pl.BlockSpec((tm, w), lambda i: (i, 0))
    return pl.pallas_call(
        _post_kernel,
        grid=(m // tm,),
        in_specs=[tok(D_MODEL), tok(D_MODEL),
                  pl.BlockSpec((1, D_MODEL, D_MODEL), lambda i: (layer, 0, 0)),
                  pl.BlockSpec((1, 2, D_MODEL), lambda i: (layer, 0, 0)),
                  pl.BlockSpec((D_MODEL, LANES), lambda i: (0, 0)),
                  pl.BlockSpec((1, LANES), lambda i: (0, 0))],
        out_specs=[tok(D_MODEL), tok(LANES)],
        out_shape=[jax.ShapeDtypeStruct((m, D_MODEL), F32), jax.ShapeDtypeStruct((m, LANES), F32)],
        compiler_params=_cparams("parallel"),
        name="attn_out_router",
    )(x1, o, wo, ln, wr, br)


def _moe_kernel(x_ref, comb_ref, w13_ref, w2_ref, ln_ref, o_ref, acc_ref, xb_ref):
    e = pl.program_id(1)

    @pl.when(e == 0)
    def _():
        acc_ref[...] = jnp.zeros_like(acc_ref)
        xb_ref[...] = x_ref[...].astype(BF16)

    comb = comb_ref[...]
    lane = lax.broadcasted_iota(jnp.int32, comb.shape, 1)
    ce = jnp.sum(jnp.where(lane == e, comb, 0.0), -1, keepdims=True)
    h13 = jnp.dot(xb_ref[...], w13_ref[0, 0], preferred_element_type=F32)
    hid = _silu(h13[:, :D_EXPERT]) * h13[:, D_EXPERT:] * ce
    acc_ref[...] += jnp.dot(hid.astype(BF16), w2_ref[0, 0], preferred_element_type=F32)

    @pl.when(e == pl.num_programs(1) - 1)
    def _():
        o_ref[...] = _layer_norm(DN_ALPHA * x_ref[...] + acc_ref[...],
                                 ln_ref[0, 0:1, :], ln_ref[0, 1:2, :])


def _moe(x2, comb, w13, w2, ln, layer, tm):
    m = x2.shape[0]
    return pl.pallas_call(
        _moe_kernel,
        grid=(m // tm, N_EXPERTS),
        in_specs=[pl.BlockSpec((tm, D_MODEL), lambda i, e: (i, 0)),
                  pl.BlockSpec((tm, LANES), lambda i, e: (i, 0)),
                  pl.BlockSpec((1, 1, D_MODEL, 2 * D_EXPERT), lambda i, e: (layer, e, 0, 0)),
                  pl.BlockSpec((1, 1, D_EXPERT, D_MODEL), lambda i, e: (layer, e, 0, 0)),
                  pl.BlockSpec((1, 2, D_MODEL), lambda i, e: (layer, 0, 0))],
        out_specs=pl.BlockSpec((tm, D_MODEL), lambda i, e: (i, 0)),
        out_shape=jax.ShapeDtypeStruct((m, D_MODEL), F32),
        scratch_shapes=[pltpu.VMEM((tm, D_MODEL), F32), pltpu.VMEM((tm, D_MODEL), BF16)],
        compiler_params=_cparams("parallel", "arbitrary"),
        name="moe_experts",
    )(x2, comb, w13, w2, ln)


def _prep_w_in(w_in):
    o = np.cumsum((0, GDN_QKV, 512, 4, 4, 256, 256, 512, 512, 512, 512, 4096))
    seg = lambda i: w_in[:, :, o[i]:o[i + 1]]
    swap = np.arange(256) ^ 1
    main = jnp.concatenate([seg(0), seg(1), seg(4), seg(5), seg(4)[:, :, swap], seg(5)[:, :, swap],
                            seg(6), seg(7), seg(10), seg(8), seg(9)], axis=-1).astype(BF16)
    ba = jnp.concatenate([seg(2), seg(3), jnp.zeros((DEPTH, D_MODEL, LANES - 8), w_in.dtype)],
                         axis=-1).astype(BF16)
    return main, ba


def _rope_tables(pos):
    inv_freq = 1.0 / (ROPE_BASE ** jnp.linspace(0.0, 1.0, RET_DK // 2, dtype=F32))
    ang = pos.astype(F32)[:, None] * inv_freq
    cos = jnp.repeat(jnp.cos(ang), 2, axis=1)
    sin = jnp.repeat(jnp.sin(ang), 2, axis=1)
    sign = jnp.tile(jnp.array([-1.0, 1.0], F32), RET_DK // 2)
    return jnp.tile(cos, (1, RET_HEADS)), jnp.tile(sin * sign, (1, RET_HEADS))


def _prep_s5(a_re, a_im, log_dt, b_re, b_im, c_re, c_im):
    dt = jnp.exp(log_dt.astype(F32))[..., None]
    mag = jnp.exp(a_re * dt)
    ab_re, ab_im = mag * jnp.cos(a_im * dt), mag * jnp.sin(a_im * dt)
    den = a_re * a_re + a_im * a_im
    coef_re = ((ab_re - 1.0) * a_re + ab_im * a_im) / den
    coef_im = (ab_im * a_re - (ab_re - 1.0) * a_im) / den
    bb_re = coef_re[..., None] * b_re - coef_im[..., None] * b_im
    bb_im = coef_re[..., None] * b_im + coef_im[..., None] * b_re
    eye = jnp.eye(16, dtype=F32)

    def pack_b(bb):
        x = bb.reshape(DEPTH, 2, 16, S5_STATE, S5_GROUP)
        return jnp.einsum('dkgpc,gh->dkgchp', x, eye).reshape(DEPTH, 2, 256, 1024)

    def pack_c(cm):
        x = cm.reshape(DEPTH, 2, 16, S5_GROUP, S5_STATE)
        return jnp.einsum('dkgcp,gh->dkgphc', x, eye).reshape(DEPTH, 2, 1024, 256)

    bd = jnp.concatenate([pack_b(bb_re), pack_b(bb_im)], axis=-1).astype(BF16)
    cd = jnp.concatenate([pack_c(c_re.astype(F32)), -pack_c(c_im.astype(F32))], axis=-2).astype(BF16)
    ar = ab_re.reshape(DEPTH, 1, S5_CH)
    ai = ab_im.reshape(DEPTH, 1, S5_CH)
    return bd, cd, ar, ai


def kernel(x_prompt, x_sample, mem_prompt, state_gdn_conv, state_gdn, state_ret, state_pool,
           state_s5_re, state_s5_im, cache_mem_k, cache_mem_v, w_in, gdn_conv_w, gdn_a_log,
           gdn_dt_bias, gdn_norm_w, pool_w, pool_scale, s5_a_re, s5_a_im, s5_log_dt, s5_b_re,
           s5_b_im, s5_c_re, s5_c_im, s5_d, s5_w_glu, w_branch, w_out, xa_w_q, xa_w_k, xa_w_v,
           xa_w_o, ln_g, ln_b, w_router, b_router, moe_w1, moe_w3, moe_w2):
    bp, seq, _ = x_prompt.shape
    bs, dseq, _ = x_sample.shape
    past = 16384

    w_main, w_ba = _prep_w_in(w_in)
    gp = jnp.zeros((DEPTH, 2, LANES), F32)
    gp = gp.at[:, 0, 4:8].set(gdn_a_log.astype(F32)).at[:, 1, 4:8].set(gdn_dt_bias.astype(F32))
    nw = gdn_norm_w.astype(F32).reshape(DEPTH, 1, GDN_DK)
    cw = gdn_conv_w.astype(F32)
    pw = pool_w.astype(BF16)
    psc = pool_scale.astype(F32).reshape(DEPTH, 1, BRANCH)
    bd, cd, ar, ai = _prep_s5(s5_a_re.astype(F32), s5_a_im.astype(F32), s5_log_dt,
                              s5_b_re.astype(F32), s5_b_im.astype(F32), s5_c_re, s5_c_im)
    d_skip = s5_d.astype(F32).reshape(DEPTH, 1, BRANCH)
    glu = s5_w_glu.astype(BF16)
    wb = w_branch.astype(BF16)
    wo = w_out.astype(BF16)
    wq, wk, wv, wxo = (w.astype(BF16) for w in (xa_w_q, xa_w_k, xa_w_v, xa_w_o))
    ln = jnp.stack([ln_g.astype(F32), ln_b.astype(F32)], axis=2)
    ln1, ln2, ln3 = ln[:, 0], ln[:, 1], ln[:, 2]
    wr = jnp.concatenate([w_router, jnp.zeros((D_MODEL, LANES - N_EXPERTS), w_router.dtype)],
                         axis=1).astype(BF16)
    br = jnp.concatenate([b_router.astype(F32), jnp.full((LANES - N_EXPERTS,), -1e30, F32)])[None]
    w13 = jnp.concatenate([moe_w1, moe_w3], axis=-1).astype(BF16)
    w2 = moe_w2.astype(BF16)

    cos_p, sin_p = _rope_tables(jnp.arange(seq))
    cos_s, sin_s = _rope_tables(past + jnp.arange(dseq))
    ns_s = 16
    cos_s, sin_s = jnp.tile(cos_s, (ns_s, 1)), jnp.tile(sin_s, (ns_s, 1))

    zeros = lambda *s: jnp.zeros((1,) + s, F32)
    z_conv, z_gdn = zeros(bp, GDN_CONV - 1, GDN_QKV), zeros(bp, GDN_HEADS, 128, 128)
    z_ret, z_pool, z_s5 = zeros(bp, 256, RET_DV), zeros(bp, POOL_BUF, BRANCH), zeros(bp, S5_CH)
    st_ret = state_ret.reshape(DEPTH, bs, 256, RET_DV)
    st_s5r = state_s5_re.reshape(DEPTH, bs, S5_CH)
    st_s5i = state_s5_im.reshape(DEPTH, bs, S5_CH)
    mem2d = mem_prompt.reshape(bp * MEM_LEN, D_MODEL)
    ck = cache_mem_k.reshape(DEPTH, bs, MEM_LEN, D_MODEL)
    cv = cache_mem_v.reshape(DEPTH, bs, MEM_LEN, D_MODEL)

    def block(x2d, nseq, length, layer, mem_k, mem_v, layer_mem, states, layer_state, cfg):
        conv0, gdn0, ret0, pool0, s5r0, s5i0 = states
        tm = cfg["tm"]
        h = _matmul(x2d, w_main, layer, tm, 1024)
        hba = _matmul(x2d, w_ba, layer, tm, LANES)
        h3 = h.reshape(nseq, length, H_COLS)
        ba3 = hba.reshape(nseq, length, LANES)
        br_a, new_gdn = _gdn(h3, ba3, conv0, gdn0, layer_state, cw, gp, nw, layer,
                             cfg["ns"], cfg["gdn_sl"])
        br_b, new_ret = _retention(h3, cfg["cos"], cfg["sin"], ret0, layer_state,
                                   cfg["ns"], cfg["ret_sl"])
        br_c = _pool(h3, pool0, layer_state, pw, psc, layer, cfg["ns"], cfg["pool_sl"], cfg["pos0"])
        br_d, new_re, new_im = _s5(h3, s5r0, s5i0, layer_state, bd, cd, ar, ai, d_skip, glu,
                                   layer, cfg["s5_tl"])
        brs = [b.reshape(nseq * length, BRANCH) for b in (br_a, br_b, br_c, br_d)]
        x1, q = _merge(x2d, h, brs, wb, wo, wq, ln1, layer, cfg["tm_merge"])
        o = _attention(q.reshape(nseq, length, D_MODEL), mem_k, mem_v, layer_mem, cfg["tq"])
        x2, comb = _post(x1, o.reshape(nseq * length, D_MODEL), wxo, ln2, wr, br, layer,
                         cfg["tm_merge"])
        x3 = _moe(x2, comb, w13, w2, ln3, layer, tm)
        new_conv = h3[:, length - (GDN_CONV - 1):, COL_QKV:COL_QKV + GDN_QKV]
        pool_u = h3[:, :, COL_POOL:COL_POOL + BRANCH]
        return x3, (new_conv, new_gdn, new_ret.reshape(nseq, RET_HEADS, RET_DK, RET_DV), pool_u,
                    new_re.reshape(nseq, S5_GROUPS, S5_STATE), new_im.reshape(nseq, S5_GROUPS, S5_STATE))

    cfg_p = dict(tm=1024, tm_merge=256, ns=1, gdn_sl=64, ret_sl=128, pool_sl=512, s5_tl=128,
                 tq=512, pos0=0, cos=cos_p, sin=sin_p)
    cfg_s = dict(tm=1024, tm_merge=256, ns=ns_s, gdn_sl=dseq, ret_sl=dseq, pool_sl=dseq, s5_tl=dseq,
                 tq=dseq, pos0=past, cos=cos_s, sin=sin_s)

    yp = x_prompt.reshape(bp * seq, D_MODEL)
    ys = x_sample.reshape(bs * dseq, D_MODEL)
    p_out, s_out = [], []
    for l in range(DEPTH):
        mem_k = _matmul(mem2d, wk, l, 1024, 1024)
        mem_v = _matmul(mem2d, wv, l, 1024, 1024)
        mk4 = mem_k.reshape(1, bp, MEM_LEN, D_MODEL)
        mv4 = mem_v.reshape(1, bp, MEM_LEN, D_MODEL)
        yp, st = block(yp, bp, seq, l, mk4, mv4, 0,
                       (z_conv, z_gdn, z_ret, z_pool, z_s5, z_s5), 0, cfg_p)
        conv, gdn, ret, pool_u, s5r, s5i = st
        p_out.append((conv, gdn, ret, pool_u[:, seq - POOL_BUF:], s5r, s5i,
                      mem_k.reshape(bp, MEM_LEN, XA_HEADS, XA_DH),
                      mem_v.reshape(bp, MEM_LEN, XA_HEADS, XA_DH)))
        ys, st = block(ys, bs, dseq, l, ck, cv, l,
                       (state_gdn_conv, state_gdn, st_ret, state_pool, st_s5r, st_s5i), l, cfg_s)
        conv, gdn, ret, pool_u, s5r, s5i = st
        new_pool = jnp.concatenate([state_pool[l][:, dseq:], pool_u], axis=1)
        s_out.append((conv, gdn, ret, new_pool, s5r, s5i))
    p_st = [jnp.stack(t) for t in zip(*p_out)]
    s_st = [jnp.stack(t) for t in zip(*s_out)]
    return (yp.reshape(bp, seq, D_MODEL), ys.reshape(bs, dseq, D_MODEL), *p_st, *s_st)
```

```python
import functools
import math

import jax
import jax.numpy as jnp
import numpy as np
from jax import lax
from jax.experimental import pallas as pl
from jax.experimental.pallas import tpu as pltpu

F32 = jnp.float32
BF16 = jnp.bfloat16

D_MODEL = 1024
DEPTH = 4
BRANCH = 512
N_BRANCH = 4
GDN_HEADS = 4
GDN_DK = 128
GDN_QKV = 1536
GDN_CONV = 4
RET_HEADS = 4
RET_DK = 64
RET_DV = 128
ROPE_BASE = 10000.0
POOL_WINDOWS = (2, 4, 8, 16)
POOL_BUF = 15
S5_GROUPS = 32
S5_GROUP = 16
S5_STATE = 64
S5_CH = S5_GROUPS * S5_STATE
MEM_LEN = 256
XA_HEADS = 4
XA_DH = 256
N_EXPERTS = 16
EXPERTS_PER_GROUP = 4
D_EXPERT = 512
DN_ALPHA = (2.0 * DEPTH) ** 0.25
LN_EPS = 1e-5
RMS_EPS = 1e-6

LANES = 128
SUBLANES = 8
VMEM_LIMIT = 52 * 1024 * 1024

COL_QKV = 0
COL_Z = 1536
COL_RET = 2048
COL_GATE = 4096
COL_POOL = 8192
COL_S5 = 8704
H_COLS = 9216


def _cparams(*sem):
    return pltpu.CompilerParams(dimension_semantics=sem, vmem_limit_bytes=VMEM_LIMIT)


def _bdot(a, b):
    return jnp.dot(a.astype(BF16), b.astype(BF16), preferred_element_type=F32)


def _bdot_nt(a, b):
    return lax.dot_general(a.astype(BF16), b.astype(BF16), (((1,), (1,)), ((), ())),
                           preferred_element_type=F32)


def _silu(x):
    return x * jax.nn.sigmoid(x)


def _layer_norm(x, g, b):
    mu = jnp.mean(x, -1, keepdims=True)
    xc = x - mu
    var = jnp.mean(xc * xc, -1, keepdims=True)
    return xc * lax.rsqrt(var + LN_EPS) * g + b


def _mm_kernel(x_ref, w_ref, o_ref, xb_ref):
    @pl.when(pl.program_id(1) == 0)
    def _():
        xb_ref[...] = x_ref[...].astype(BF16)

    o_ref[...] = jnp.dot(xb_ref[...], w_ref[0], preferred_element_type=F32)


def _matmul(x, w, layer, tm, tn):
    m, k = x.shape
    n = w.shape[-1]
    return pl.pallas_call(
        _mm_kernel,
        grid=(m // tm, n // tn),
        in_specs=[pl.BlockSpec((tm, k), lambda i, j: (i, 0)),
                  pl.BlockSpec((1, k, tn), lambda i, j: (layer, 0, j))],
        out_specs=pl.BlockSpec((tm, tn), lambda i, j: (i, j)),
        out_shape=jax.ShapeDtypeStruct((m, n), F32),
        scratch_shapes=[pltpu.VMEM((tm, k), BF16)],
        compiler_params=_cparams("parallel", "arbitrary"),
        name="token_matmul",
    )(x, w)


def _inproj_kernel(x_ref, w_ref, wba_ref, o_ref, ba_ref, xb_ref):
    @pl.when(pl.program_id(1) == 0)
    def _():
        xb_ref[...] = x_ref[...].astype(BF16)
        ba_ref[...] = jnp.dot(xb_ref[...], wba_ref[0], preferred_element_type=F32)

    o_ref[...] = jnp.dot(xb_ref[...], w_ref[0], preferred_element_type=F32)


def _in_proj(x, w, w_ba, layer, tm, tn):
    m, k = x.shape
    n = w.shape[-1]
    return pl.pallas_call(
        _inproj_kernel,
        grid=(m // tm, n // tn),
        in_specs=[pl.BlockSpec((tm, k), lambda i, j: (i, 0)),
                  pl.BlockSpec((1, k, tn), lambda i, j: (layer, 0, j)),
                  pl.BlockSpec((1, k, LANES), lambda i, j: (layer, 0, 0))],
        out_specs=[pl.BlockSpec((tm, tn), lambda i, j: (i, j)),
                   pl.BlockSpec((tm, LANES), lambda i, j: (i, 0))],
        out_shape=[jax.ShapeDtypeStruct((m, n), F32), jax.ShapeDtypeStruct((m, LANES), F32)],
        scratch_shapes=[pltpu.VMEM((tm, k), BF16)],
        compiler_params=_cparams("parallel", "arbitrary"),
        name="in_proj",
    )(x, w, w_ba)


def _branch_out_spec(ns, sl, n_chunks):
    return pl.BlockSpec((ns * sl, BRANCH), lambda b, n: (b * n_chunks + n, 0))


def _seg_ids(c, sl):
    r = lax.broadcasted_iota(jnp.int32, (c, 1), 0)
    sh = int(math.log2(sl))
    return r, r & (sl - 1), r >> sh


def _gdn_kernel(qkv_ref, z_ref, ba_ref, cs_ref, s0_ref, cw_ref, gp_ref, nw_ref,
                o_ref, s_ref, full_ref, *, nb, ns, rl, sl):
    n = pl.program_id(1)

    @pl.when(n == 0)
    def _():
        full_ref[:, 5:8, :] = cs_ref[0]
        s_ref[...] = s0_ref[0]

    for bi in range(nb):
        _gdn_block(qkv_ref, z_ref, ba_ref, cw_ref, gp_ref, nw_ref, o_ref, s_ref, full_ref,
                   bi=bi, ns=ns, rl=rl, sl=sl)


def _gdn_block(qkv_ref, z_ref, ba_ref, cw_ref, gp_ref, nw_ref, o_ref, s_ref, full_ref,
               *, bi, ns, rl, sl):
    c = ns * rl
    nsps = rl // sl
    sq = slice(bi * ns, (bi + 1) * ns)

    u3 = qkv_ref[sq]
    full_ref[sq, 8:8 + rl, :] = u3
    cw = cw_ref[0]
    acc = (cw[3:4] * u3 + cw[2:3] * full_ref[sq, 7:7 + rl, :] + cw[1:2] * full_ref[sq, 6:6 + rl, :]
           + cw[0:1] * full_ref[sq, 5:5 + rl, :])
    full_ref[sq, 5:8, :] = full_ref[sq, 5 + rl:8 + rl, :]
    qkv = _silu(acc).reshape(c, GDN_QKV)

    ba = ba_ref[sq].reshape(c, LANES)
    beta_t = jax.nn.sigmoid(ba)
    xs = ba + gp_ref[0, 1:2, :]
    softplus = jnp.maximum(xs, 0.0) + jnp.log1p(jnp.exp(-jnp.abs(xs)))
    g_t = -jnp.exp(gp_ref[0, 0:1, :]) * softplus

    r, t, seq = _seg_ids(c, sl)
    gc = g_t
    s = 1
    while s < sl:
        gc = gc + jnp.where(t >= s, pltpu.roll(gc, s, 0), 0.0)
        s *= 2
    tot = gc
    s = 1
    while s < sl:
        tot = jnp.where(t + s < sl, pltpu.roll(tot, c - s, 0), tot)
        s *= 2
    rc = tot - gc
    if c < LANES:
        gc_pad = jnp.concatenate([gc, jnp.zeros((LANES - c, LANES), F32)], axis=0)
    else:
        gc_pad = gc
    gc_t = gc_pad.T

    rr = lax.broadcasted_iota(jnp.int32, (c, c), 0)
    cc = lax.broadcasted_iota(jnp.int32, (c, c), 1)
    sh = int(math.log2(sl))
    same = (rr >> sh) == (cc >> sh)
    causal = jnp.logical_and(same, rr >= cc)
    strict = jnp.logical_and(same, rr > cc)
    eye = (rr == cc).astype(F32)

    z = z_ref[sq].reshape(c, BRANCH)
    nw = nw_ref[0]
    outs = []
    for h in range(GDN_HEADS):
        q = qkv[:, h * 128:(h + 1) * 128]
        k = qkv[:, 512 + h * 128:512 + (h + 1) * 128]
        v = qkv[:, 1024 + h * 128:1024 + (h + 1) * 128]
        q = q * lax.rsqrt(jnp.sum(q * q, -1, keepdims=True) + RMS_EPS) * (GDN_DK ** -0.5)
        k = k * lax.rsqrt(jnp.sum(k * k, -1, keepdims=True) + RMS_EPS)
        beta = beta_t[:, h:h + 1]
        gcc = gc[:, 4 + h:5 + h]
        gcr = gc_t[4 + h:5 + h, 0:c]
        rcc = rc[:, 4 + h:5 + h]
        decay = jnp.where(causal, jnp.exp(jnp.where(causal, gcc - gcr, 0.0)), 0.0)
        kb = k * beta
        a = jnp.where(strict, _bdot_nt(kb, k) * decay, 0.0)
        npow = -a
        sm = eye + npow
        if sl > 2:
            npow = _bdot(npow, npow)
            m = 2
            while 2 * m < sl:
                prod = _bdot(npow, jnp.concatenate([npow, sm], axis=1))
                npow = prod[:, :c]
                sm = sm + prod[:, c:]
                m *= 2
            sm = sm + _bdot(npow, sm)
        eg = jnp.exp(gcc)
        rhs = jnp.concatenate([v * beta, kb * eg], axis=1)
        sol = _bdot(sm, rhs)
        u_c = sol[:, :128]
        w_c = sol[:, 128:]
        attn = jnp.where(causal, _bdot_nt(q, k) * decay, 0.0)
        q_dec = q * eg
        k_dec_t = (k * jnp.exp(rcc)).T
        v_parts, o_parts = [], []
        for si in range(ns):
            st = s_ref[bi * ns + si, h]
            for j in range(nsps):
                sg = si * nsps + j
                rows = slice(sg * sl, (sg + 1) * sl)
                v_j = u_c[rows] - _bdot(w_c[rows], st)
                o_parts.append(_bdot(q_dec[rows], st))
                v_parts.append(v_j)
                pieces = [v_j]
                if sg > 0:
                    pieces.insert(0, jnp.zeros((sg * sl, 128), F32))
                if (sg + 1) * sl < c:
                    pieces.append(jnp.zeros((c - (sg + 1) * sl, 128), F32))
                vz = v_j if len(pieces) == 1 else jnp.concatenate(pieces, axis=0)
                last = jnp.exp(tot[sg * sl:sg * sl + 1, 4 + h:5 + h])
                st = st * last + _bdot(k_dec_t, vz)
            s_ref[bi * ns + si, h] = st
        v_new = v_parts[0] if len(v_parts) == 1 else jnp.concatenate(v_parts, axis=0)
        o_cross = o_parts[0] if len(o_parts) == 1 else jnp.concatenate(o_parts, axis=0)
        o = o_cross + _bdot(attn, v_new)
        o = o * lax.rsqrt(jnp.mean(o * o, -1, keepdims=True) + RMS_EPS) * nw
        outs.append(o * _silu(z[:, h * 128:(h + 1) * 128]))
    o_ref[bi] = jnp.concatenate(outs, axis=1).astype(o_ref.dtype)


def _gdn(h3, ba3, conv_state, s0, layer_state, cw, gp, nw, layer, nb, ns, rl, sl):
    nseq, length, _ = h3.shape
    assert ns == 1 or rl == length
    nq = nb * ns
    grid = (nseq // nq, length // rl)
    kern = functools.partial(_gdn_kernel, nb=nb, ns=ns, rl=rl, sl=sl)
    branch, new_state = pl.pallas_call(
        kern,
        grid=grid,
        in_specs=[
            pl.BlockSpec((nq, rl, GDN_QKV), lambda b, n: (b, n, COL_QKV // GDN_QKV)),
            pl.BlockSpec((nq, rl, BRANCH), lambda b, n: (b, n, COL_Z // BRANCH)),
            pl.BlockSpec((nq, rl, LANES), lambda b, n: (b, n, 0)),
            pl.BlockSpec((1, nq, GDN_CONV - 1, GDN_QKV), lambda b, n: (layer_state, b, 0, 0)),
            pl.BlockSpec((1, nq, GDN_HEADS, 128, 128), lambda b, n: (layer_state, b, 0, 0, 0)),
            pl.BlockSpec((1, GDN_CONV, GDN_QKV), lambda b, n: (layer, 0, 0)),
            pl.BlockSpec((1, 2, LANES), lambda b, n: (layer, 0, 0)),
            pl.BlockSpec((1, 1, LANES), lambda b, n: (layer, 0, 0)),
        ],
        out_specs=[
            pl.BlockSpec((nb, ns * rl, BRANCH), lambda b, n: (b, n, 0)),
            pl.BlockSpec((nq, GDN_HEADS, 128, 128), lambda b, n: (b, 0, 0, 0)),
        ],
        out_shape=[jax.ShapeDtypeStruct((nseq // ns, ns * length, BRANCH), BF16),
                   jax.ShapeDtypeStruct((nseq, GDN_HEADS, 128, 128), F32)],
        scratch_shapes=[pltpu.VMEM((nq, SUBLANES + rl, GDN_QKV), F32)],
        compiler_params=_cparams("parallel", "arbitrary"),
        name="gated_deltanet",
    )(h3, h3, ba3, conv_state, s0, cw, gp, nw)
    return branch.reshape(nseq * length, BRANCH), new_state


_RET_LOG_GAMMA = tuple(math.log(1.0 - 2.0 ** (-5.0 - h)) for h in range(RET_HEADS))


def _per_head(idx, vals):
    out = jnp.full(idx.shape, vals[3], F32)
    for h in (2, 1, 0):
        out = jnp.where(idx < (h + 1) * RET_DK, vals[h], out)
    return out


def _ret_kernel(hb_ref, cos_ref, sin_ref, s0_ref, o_ref, s_ref, *, ns, sl):
    c = ns * sl
    n = pl.program_id(1)

    @pl.when(n == 0)
    def _():
        s_ref[...] = s0_ref[0]

    hb = hb_ref[...].reshape(c, 2048)
    cos = cos_ref[...]
    sin = sin_ref[...]
    rq = (hb[:, 0:256] * cos + hb[:, 512:768] * sin) * (RET_DK ** -0.5)
    rk = hb[:, 256:512] * cos + hb[:, 768:1024] * sin
    v = hb[:, 1024:1536]
    g = hb[:, 1536:2048]

    r, t, seq = _seg_ids(c, sl)
    tf = t.astype(F32)
    lane = lax.broadcasted_iota(jnp.int32, (1, 256), 1)
    lgl = _per_head(lane, _RET_LOG_GAMMA)
    q_dec = rq * jnp.exp(lgl * (tf + 1.0))
    k_dec = rk * jnp.exp(lgl * (sl - 1.0 - tf))
    k_dec_t = k_dec.T

    rr = lax.broadcasted_iota(jnp.int32, (c, c), 0)
    cc = lax.broadcasted_iota(jnp.int32, (c, c), 1)
    sh = int(math.log2(sl))
    causal = jnp.logical_and((rr >> sh) == (cc >> sh), rr >= cc)
    rel = jnp.maximum(rr - cc, 0).astype(F32)
    cseq = lax.broadcasted_iota(jnp.int32, (1, c), 1) >> sh
    rowi = lax.broadcasted_iota(jnp.int32, (256, 1), 0)
    cd_rows = jnp.exp(_per_head(rowi, _RET_LOG_GAMMA) * float(sl))

    cross = []
    for si in range(ns):
        rows = slice(si * sl, (si + 1) * sl)
        st = s_ref[si]
        parts = []
        for h in range(RET_HEADS):
            mh = jnp.logical_and(lane >= h * RET_DK, lane < (h + 1) * RET_DK)
            parts.append(_bdot(jnp.where(mh, q_dec[rows], 0.0), st))
        cross.append(parts)
        kd = k_dec_t if ns == 1 else jnp.where(cseq == si, k_dec_t, 0.0)
        res = _bdot(kd, v)
        upd = jnp.zeros((256, RET_DV), F32)
        for h in range(RET_HEADS):
            rm = jnp.logical_and(rowi >= h * RET_DK, rowi < (h + 1) * RET_DK)
            upd = upd + jnp.where(rm, res[:, h * 128:(h + 1) * 128], 0.0)
        s_ref[si] = st * cd_rows + upd

    outs = []
    for h in range(RET_HEADS):
        mh = jnp.logical_and(lane >= h * RET_DK, lane < (h + 1) * RET_DK)
        dmask = jnp.where(causal, jnp.exp(_RET_LOG_GAMMA[h] * rel), 0.0)
        sc = _bdot_nt(jnp.where(mh, rq, 0.0), rk) * dmask
        o = _bdot(sc, v[:, h * 128:(h + 1) * 128])
        oc = cross[0][h] if ns == 1 else jnp.concatenate([cross[si][h] for si in range(ns)], axis=0)
        o = o + oc
        mu = jnp.mean(o, -1, keepdims=True)
        oc2 = o - mu
        o = oc2 * lax.rsqrt(jnp.mean(oc2 * oc2, -1, keepdims=True) + LN_EPS)
        outs.append(_silu(g[:, h * 128:(h + 1) * 128]) * o)
    o_ref[...] = jnp.concatenate(outs, axis=1).astype(o_ref.dtype)


def _retention(h3, cos_tab, sin_tab, s0, layer_state, ns, sl):
    nseq, length, _ = h3.shape
    c = ns * sl
    grid = (nseq // ns, length // sl)
    kern = functools.partial(_ret_kernel, ns=ns, sl=sl)
    return pl.pallas_call(
        kern,
        grid=grid,
        in_specs=[
            pl.BlockSpec((ns, sl, 2048), lambda b, n: (b, n, COL_RET // 2048)),
            pl.BlockSpec((c, 256), lambda b, n: (n, 0)),
            pl.BlockSpec((c, 256), lambda b, n: (n, 0)),
            pl.BlockSpec((1, ns, 256, RET_DV), lambda b, n: (layer_state, b, 0, 0)),
        ],
        out_specs=[
            _branch_out_spec(ns, sl, grid[1]),
            pl.BlockSpec((ns, 256, RET_DV), lambda b, n: (b, 0, 0)),
        ],
        out_shape=[jax.ShapeDtypeStruct((nseq * length, BRANCH), BF16),
                   jax.ShapeDtypeStruct((nseq, 256, RET_DV), F32)],
        compiler_params=_cparams("parallel", "arbitrary"),
        name="retention",
    )(h3, cos_tab, sin_tab, s0)


def _pool_kernel(u_ref, st_ref, w_ref, sc_ref, o_ref, full_ref, *, ns, sl, pos0, carry):
    c = ns * sl
    n = pl.program_id(1)

    @pl.when(n == 0)
    def _():
        full_ref[:, 1:16, :] = st_ref[0]

    u3 = u_ref[...]
    full_ref[:, 16:16 + sl, :] = u3
    tpos = lax.broadcasted_iota(jnp.int32, (1, sl, 1), 1) + n * sl
    n_avail = (tpos + (pos0 + 1)).astype(F32)
    outs = []
    for gi, w in enumerate(POOL_WINDOWS):
        cols = slice(gi * 128, (gi + 1) * 128)
        ug = u3[:, :, cols]
        acc = ug
        for j in range(1, w):
            acc = acc + full_ref[:, 16 - j:16 - j + sl, cols]
        pooled = acc / jnp.minimum(n_avail, float(w)) - ug
        mixed = _bdot(pooled.reshape(c, 128), w_ref[0, gi])
        outs.append(mixed * sc_ref[0, :, cols])
    if carry:
        full_ref[:, 1:16, :] = full_ref[:, sl + 1:sl + 16, :]
    o_ref[...] = jnp.concatenate(outs, axis=1).astype(o_ref.dtype)


def _pool(h3, state, layer_state, w_pool, pool_scale, layer, ns, sl, pos0):
    nseq, length, _ = h3.shape
    grid = (nseq // ns, length // sl)
    kern = functools.partial(_pool_kernel, ns=ns, sl=sl, pos0=pos0, carry=grid[1] > 1)
    return pl.pallas_call(
        kern,
        grid=grid,
        in_specs=[
            pl.BlockSpec((ns, sl, BRANCH), lambda b, n: (b, n, COL_POOL // BRANCH)),
            pl.BlockSpec((1, ns, POOL_BUF, BRANCH), lambda b, n: (layer_state, b, 0, 0)),
            pl.BlockSpec((1, 4, 128, 128), lambda b, n: (layer, 0, 0, 0)),
            pl.BlockSpec((1, 1, BRANCH), lambda b, n: (layer, 0, 0)),
        ],
        out_specs=_branch_out_spec(ns, sl, grid[1]),
        out_shape=jax.ShapeDtypeStruct((nseq * length, BRANCH), BF16),
        scratch_shapes=[pltpu.VMEM((ns, 16 + sl, BRANCH), F32)],
        compiler_params=_cparams("parallel", "arbitrary"),
        name="multi_pool",
    )(h3, state, w_pool, pool_scale)


S5_SEQS = SUBLANES
S5_SLABS = 2 * S5_CH // LANES


def _s5_pitch(tl):
    return tl + 4


def _s5_kernel(u_ref, hr_ref, hi_ref, bd_ref, cd_ref, ar_ref, ai_ref, d_ref, glu_ref,
               o_ref, or_ref, oi_ref, bu_ref, h_ref, arb_ref, aib_ref, up_ref, *, tl):
    pitch = _s5_pitch(tl)
    n = pl.program_id(1)

    @pl.when(n == 0)
    def _():
        hr = hr_ref[0]
        hi = hi_ref[0]
        for kb in range(2):
            for j in range(8):
                src = slice(kb * 1024 + j * 128, kb * 1024 + (j + 1) * 128)
                h_ref[kb * 16 + j] = hr[:, src]
                h_ref[kb * 16 + 8 + j] = hi[:, src]
        for kb in range(2):
            for j in range(8):
                src = slice(kb * 1024 + j * 128, kb * 1024 + (j + 1) * 128)
                arb_ref[kb * 8 + j] = jnp.broadcast_to(ar_ref[0, :, src], (S5_SEQS, LANES))
                aib_ref[kb * 8 + j] = jnp.broadcast_to(ai_ref[0, :, src], (S5_SEQS, LANES))

        up_ref[...] = jnp.zeros_like(up_ref)

    for s in range(S5_SEQS):
        up_ref[s * pitch:s * pitch + tl, :] = u_ref[s]
    up = up_ref[...]
    upb = up.astype(BF16)
    for kb in range(2):
        bu = jnp.dot(upb[:, kb * 256:(kb + 1) * 256], bd_ref[0, kb],
                     preferred_element_type=F32)
        for j in range(16):
            bu_ref[kb * 16 + j] = bu[:, j * 128:(j + 1) * 128]

    def step(l, carry):
        new = []
        for kb in range(2):
            for j in range(8):
                re = carry[kb * 16 + j]
                im = carry[kb * 16 + 8 + j]
                ar = arb_ref[kb * 8 + j]
                ai = aib_ref[kb * 8 + j]
                idx = pl.ds(l, S5_SEQS, stride=pitch)
                b_re = bu_ref[kb * 16 + j, idx, :]
                b_im = bu_ref[kb * 16 + 8 + j, idx, :]
                n_re = ar * re - ai * im + b_re
                n_im = ar * im + ai * re + b_im
                bu_ref[kb * 16 + j, idx, :] = n_re
                bu_ref[kb * 16 + 8 + j, idx, :] = n_im
                new.append((kb * 16 + j, n_re))
                new.append((kb * 16 + 8 + j, n_im))
        new.sort(key=lambda p: p[0])
        return tuple(p[1] for p in new)

    h0 = tuple(h_ref[i] for i in range(S5_SLABS))
    hf = lax.fori_loop(0, tl, step, h0)
    for i in range(S5_SLABS):
        h_ref[i] = hf[i]

    @pl.when(n == pl.num_programs(1) - 1)
    def _():
        for kb in range(2):
            for j in range(8):
                dst = slice(kb * 1024 + j * 128, kb * 1024 + (j + 1) * 128)
                or_ref[:, dst] = hf[kb * 16 + j]
                oi_ref[:, dst] = hf[kb * 16 + 8 + j]

    ys = []
    for kb in range(2):
        hs = jnp.concatenate([bu_ref[kb * 16 + j].astype(BF16) for j in range(16)], axis=1)
        ys.append(jnp.dot(hs, cd_ref[0, kb], preferred_element_type=F32))
    y = jnp.concatenate(ys, axis=1) + d_ref[0] * up
    act = jax.nn.gelu(y)
    out = act * jax.nn.sigmoid(_bdot(act, glu_ref[0]))
    for s in range(S5_SEQS):
        o_ref[s] = out[s * pitch:s * pitch + tl, :]


def _s5(h3, h_re, h_im, layer_state, bd, cd, ar, ai, d_skip, w_glu, layer, tl):
    nseq, length, _ = h3.shape
    grid = (nseq // S5_SEQS, length // tl)
    kern = functools.partial(_s5_kernel, tl=tl)
    rows = S5_SEQS * _s5_pitch(tl)
    return pl.pallas_call(
        kern,
        grid=grid,
        in_specs=[
            pl.BlockSpec((S5_SEQS, tl, BRANCH), lambda b, n: (b, n, COL_S5 // BRANCH)),
            pl.BlockSpec((1, S5_SEQS, S5_CH), lambda b, n: (layer_state, b, 0)),
            pl.BlockSpec((1, S5_SEQS, S5_CH), lambda b, n: (layer_state, b, 0)),
            pl.BlockSpec((1, 2, 256, 2048), lambda b, n: (layer, 0, 0, 0)),
            pl.BlockSpec((1, 2, 2048, 256), lambda b, n: (layer, 0, 0, 0)),
            pl.BlockSpec((1, 1, S5_CH), lambda b, n: (layer, 0, 0)),
            pl.BlockSpec((1, 1, S5_CH), lambda b, n: (layer, 0, 0)),
            pl.BlockSpec((1, 1, BRANCH), lambda b, n: (layer, 0, 0)),
            pl.BlockSpec((1, BRANCH, BRANCH), lambda b, n: (layer, 0, 0)),
        ],
        out_specs=[
            pl.BlockSpec((S5_SEQS, tl, BRANCH), lambda b, n: (b, n, 0)),
            pl.BlockSpec((S5_SEQS, S5_CH), lambda b, n: (b, 0)),
            pl.BlockSpec((S5_SEQS, S5_CH), lambda b, n: (b, 0)),
        ],
        out_shape=[jax.ShapeDtypeStruct((nseq, length, BRANCH), F32),
                   jax.ShapeDtypeStruct((nseq, S5_CH), F32),
                   jax.ShapeDtypeStruct((nseq, S5_CH), F32)],
        scratch_shapes=[pltpu.VMEM((S5_SLABS, rows, LANES), F32),
                        pltpu.VMEM((S5_SLABS, S5_SEQS, LANES), F32),
                        pltpu.VMEM((S5_SLABS // 2, S5_SEQS, LANES), F32),
                        pltpu.VMEM((S5_SLABS // 2, S5_SEQS, LANES), F32),
                        pltpu.VMEM((rows, BRANCH), F32)],
        compiler_params=_cparams("parallel", "arbitrary"),
        name="s5_ssm",
    )(h3, h_re, h_im, bd, cd, ar, ai, d_skip, w_glu)


def _merge_kernel(x_ref, ba_ref, bb_ref, bc_ref, bd_ref, g_ref, wb_ref, wo_ref, wq_ref, ln_ref,
                  x1_ref, q_ref):
    mixed = None
    for i, br in enumerate((ba_ref, bb_ref, bc_ref, bd_ref)):
        proj = jnp.dot(br[...].astype(BF16), wb_ref[0, i], preferred_element_type=F32)
        term = jax.nn.sigmoid(g_ref[:, i * D_MODEL:(i + 1) * D_MODEL]) * proj
        mixed = term if mixed is None else mixed + term
    y = jnp.dot(mixed.astype(BF16), wo_ref[0], preferred_element_type=F32)
    x1 = _layer_norm(DN_ALPHA * x_ref[...] + y, ln_ref[0, 0:1, :], ln_ref[0, 1:2, :])
    x1_ref[...] = x1
    q_ref[...] = jnp.dot(x1.astype(BF16), wq_ref[0], preferred_element_type=F32)


def _merge(x, h, brs, wb, wo, wq, ln, layer, tm):
    m = x.shape[0]
    tok = lambda w: pl.BlockSpec((tm, w), lambda i: (i, 0))
    return pl.pallas_call(
        _merge_kernel,
        grid=(m // tm,),
        in_specs=[tok(D_MODEL), tok(BRANCH), tok(BRANCH), tok(BRANCH), tok(BRANCH),
                  pl.BlockSpec((tm, 4096), lambda i: (i, COL_GATE // 4096)),
                  pl.BlockSpec((1, N_BRANCH, BRANCH, D_MODEL), lambda i: (layer, 0, 0, 0)),
                  pl.BlockSpec((1, D_MODEL, D_MODEL), lambda i: (layer, 0, 0)),
                  pl.BlockSpec((1, D_MODEL, D_MODEL), lambda i: (layer, 0, 0)),
                  pl.BlockSpec((1, 2, D_MODEL), lambda i: (layer, 0, 0))],
        out_specs=[tok(D_MODEL), tok(D_MODEL)],
        out_shape=[jax.ShapeDtypeStruct((m, D_MODEL), F32)] * 2,
        compiler_params=_cparams("parallel"),
        name="branch_merge",
    )(x, *brs, h, wb, wo, wq, ln)


def _attn_kernel(q_ref, k_ref, v_ref, o_ref):
    q = q_ref[0]
    outs = []
    for h in range(XA_HEADS):
        cols = slice(h * XA_DH, (h + 1) * XA_DH)
        s = _bdot_nt(q[:, cols], k_ref[0, 0, :, cols]) * (XA_DH ** -0.5)
        e = jnp.exp(s - jnp.max(s, -1, keepdims=True))
        p = e / jnp.sum(e, -1, keepdims=True)
        outs.append(_bdot(p, v_ref[0, 0, :, cols]))
    o_ref[0] = jnp.concatenate(outs, axis=1)


def _attention(q3, mem_k, mem_v, layer_mem, tq):
    nseq, length, _ = q3.shape
    mem = pl.BlockSpec((1, 1, MEM_LEN, D_MODEL), lambda b, i: (layer_mem, b, 0, 0))
    return pl.pallas_call(
        _attn_kernel,
        grid=(nseq, length // tq),
        in_specs=[pl.BlockSpec((1, tq, D_MODEL), lambda b, i: (b, i, 0)), mem, mem],
        out_specs=pl.BlockSpec((1, tq, D_MODEL), lambda b, i: (b, i, 0)),
        out_shape=jax.ShapeDtypeStruct((nseq, length, D_MODEL), F32),
        compiler_params=_cparams("parallel", "arbitrary"),
        name="memory_attention",
    )(q3, mem_k, mem_v)


def _post_kernel(x1_ref, o_ref, wo_ref, ln_ref, wr_ref, br_ref, x2_ref, comb_ref):
    y = jnp.dot(o_ref[...].astype(BF16), wo_ref[0], preferred_element_type=F32)
    x2 = _layer_norm(DN_ALPHA * x1_ref[...] + y, ln_ref[0, 0:1, :], ln_ref[0, 1:2, :])
    x2_ref[...] = x2
    logits = jnp.dot(x2.astype(BF16), wr_ref[...], preferred_element_type=F32) + br_ref[...]
    e = jnp.exp(logits - jnp.max(logits, -1, keepdims=True))
    p = e / jnp.sum(e, -1, keepdims=True)
    lane = lax.broadcasted_iota(jnp.int32, p.shape, 1)
    lanef = lane.astype(F32)
    grp = lane >> 2
    best = None
    for gidx in range(N_EXPERTS // EXPERTS_PER_GROUP):
        gm = jnp.max(jnp.where(grp == gidx, p, -1.0), -1, keepdims=True)
        if best is None:
            best, gi = gm, jnp.zeros(gm.shape, jnp.int32)
        else:
            upd = gm > best
            gi = jnp.where(upd, gidx, gi)
            best = jnp.where(upd, gm, best)
    cand = jnp.where(grp == gi, p, -1.0)
    m1 = jnp.max(cand, -1, keepdims=True)
    e1 = jnp.min(jnp.where(cand == m1, lanef, 1e9), -1, keepdims=True)
    cand2 = jnp.where(lanef == e1, -2.0, cand)
    m2 = jnp.max(cand2, -1, keepdims=True)
    e2 = jnp.min(jnp.where(cand2 == m2, lanef, 1e9), -1, keepdims=True)
    tot = m1 + m2
    comb_ref[...] = jnp.where(lanef == e1, m1 / tot, 0.0) + jnp.where(lanef == e2, m2 / tot, 0.0)


def _post(x1, o, wo, ln, wr, br, layer, tm):
    m = x1.shape[0]
    tok = lambda w: pl.BlockSpec((tm, w), lambda i: (i, 0))
    return pl.pallas_call(
        _post_kernel,
        grid=(m // tm,),
        in_specs=[tok(D_MODEL), tok(D_MODEL),
                  pl.BlockSpec((1, D_MODEL, D_MODEL), lambda i: (layer, 0, 0)),
                  pl.BlockSpec((1, 2, D_MODEL), lambda i: (layer, 0, 0)),
                  pl.BlockSpec((D_MODEL, LANES), lambda i: (0, 0)),
                  pl.BlockSpec((1, LANES), lambda i: (0, 0))],
        out_specs=[tok(D_MODEL), tok(LANES)],
        out_shape=[jax.ShapeDtypeStruct((m, D_MODEL), F32), jax.ShapeDtypeStruct((m, LANES), F32)],
        compiler_params=_cparams("parallel"),
        name="attn_out_router",
    )(x1, o, wo, ln, wr, br)


def _moe_kernel(x_ref, comb_ref, w1_ref, w3_ref, w2_ref, ln_ref, o_ref, acc_ref, xb_ref):
    e = pl.program_id(1)

    @pl.when(e == 0)
    def _():
        acc_ref[...] = jnp.zeros_like(acc_ref)
        xb_ref[...] = x_ref[...].astype(BF16)

    comb = comb_ref[...]
    lane = lax.broadcasted_iota(jnp.int32, comb.shape, 1)
    ce = jnp.sum(jnp.where(lane == e, comb, 0.0), -1, keepdims=True)
    xb = xb_ref[...]
    h1 = jnp.dot(xb, w1_ref[0, 0].astype(BF16), preferred_element_type=F32)
    h3 = jnp.dot(xb, w3_ref[0, 0].astype(BF16), preferred_element_type=F32)
    hid = _silu(h1) * h3 * ce
    acc_ref[...] += jnp.dot(hid.astype(BF16), w2_ref[0, 0].astype(BF16),
                            preferred_element_type=F32)

    @pl.when(e == pl.num_programs(1) - 1)
    def _():
        o_ref[...] = _layer_norm(DN_ALPHA * x_ref[...] + acc_ref[...],
                                 ln_ref[0, 0:1, :], ln_ref[0, 1:2, :])


def _moe(x2, comb, w1, w3, w2, ln, layer, tm):
    m = x2.shape[0]
    w_up = pl.BlockSpec((1, 1, D_MODEL, D_EXPERT), lambda i, e: (layer, e, 0, 0))
    return pl.pallas_call(
        _moe_kernel,
        grid=(m // tm, N_EXPERTS),
        in_specs=[pl.BlockSpec((tm, D_MODEL), lambda i, e: (i, 0)),
                  pl.BlockSpec((tm, LANES), lambda i, e: (i, 0)),
                  w_up, w_up,
                  pl.BlockSpec((1, 1, D_EXPERT, D_MODEL), lambda i, e: (layer, e, 0, 0)),
                  pl.BlockSpec((1, 2, D_MODEL), lambda i, e: (layer, 0, 0))],
        out_specs=pl.BlockSpec((tm, D_MODEL), lambda i, e: (i, 0)),
        out_shape=jax.ShapeDtypeStruct((m, D_MODEL), F32),
        scratch_shapes=[pltpu.VMEM((tm, D_MODEL), F32), pltpu.VMEM((tm, D_MODEL), BF16)],
        compiler_params=_cparams("parallel", "arbitrary"),
        name="moe_experts",
    )(x2, comb, w1, w3, w2, ln)


def _prep_w_in(w_in):
    o = np.cumsum((0, GDN_QKV, 512, 4, 4, 256, 256, 512, 512, 512, 512, 4096))
    seg = lambda i: w_in[:, :, o[i]:o[i + 1]]
    swap = np.arange(256) ^ 1
    main = jnp.concatenate([seg(0), seg(1), seg(4), seg(5), seg(4)[:, :, swap], seg(5)[:, :, swap],
                            seg(6), seg(7), seg(10), seg(8), seg(9)], axis=-1).astype(BF16)
    ba = jnp.concatenate([seg(2), seg(3), jnp.zeros((DEPTH, D_MODEL, LANES - 8), w_in.dtype)],
                         axis=-1).astype(BF16)
    return main, ba


def _rope_tables(pos):
    inv_freq = 1.0 / (ROPE_BASE ** jnp.linspace(0.0, 1.0, RET_DK // 2, dtype=F32))
    ang = pos.astype(F32)[:, None] * inv_freq
    cos = jnp.repeat(jnp.cos(ang), 2, axis=1)
    sin = jnp.repeat(jnp.sin(ang), 2, axis=1)
    sign = jnp.tile(jnp.array([-1.0, 1.0], F32), RET_DK // 2)
    return jnp.tile(cos, (1, RET_HEADS)), jnp.tile(sin * sign, (1, RET_HEADS))


def _prep_s5(a_re, a_im, log_dt, b_re, b_im, c_re, c_im):
    dt = jnp.exp(log_dt.astype(F32))[..., None]
    mag = jnp.exp(a_re * dt)
    ab_re, ab_im = mag * jnp.cos(a_im * dt), mag * jnp.sin(a_im * dt)
    den = a_re * a_re + a_im * a_im
    coef_re = ((ab_re - 1.0) * a_re + ab_im * a_im) / den
    coef_im = (ab_im * a_re - (ab_re - 1.0) * a_im) / den
    bb_re = coef_re[..., None] * b_re - coef_im[..., None] * b_im
    bb_im = coef_re[..., None] * b_im + coef_im[..., None] * b_re
    eye = jnp.eye(16, dtype=F32)

    def pack_b(bb):
        x = bb.reshape(DEPTH, 2, 16, S5_STATE, S5_GROUP)
        return jnp.einsum('dkgpc,gh->dkgchp', x, eye).reshape(DEPTH, 2, 256, 1024)

    def pack_c(cm):
        x = cm.reshape(DEPTH, 2, 16, S5_GROUP, S5_STATE)
        return jnp.einsum('dkgcp,gh->dkgphc', x, eye).reshape(DEPTH, 2, 1024, 256)

    bd = jnp.concatenate([pack_b(bb_re), pack_b(bb_im)], axis=-1).astype(BF16)
    cd = jnp.concatenate([pack_c(c_re.astype(F32)), -pack_c(c_im.astype(F32))], axis=-2).astype(BF16)
    ar = ab_re.reshape(DEPTH, 1, S5_CH)
    ai = ab_im.reshape(DEPTH, 1, S5_CH)
    return bd, cd, ar, ai


def kernel(x_prompt, x_sample, mem_prompt, state_gdn_conv, state_gdn, state_ret, state_pool,
           state_s5_re, state_s5_im, cache_mem_k, cache_mem_v, w_in, gdn_conv_w, gdn_a_log,
           gdn_dt_bias, gdn_norm_w, pool_w, pool_scale, s5_a_re, s5_a_im, s5_log_dt, s5_b_re,
           s5_b_im, s5_c_re, s5_c_im, s5_d, s5_w_glu, w_branch, w_out, xa_w_q, xa_w_k, xa_w_v,
           xa_w_o, ln_g, ln_b, w_router, b_router, moe_w1, moe_w3, moe_w2):
    bp, seq, _ = x_prompt.shape
    bs, dseq, _ = x_sample.shape
    past = 16384

    w_main, w_ba = _prep_w_in(w_in)
    gp = jnp.zeros((DEPTH, 2, LANES), F32)
    gp = gp.at[:, 0, 4:8].set(gdn_a_log.astype(F32)).at[:, 1, 4:8].set(gdn_dt_bias.astype(F32))
    nw = gdn_norm_w.astype(F32).reshape(DEPTH, 1, GDN_DK)
    cw = gdn_conv_w.astype(F32)
    pw = pool_w.astype(BF16)
    psc = pool_scale.astype(F32).reshape(DEPTH, 1, BRANCH)
    bd, cd, ar, ai = _prep_s5(s5_a_re.astype(F32), s5_a_im.astype(F32), s5_log_dt,
                              s5_b_re.astype(F32), s5_b_im.astype(F32), s5_c_re, s5_c_im)
    d_skip = s5_d.astype(F32).reshape(DEPTH, 1, BRANCH)
    glu = s5_w_glu.astype(BF16)
    wb = w_branch.astype(BF16)
    wo = w_out.astype(BF16)
    wq, wk, wv, wxo = (w.astype(BF16) for w in (xa_w_q, xa_w_k, xa_w_v, xa_w_o))
    ln = jnp.stack([ln_g.astype(F32), ln_b.astype(F32)], axis=2)
    ln1, ln2, ln3 = ln[:, 0], ln[:, 1], ln[:, 2]
    wr = jnp.concatenate([w_router, jnp.zeros((D_MODEL, LANES - N_EXPERTS), w_router.dtype)],
                         axis=1).astype(BF16)
    br = jnp.concatenate([b_router.astype(F32), jnp.full((LANES - N_EXPERTS,), -1e30, F32)])[None]

    cos_p, sin_p = _rope_tables(jnp.arange(seq))
    cos_s, sin_s = _rope_tables(past + jnp.arange(dseq))
    ns_s = 16
    cos_s, sin_s = jnp.tile(cos_s, (ns_s, 1)), jnp.tile(sin_s, (ns_s, 1))

    zeros = lambda *s: jnp.zeros((1,) + s, F32)
    z_conv, z_gdn = zeros(bp, GDN_CONV - 1, GDN_QKV), zeros(bp, GDN_HEADS, 128, 128)
    z_ret, z_pool, z_s5 = zeros(bp, 256, RET_DV), zeros(bp, POOL_BUF, BRANCH), zeros(bp, S5_CH)
    st_ret = state_ret.reshape(DEPTH, bs, 256, RET_DV)
    st_s5r = state_s5_re.reshape(DEPTH, bs, S5_CH)
    st_s5i = state_s5_im.reshape(DEPTH, bs, S5_CH)
    mem2d = mem_prompt.reshape(bp * MEM_LEN, D_MODEL)
    ck = cache_mem_k.reshape(DEPTH, bs, MEM_LEN, D_MODEL).astype(BF16)
    cv = cache_mem_v.reshape(DEPTH, bs, MEM_LEN, D_MODEL).astype(BF16)

    def block(x2d, nseq, length, layer, mem_k, mem_v, layer_mem, states, layer_state, cfg):
        conv0, gdn0, ret0, pool0, s5r0, s5i0 = states
        tm = cfg["tm"]
        h, hba = _in_proj(x2d, w_main, w_ba, layer, tm, 1024)
        h3 = h.reshape(nseq, length, H_COLS)
        ba3 = hba.reshape(nseq, length, LANES)
        br_a, new_gdn = _gdn(h3, ba3, conv0, gdn0, layer_state, cw, gp, nw, layer,
                             cfg["gdn_nb"], cfg["ns"], cfg["gdn_rl"], cfg["gdn_sl"])
        br_b, new_ret = _retention(h3, cfg["cos"], cfg["sin"], ret0, layer_state,
                                   cfg["ns"], cfg["ret_sl"])
        br_c = _pool(h3, pool0, layer_state, pw, psc, layer, cfg["ns"], cfg["pool_sl"], cfg["pos0"])
        br_d, new_re, new_im = _s5(h3, s5r0, s5i0, layer_state, bd, cd, ar, ai, d_skip, glu,
                                   layer, cfg["s5_tl"])
        brs = [br_a, br_b, br_c, br_d.reshape(nseq * length, BRANCH)]
        x1, q = _merge(x2d, h, brs, wb, wo, wq, ln1, layer, cfg["tm_merge"])
        o = _attention(q.reshape(nseq, length, D_MODEL), mem_k, mem_v, layer_mem, cfg["tq"])
        x2, comb = _post(x1, o.reshape(nseq * length, D_MODEL), wxo, ln2, wr, br, layer, tm)
        x3 = _moe(x2, comb, moe_w1, moe_w3, moe_w2, ln3, layer, tm)
        new_conv = h3[:, length - (GDN_CONV - 1):, COL_QKV:COL_QKV + GDN_QKV]
        pool_u = h3[:, :, COL_POOL:COL_POOL + BRANCH]
        return x3, (new_conv, new_gdn, new_ret.reshape(nseq, RET_HEADS, RET_DK, RET_DV), pool_u,
                    new_re.reshape(nseq, S5_GROUPS, S5_STATE), new_im.reshape(nseq, S5_GROUPS, S5_STATE))

    cfg_p = dict(tm=1024, tm_merge=512, ns=1, gdn_nb=1, gdn_rl=256, gdn_sl=64, ret_sl=128,
                 pool_sl=512, s5_tl=128, tq=512, pos0=0, cos=cos_p, sin=sin_p)
    cfg_s = dict(tm=1024, tm_merge=512, ns=ns_s, gdn_nb=1, gdn_rl=dseq, gdn_sl=dseq, ret_sl=dseq, pool_sl=dseq, s5_tl=dseq,
                 tq=dseq, pos0=past, cos=cos_s, sin=sin_s)

    yp = x_prompt.reshape(bp * seq, D_MODEL)
    ys = x_sample.reshape(bs * dseq, D_MODEL)
    p_out, s_out = [], []
    for l in range(DEPTH):
        mem_k = _matmul(mem2d, wk, l, 1024, 1024)
        mem_v = _matmul(mem2d, wv, l, 1024, 1024)
        mk4 = mem_k.reshape(1, bp, MEM_LEN, D_MODEL)
        mv4 = mem_v.reshape(1, bp, MEM_LEN, D_MODEL)
        yp, st = block(yp, bp, seq, l, mk4, mv4, 0,
                       (z_conv, z_gdn, z_ret, z_pool, z_s5, z_s5), 0, cfg_p)
        conv, gdn, ret, pool_u, s5r, s5i = st
        p_out.append((conv, gdn, ret, pool_u[:, seq - POOL_BUF:], s5r, s5i,
                      mem_k.reshape(bp, MEM_LEN, XA_HEADS, XA_DH),
                      mem_v.reshape(bp, MEM_LEN, XA_HEADS, XA_DH)))
        ys, st = block(ys, bs, dseq, l, ck, cv, l,
                       (state_gdn_conv, state_gdn, st_ret, state_pool, st_s5r, st_s5i), l, cfg_s)
        conv, gdn, ret, pool_u, s5r, s5i = st
        new_pool = jnp.concatenate([state_pool[l][:, dseq:], pool_u], axis=1)
        s_out.append((conv, gdn, ret, new_pool, s5r, s5i))
    p_st = [jnp.stack(t) for t in zip(*p_out)]
    s_st = [jnp.stack(t) for t in zip(*s_out)]
    return (yp.reshape(bp, seq, D_MODEL), ys.reshape(bs, dseq, D_MODEL), *p_st, *s_st)
```

```python
import functools
import math

import jax
import jax.numpy as jnp
import numpy as np
from jax import lax
from jax.experimental import pallas as pl
from jax.experimental.pallas import tpu as pltpu

F32 = jnp.float32
BF16 = jnp.bfloat16

D_MODEL = 1024
DEPTH = 4
BRANCH = 512
N_BRANCH = 4
GDN_HEADS = 4
GDN_DK = 128
GDN_QKV = 1536
GDN_CONV = 4
RET_HEADS = 4
RET_DK = 64
RET_DV = 128
ROPE_BASE = 10000.0
POOL_WINDOWS = (2, 4, 8, 16)
POOL_BUF = 15
S5_GROUPS = 32
S5_GROUP = 16
S5_STATE = 64
S5_CH = S5_GROUPS * S5_STATE
MEM_LEN = 256
XA_HEADS = 4
XA_DH = 256
N_EXPERTS = 16
EXPERTS_PER_GROUP = 4
D_EXPERT = 512
DN_ALPHA = (2.0 * DEPTH) ** 0.25
LN_EPS = 1e-5
RMS_EPS = 1e-6

LANES = 128
SUBLANES = 8
VMEM_LIMIT = 52 * 1024 * 1024

COL_QKV = 0
COL_Z = 1536
COL_RET = 2048
COL_GATE = 4096
COL_POOL = 8192
COL_S5 = 8704
H_COLS = 9216


def _cparams(*sem):
    return pltpu.CompilerParams(dimension_semantics=sem, vmem_limit_bytes=VMEM_LIMIT)


def _bdot(a, b):
    return jnp.dot(a.astype(BF16), b.astype(BF16), preferred_element_type=F32)


def _bdot_nt(a, b):
    return lax.dot_general(a.astype(BF16), b.astype(BF16), (((1,), (1,)), ((), ())),
                           preferred_element_type=F32)


def _silu(x):
    return x * jax.nn.sigmoid(x)


def _layer_norm(x, g, b):
    mu = jnp.mean(x, -1, keepdims=True)
    xc = x - mu
    var = jnp.mean(xc * xc, -1, keepdims=True)
    return xc * lax.rsqrt(var + LN_EPS) * g + b


def _mm_kernel(x_ref, w_ref, o_ref, xb_ref):
    @pl.when(pl.program_id(1) == 0)
    def _():
        xb_ref[...] = x_ref[...].astype(BF16)

    o_ref[...] = jnp.dot(xb_ref[...], w_ref[0], preferred_element_type=F32)


def _matmul(x, w, layer, tm, tn):
    m, k = x.shape
    n = w.shape[-1]
    return pl.pallas_call(
        _mm_kernel,
        grid=(m // tm, n // tn),
        in_specs=[pl.BlockSpec((tm, k), lambda i, j: (i, 0)),
                  pl.BlockSpec((1, k, tn), lambda i, j: (layer, 0, j))],
        out_specs=pl.BlockSpec((tm, tn), lambda i, j: (i, j)),
        out_shape=jax.ShapeDtypeStruct((m, n), F32),
        scratch_shapes=[pltpu.VMEM((tm, k), BF16)],
        compiler_params=_cparams("parallel", "arbitrary"),
        name="token_matmul",
    )(x, w)


def _inproj_kernel(x_ref, w_ref, wba_ref, o_ref, ba_ref, xb_ref):
    @pl.when(pl.program_id(1) == 0)
    def _():
        xb_ref[...] = x_ref[...].astype(BF16)
        ba_ref[...] = jnp.dot(xb_ref[...], wba_ref[0], preferred_element_type=F32)

    o_ref[...] = jnp.dot(xb_ref[...], w_ref[0], preferred_element_type=F32)


def _in_proj(x, w, w_ba, layer, tm, tn):
    m, k = x.shape
    n = w.shape[-1]
    return pl.pallas_call(
        _inproj_kernel,
        grid=(m // tm, n // tn),
        in_specs=[pl.BlockSpec((tm, k), lambda i, j: (i, 0)),
                  pl.BlockSpec((1, k, tn), lambda i, j: (layer, 0, j)),
                  pl.BlockSpec((1, k, LANES), lambda i, j: (layer, 0, 0))],
        out_specs=[pl.BlockSpec((tm, tn), lambda i, j: (i, j)),
                   pl.BlockSpec((tm, LANES), lambda i, j: (i, 0))],
        out_shape=[jax.ShapeDtypeStruct((m, n), F32), jax.ShapeDtypeStruct((m, LANES), F32)],
        scratch_shapes=[pltpu.VMEM((tm, k), BF16)],
        compiler_params=_cparams("parallel", "arbitrary"),
        name="in_proj",
    )(x, w, w_ba)


def _branch_out_spec(ns, sl, n_chunks):
    return pl.BlockSpec((ns * sl, BRANCH), lambda b, n: (b * n_chunks + n, 0))


def _seg_ids(c, sl):
    r = lax.broadcasted_iota(jnp.int32, (c, 1), 0)
    sh = int(math.log2(sl))
    return r, r & (sl - 1), r >> sh


def _gdn_kernel(qkv_ref, z_ref, ba_ref, cs_ref, s0_ref, cw_ref, gp_ref, nw_ref,
                o_ref, s_ref, full_ref, *, nb, ns, rl, sl):
    n = pl.program_id(1)

    @pl.when(n == 0)
    def _():
        full_ref[:, 5:8, :] = cs_ref[0]
        s_ref[...] = s0_ref[0]

    for bi in range(nb):
        _gdn_block(qkv_ref, z_ref, ba_ref, cw_ref, gp_ref, nw_ref, o_ref, s_ref, full_ref,
                   bi=bi, ns=ns, rl=rl, sl=sl)


def _gdn_block(qkv_ref, z_ref, ba_ref, cw_ref, gp_ref, nw_ref, o_ref, s_ref, full_ref,
               *, bi, ns, rl, sl):
    c = ns * rl
    nsps = rl // sl
    sq = slice(bi * ns, (bi + 1) * ns)

    u3 = qkv_ref[sq]
    full_ref[sq, 8:8 + rl, :] = u3
    cw = cw_ref[0]
    acc = (cw[3:4] * u3 + cw[2:3] * full_ref[sq, 7:7 + rl, :] + cw[1:2] * full_ref[sq, 6:6 + rl, :]
           + cw[0:1] * full_ref[sq, 5:5 + rl, :])
    full_ref[sq, 5:8, :] = full_ref[sq, 5 + rl:8 + rl, :]
    qkv = _silu(acc).reshape(c, GDN_QKV)

    ba = ba_ref[sq].reshape(c, LANES)
    beta_t = jax.nn.sigmoid(ba)
    xs = ba + gp_ref[0, 1:2, :]
    softplus = jnp.maximum(xs, 0.0) + jnp.log1p(jnp.exp(-jnp.abs(xs)))
    g_t = -jnp.exp(gp_ref[0, 0:1, :]) * softplus

    r, t, seq = _seg_ids(c, sl)
    gc = g_t
    s = 1
    while s < sl:
        gc = gc + jnp.where(t >= s, pltpu.roll(gc, s, 0), 0.0)
        s *= 2
    tot = gc
    s = 1
    while s < sl:
        tot = jnp.where(t + s < sl, pltpu.roll(tot, c - s, 0), tot)
        s *= 2
    rc = tot - gc
    if c < LANES:
        gc_pad = jnp.concatenate([gc, jnp.zeros((LANES - c, LANES), F32)], axis=0)
    else:
        gc_pad = gc
    gc_t = gc_pad.T

    rr = lax.broadcasted_iota(jnp.int32, (c, c), 0)
    cc = lax.broadcasted_iota(jnp.int32, (c, c), 1)
    sh = int(math.log2(sl))
    same = (rr >> sh) == (cc >> sh)
    causal = jnp.logical_and(same, rr >= cc)
    strict = jnp.logical_and(same, rr > cc)
    eye = (rr == cc).astype(F32)

    z = z_ref[sq].reshape(c, BRANCH)
    nw = nw_ref[0]
    heads = range(GDN_HEADS)

    qs, ks, kbs, q_decs, k_dec_ts, decays, rhss = [], [], [], [], [], [], []
    for h in heads:
        q = qkv[:, h * 128:(h + 1) * 128]
        k = qkv[:, 512 + h * 128:512 + (h + 1) * 128]
        v = qkv[:, 1024 + h * 128:1024 + (h + 1) * 128]
        q = q * lax.rsqrt(jnp.sum(q * q, -1, keepdims=True) + RMS_EPS) * (GDN_DK ** -0.5)
        k = k * lax.rsqrt(jnp.sum(k * k, -1, keepdims=True) + RMS_EPS)
        beta = beta_t[:, h:h + 1]
        gcc = gc[:, 4 + h:5 + h]
        gcr = gc_t[4 + h:5 + h, 0:c]
        eg = jnp.exp(gcc)
        kb = k * beta
        qs.append(q)
        ks.append(k)
        kbs.append(kb)
        q_decs.append(q * eg)
        k_dec_ts.append((k * jnp.exp(rc[:, 4 + h:5 + h])).T)
        decays.append(jnp.where(causal, jnp.exp(jnp.where(causal, gcc - gcr, 0.0)), 0.0))
        rhss.append(jnp.concatenate([v * beta, kb * eg], axis=1))
    npows = [-jnp.where(strict, _bdot_nt(kbs[h], ks[h]) * decays[h], 0.0) for h in heads]
    sms = [eye + npows[h] for h in heads]
    if sl > 2:
        npows = [_bdot(npows[h], npows[h]) for h in heads]
        m = 2
        while 2 * m < sl:
            prods = [_bdot(npows[h], jnp.concatenate([npows[h], sms[h]], axis=1)) for h in heads]
            npows = [p[:, :c] for p in prods]
            sms = [sms[h] + prods[h][:, c:] for h in heads]
            m *= 2
        sms = [sms[h] + _bdot(npows[h], sms[h]) for h in heads]
    sols = [_bdot(sms[h], rhss[h]) for h in heads]
    attns = [jnp.where(causal, _bdot_nt(qs[h], ks[h]) * decays[h], 0.0) for h in heads]

    v_parts = [[] for _ in heads]
    o_parts = [[] for _ in heads]
    for si in range(ns):
        sts = [s_ref[bi * ns + si, h] for h in heads]
        for j in range(nsps):
            sg = si * nsps + j
            rows = slice(sg * sl, (sg + 1) * sl)
            for h in heads:
                v_j = sols[h][rows, :128] - _bdot(sols[h][rows, 128:], sts[h])
                o_parts[h].append(_bdot(q_decs[h][rows], sts[h]))
                v_parts[h].append(v_j)
                pieces = [v_j]
                if sg > 0:
                    pieces.insert(0, jnp.zeros((sg * sl, 128), F32))
                if (sg + 1) * sl < c:
                    pieces.append(jnp.zeros((c - (sg + 1) * sl, 128), F32))
                vz = v_j if len(pieces) == 1 else jnp.concatenate(pieces, axis=0)
                last = jnp.exp(tot[sg * sl:sg * sl + 1, 4 + h:5 + h])
                sts[h] = sts[h] * last + _bdot(k_dec_ts[h], vz)
        for h in heads:
            s_ref[bi * ns + si, h] = sts[h]

    outs = []
    for h in heads:
        v_new = v_parts[h][0] if len(v_parts[h]) == 1 else jnp.concatenate(v_parts[h], axis=0)
        o_cross = o_parts[h][0] if len(o_parts[h]) == 1 else jnp.concatenate(o_parts[h], axis=0)
        o = o_cross + _bdot(attns[h], v_new)
        o = o * lax.rsqrt(jnp.mean(o * o, -1, keepdims=True) + RMS_EPS) * nw
        outs.append(o * _silu(z[:, h * 128:(h + 1) * 128]))
    o_ref[bi] = jnp.concatenate(outs, axis=1).astype(o_ref.dtype)


def _gdn(h3, ba3, conv_state, s0, layer_state, cw, gp, nw, layer, nb, ns, rl, sl):
    nseq, length, _ = h3.shape
    assert ns == 1 or rl == length
    nq = nb * ns
    grid = (nseq // nq, length // rl)
    kern = functools.partial(_gdn_kernel, nb=nb, ns=ns, rl=rl, sl=sl)
    branch, new_state = pl.pallas_call(
        kern,
        grid=grid,
        in_specs=[
            pl.BlockSpec((nq, rl, GDN_QKV), lambda b, n: (b, n, COL_QKV // GDN_QKV)),
            pl.BlockSpec((nq, rl, BRANCH), lambda b, n: (b, n, COL_Z // BRANCH)),
            pl.BlockSpec((nq, rl, LANES), lambda b, n: (b, n, 0)),
            pl.BlockSpec((1, nq, GDN_CONV - 1, GDN_QKV), lambda b, n: (layer_state, b, 0, 0)),
            pl.BlockSpec((1, nq, GDN_HEADS, 128, 128), lambda b, n: (layer_state, b, 0, 0, 0)),
            pl.BlockSpec((1, GDN_CONV, GDN_QKV), lambda b, n: (layer, 0, 0)),
            pl.BlockSpec((1, 2, LANES), lambda b, n: (layer, 0, 0)),
            pl.BlockSpec((1, 1, LANES), lambda b, n: (layer, 0, 0)),
        ],
        out_specs=[
            pl.BlockSpec((nb, ns * rl, BRANCH), lambda b, n: (b, n, 0)),
            pl.BlockSpec((nq, GDN_HEADS, 128, 128), lambda b, n: (b, 0, 0, 0)),
        ],
        out_shape=[jax.ShapeDtypeStruct((nseq // ns, ns * length, BRANCH), BF16),
                   jax.ShapeDtypeStruct((nseq, GDN_HEADS, 128, 128), F32)],
        scratch_shapes=[pltpu.VMEM((nq, SUBLANES + rl, GDN_QKV), F32)],
        compiler_params=_cparams("parallel", "arbitrary"),
        name="gated_deltanet",
    )(h3, h3, ba3, conv_state, s0, cw, gp, nw)
    return branch.reshape(nseq * length, BRANCH), new_state


_RET_LOG_GAMMA = tuple(math.log(1.0 - 2.0 ** (-5.0 - h)) for h in range(RET_HEADS))


def _per_head(idx, vals):
    out = jnp.full(idx.shape, vals[3], F32)
    for h in (2, 1, 0):
        out = jnp.where(idx < (h + 1) * RET_DK, vals[h], out)
    return out


def _ret_kernel(hb_ref, cos_ref, sin_ref, s0_ref, o_ref, s_ref, *, ns, sl):
    c = ns * sl
    n = pl.program_id(1)

    @pl.when(n == 0)
    def _():
        s_ref[...] = s0_ref[0]

    hb = hb_ref[...].reshape(c, 2048)
    cos = cos_ref[...]
    sin = sin_ref[...]
    rq = (hb[:, 0:256] * cos + hb[:, 512:768] * sin) * (RET_DK ** -0.5)
    rk = hb[:, 256:512] * cos + hb[:, 768:1024] * sin
    v = hb[:, 1024:1536]
    g = hb[:, 1536:2048]

    r, t, seq = _seg_ids(c, sl)
    tf = t.astype(F32)
    lane = lax.broadcasted_iota(jnp.int32, (1, 256), 1)
    lgl = _per_head(lane, _RET_LOG_GAMMA)
    q_dec = rq * jnp.exp(lgl * (tf + 1.0))
    k_dec = rk * jnp.exp(lgl * (sl - 1.0 - tf))
    k_dec_t = k_dec.T

    rr = lax.broadcasted_iota(jnp.int32, (c, c), 0)
    cc = lax.broadcasted_iota(jnp.int32, (c, c), 1)
    sh = int(math.log2(sl))
    causal = jnp.logical_and((rr >> sh) == (cc >> sh), rr >= cc)
    rel = jnp.maximum(rr - cc, 0).astype(F32)
    cseq = lax.broadcasted_iota(jnp.int32, (1, c), 1) >> sh
    rowi = lax.broadcasted_iota(jnp.int32, (256, 1), 0)
    cd_rows = jnp.exp(_per_head(rowi, _RET_LOG_GAMMA) * float(sl))

    cross = []
    for si in range(ns):
        rows = slice(si * sl, (si + 1) * sl)
        st = s_ref[si]
        parts = []
        for h in range(RET_HEADS):
            mh = jnp.logical_and(lane >= h * RET_DK, lane < (h + 1) * RET_DK)
            parts.append(_bdot(jnp.where(mh, q_dec[rows], 0.0), st))
        cross.append(parts)
        kd = k_dec_t if ns == 1 else jnp.where(cseq == si, k_dec_t, 0.0)
        res = _bdot(kd, v)
        upd = jnp.zeros((256, RET_DV), F32)
        for h in range(RET_HEADS):
            rm = jnp.logical_and(rowi >= h * RET_DK, rowi < (h + 1) * RET_DK)
            upd = upd + jnp.where(rm, res[:, h * 128:(h + 1) * 128], 0.0)
        s_ref[si] = st * cd_rows + upd

    heads = range(RET_HEADS)
    mhs = [jnp.logical_and(lane >= h * RET_DK, lane < (h + 1) * RET_DK) for h in heads]
    scs = [_bdot_nt(jnp.where(mhs[h], rq, 0.0), rk)
           * jnp.where(causal, jnp.exp(_RET_LOG_GAMMA[h] * rel), 0.0) for h in heads]
    os_ = [_bdot(scs[h], v[:, h * 128:(h + 1) * 128]) for h in heads]
    outs = []
    for h in heads:
        oc = cross[0][h] if ns == 1 else jnp.concatenate([cross[si][h] for si in range(ns)], axis=0)
        o = os_[h] + oc
        mu = jnp.mean(o, -1, keepdims=True)
        oc2 = o - mu
        o = oc2 * lax.rsqrt(jnp.mean(oc2 * oc2, -1, keepdims=True) + LN_EPS)
        outs.append(_silu(g[:, h * 128:(h + 1) * 128]) * o)
    o_ref[...] = jnp.concatenate(outs, axis=1).astype(o_ref.dtype)


def _retention(h3, cos_tab, sin_tab, s0, layer_state, ns, sl):
    nseq, length, _ = h3.shape
    c = ns * sl
    grid = (nseq // ns, length // sl)
    kern = functools.partial(_ret_kernel, ns=ns, sl=sl)
    return pl.pallas_call(
        kern,
        grid=grid,
        in_specs=[
            pl.BlockSpec((ns, sl, 2048), lambda b, n: (b, n, COL_RET // 2048)),
            pl.BlockSpec((c, 256), lambda b, n: (n, 0)),
            pl.BlockSpec((c, 256), lambda b, n: (n, 0)),
            pl.BlockSpec((1, ns, 256, RET_DV), lambda b, n: (layer_state, b, 0, 0)),
        ],
        out_specs=[
            _branch_out_spec(ns, sl, grid[1]),
            pl.BlockSpec((ns, 256, RET_DV), lambda b, n: (b, 0, 0)),
        ],
        out_shape=[jax.ShapeDtypeStruct((nseq * length, BRANCH), BF16),
                   jax.ShapeDtypeStruct((nseq, 256, RET_DV), F32)],
        compiler_params=_cparams("parallel", "arbitrary"),
        name="retention",
    )(h3, cos_tab, sin_tab, s0)


def _pool_kernel(u_ref, st_ref, w_ref, sc_ref, o_ref, full_ref, *, ns, sl, pos0, carry):
    c = ns * sl
    n = pl.program_id(1)

    @pl.when(n == 0)
    def _():
        full_ref[:, 1:16, :] = st_ref[0]

    u3 = u_ref[...]
    full_ref[:, 16:16 + sl, :] = u3
    tpos = lax.broadcasted_iota(jnp.int32, (1, sl, 1), 1) + n * sl
    n_avail = (tpos + (pos0 + 1)).astype(F32)
    outs = []
    for gi, w in enumerate(POOL_WINDOWS):
        cols = slice(gi * 128, (gi + 1) * 128)
        ug = u3[:, :, cols]
        acc = ug
        for j in range(1, w):
            acc = acc + full_ref[:, 16 - j:16 - j + sl, cols]
        pooled = acc / jnp.minimum(n_avail, float(w)) - ug
        mixed = _bdot(pooled.reshape(c, 128), w_ref[0, gi])
        outs.append(mixed * sc_ref[0, :, cols])
    if carry:
        full_ref[:, 1:16, :] = full_ref[:, sl + 1:sl + 16, :]
    o_ref[...] = jnp.concatenate(outs, axis=1).astype(o_ref.dtype)


def _pool(h3, state, layer_state, w_pool, pool_scale, layer, ns, sl, pos0):
    nseq, length, _ = h3.shape
    grid = (nseq // ns, length // sl)
    kern = functools.partial(_pool_kernel, ns=ns, sl=sl, pos0=pos0, carry=grid[1] > 1)
    return pl.pallas_call(
        kern,
        grid=grid,
        in_specs=[
            pl.BlockSpec((ns, sl, BRANCH), lambda b, n: (b, n, COL_POOL // BRANCH)),
            pl.BlockSpec((1, ns, POOL_BUF, BRANCH), lambda b, n: (layer_state, b, 0, 0)),
            pl.BlockSpec((1, 4, 128, 128), lambda b, n: (layer, 0, 0, 0)),
            pl.BlockSpec((1, 1, BRANCH), lambda b, n: (layer, 0, 0)),
        ],
        out_specs=_branch_out_spec(ns, sl, grid[1]),
        out_shape=jax.ShapeDtypeStruct((nseq * length, BRANCH), BF16),
        scratch_shapes=[pltpu.VMEM((ns, 16 + sl, BRANCH), F32)],
        compiler_params=_cparams("parallel", "arbitrary"),
        name="multi_pool",
    )(h3, state, w_pool, pool_scale)


S5_SEQS = SUBLANES
S5_SLABS = 2 * S5_CH // LANES


def _s5_pitch(tl):
    return tl + 4


def _s5_kernel(u_ref, hr_ref, hi_ref, bd_ref, cd_ref, ar_ref, ai_ref, d_ref, glu_ref,
               o_ref, or_ref, oi_ref, bu_ref, h_ref, arb_ref, aib_ref, up_ref, *, tl):
    pitch = _s5_pitch(tl)
    n = pl.program_id(1)

    @pl.when(n == 0)
    def _():
        hr = hr_ref[0]
        hi = hi_ref[0]
        for kb in range(2):
            for j in range(8):
                src = slice(kb * 1024 + j * 128, kb * 1024 + (j + 1) * 128)
                h_ref[kb * 16 + j] = hr[:, src]
                h_ref[kb * 16 + 8 + j] = hi[:, src]
        for kb in range(2):
            for j in range(8):
                src = slice(kb * 1024 + j * 128, kb * 1024 + (j + 1) * 128)
                arb_ref[kb * 8 + j] = jnp.broadcast_to(ar_ref[0, :, src], (S5_SEQS, LANES))
                aib_ref[kb * 8 + j] = jnp.broadcast_to(ai_ref[0, :, src], (S5_SEQS, LANES))

        up_ref[...] = jnp.zeros_like(up_ref)

    for s in range(S5_SEQS):
        up_ref[s * pitch:s * pitch + tl, :] = u_ref[s]
    up = up_ref[...]
    upb = up.astype(BF16)
    for kb in range(2):
        bu = jnp.dot(upb[:, kb * 256:(kb + 1) * 256], bd_ref[0, kb],
                     preferred_element_type=F32)
        for j in range(16):
            bu_ref[kb * 16 + j] = bu[:, j * 128:(j + 1) * 128]

    def step(l, carry):
        new = []
        for kb in range(2):
            for j in range(8):
                re = carry[kb * 16 + j]
                im = carry[kb * 16 + 8 + j]
                ar = arb_ref[kb * 8 + j]
                ai = aib_ref[kb * 8 + j]
                idx = pl.ds(l, S5_SEQS, stride=pitch)
                b_re = bu_ref[kb * 16 + j, idx, :]
                b_im = bu_ref[kb * 16 + 8 + j, idx, :]
                n_re = ar * re - ai * im + b_re
                n_im = ar * im + ai * re + b_im
                bu_ref[kb * 16 + j, idx, :] = n_re
                bu_ref[kb * 16 + 8 + j, idx, :] = n_im
                new.append((kb * 16 + j, n_re))
                new.append((kb * 16 + 8 + j, n_im))
        new.sort(key=lambda p: p[0])
        return tuple(p[1] for p in new)

    h0 = tuple(h_ref[i] for i in range(S5_SLABS))
    hf = lax.fori_loop(0, tl, step, h0)
    for i in range(S5_SLABS):
        h_ref[i] = hf[i]

    @pl.when(n == pl.num_programs(1) - 1)
    def _():
        for kb in range(2):
            for j in range(8):
                dst = slice(kb * 1024 + j * 128, kb * 1024 + (j + 1) * 128)
                or_ref[:, dst] = hf[kb * 16 + j]
                oi_ref[:, dst] = hf[kb * 16 + 8 + j]

    ys = []
    for kb in range(2):
        hs = jnp.concatenate([bu_ref[kb * 16 + j].astype(BF16) for j in range(16)], axis=1)
        ys.append(jnp.dot(hs, cd_ref[0, kb], preferred_element_type=F32))
    y = jnp.concatenate(ys, axis=1) + d_ref[0] * up
    act = jax.nn.gelu(y)
    out = act * jax.nn.sigmoid(_bdot(act, glu_ref[0]))
    for s in range(S5_SEQS):
        o_ref[s] = out[s * pitch:s * pitch + tl, :]


def _s5(h3, h_re, h_im, layer_state, bd, cd, ar, ai, d_skip, w_glu, layer, tl):
    nseq, length, _ = h3.shape
    grid = (nseq // S5_SEQS, length // tl)
    kern = functools.partial(_s5_kernel, tl=tl)
    rows = S5_SEQS * _s5_pitch(tl)
    return pl.pallas_call(
        kern,
        grid=grid,
        in_specs=[
            pl.BlockSpec((S5_SEQS, tl, BRANCH), lambda b, n: (b, n, COL_S5 // BRANCH)),
            pl.BlockSpec((1, S5_SEQS, S5_CH), lambda b, n: (layer_state, b, 0)),
            pl.BlockSpec((1, S5_SEQS, S5_CH), lambda b, n: (layer_state, b, 0)),
            pl.BlockSpec((1, 2, 256, 2048), lambda b, n: (layer, 0, 0, 0)),
            pl.BlockSpec((1, 2, 2048, 256), lambda b, n: (layer, 0, 0, 0)),
            pl.BlockSpec((1, 1, S5_CH), lambda b, n: (layer, 0, 0)),
            pl.BlockSpec((1, 1, S5_CH), lambda b, n: (layer, 0, 0)),
            pl.BlockSpec((1, 1, BRANCH), lambda b, n: (layer, 0, 0)),
            pl.BlockSpec((1, BRANCH, BRANCH), lambda b, n: (layer, 0, 0)),
        ],
        out_specs=[
            pl.BlockSpec((S5_SEQS, tl, BRANCH), lambda b, n: (b, n, 0)),
            pl.BlockSpec((S5_SEQS, S5_CH), lambda b, n: (b, 0)),
            pl.BlockSpec((S5_SEQS, S5_CH), lambda b, n: (b, 0)),
        ],
        out_shape=[jax.ShapeDtypeStruct((nseq, length, BRANCH), F32),
                   jax.ShapeDtypeStruct((nseq, S5_CH), F32),
                   jax.ShapeDtypeStruct((nseq, S5_CH), F32)],
        scratch_shapes=[pltpu.VMEM((S5_SLABS, rows, LANES), F32),
                        pltpu.VMEM((S5_SLABS, S5_SEQS, LANES), F32),
                        pltpu.VMEM((S5_SLABS // 2, S5_SEQS, LANES), F32),
                        pltpu.VMEM((S5_SLABS // 2, S5_SEQS, LANES), F32),
                        pltpu.VMEM((rows, BRANCH), F32)],
        compiler_params=_cparams("parallel", "arbitrary"),
        name="s5_ssm",
    )(h3, h_re, h_im, bd, cd, ar, ai, d_skip, w_glu)


def _merge_kernel(x_ref, ba_ref, bb_ref, bc_ref, bd_ref, g_ref, wb_ref, wo_ref, wq_ref, ln_ref,
                  x1_ref, q_ref):
    mixed = None
    for i, br in enumerate((ba_ref, bb_ref, bc_ref, bd_ref)):
        proj = jnp.dot(br[...].astype(BF16), wb_ref[0, i], preferred_element_type=F32)
        term = jax.nn.sigmoid(g_ref[:, i * D_MODEL:(i + 1) * D_MODEL]) * proj
        mixed = term if mixed is None else mixed + term
    y = jnp.dot(mixed.astype(BF16), wo_ref[0], preferred_element_type=F32)
    x1 = _layer_norm(DN_ALPHA * x_ref[...] + y, ln_ref[0, 0:1, :], ln_ref[0, 1:2, :])
    x1_ref[...] = x1
    q_ref[...] = jnp.dot(x1.astype(BF16), wq_ref[0], preferred_element_type=F32)


def _merge(x, h, brs, wb, wo, wq, ln, layer, tm):
    m = x.shape[0]
    tok = lambda w: pl.BlockSpec((tm, w), lambda i: (i, 0))
    return pl.pallas_call(
        _merge_kernel,
        grid=(m // tm,),
        in_specs=[tok(D_MODEL), tok(BRANCH), tok(BRANCH), tok(BRANCH), tok(BRANCH),
                  pl.BlockSpec((tm, 4096), lambda i: (i, COL_GATE // 4096)),
                  pl.BlockSpec((1, N_BRANCH, BRANCH, D_MODEL), lambda i: (layer, 0, 0, 0)),
                  pl.BlockSpec((1, D_MODEL, D_MODEL), lambda i: (layer, 0, 0)),
                  pl.BlockSpec((1, D_MODEL, D_MODEL), lambda i: (layer, 0, 0)),
                  pl.BlockSpec((1, 2, D_MODEL), lambda i: (layer, 0, 0))],
        out_specs=[tok(D_MODEL), tok(D_MODEL)],
        out_shape=[jax.ShapeDtypeStruct((m, D_MODEL), F32)] * 2,
        compiler_params=_cparams("parallel"),
        name="branch_merge",
    )(x, *brs, h, wb, wo, wq, ln)


def _attn_kernel(q_ref, k_ref, v_ref, o_ref):
    q = q_ref[0]
    heads = range(XA_HEADS)
    cols = [slice(h * XA_DH, (h + 1) * XA_DH) for h in heads]
    ss = [_bdot_nt(q[:, cols[h]], k_ref[0, 0, :, cols[h]]) * (XA_DH ** -0.5) for h in heads]
    es = [jnp.exp(s - jnp.max(s, -1, keepdims=True)) for s in ss]
    ps = [e / jnp.sum(e, -1, keepdims=True) for e in es]
    o_ref[0] = jnp.concatenate([_bdot(ps[h], v_ref[0, 0, :, cols[h]]) for h in heads], axis=1)


def _attention(q3, mem_k, mem_v, layer_mem, tq):
    nseq, length, _ = q3.shape
    mem = pl.BlockSpec((1, 1, MEM_LEN, D_MODEL), lambda b, i: (layer_mem, b, 0, 0))
    return pl.pallas_call(
        _attn_kernel,
        grid=(nseq, length // tq),
        in_specs=[pl.BlockSpec((1, tq, D_MODEL), lambda b, i: (b, i, 0)), mem, mem],
        out_specs=pl.BlockSpec((1, tq, D_MODEL), lambda b, i: (b, i, 0)),
        out_shape=jax.ShapeDtypeStruct((nseq, length, D_MODEL), F32),
        compiler_params=_cparams("parallel", "arbitrary"),
        name="memory_attention",
    )(q3, mem_k, mem_v)


def _post_kernel(x1_ref, o_ref, wo_ref, ln_ref, wr_ref, br_ref, x2_ref, comb_ref):
    y = jnp.dot(o_ref[...].astype(BF16), wo_ref[0], preferred_element_type=F32)
    x2 = _layer_norm(DN_ALPHA * x1_ref[...] + y, ln_ref[0, 0:1, :], ln_ref[0, 1:2, :])
    x2_ref[...] = x2
    logits = jnp.dot(x2.astype(BF16), wr_ref[...], preferred_element_type=F32) + br_ref[...]
    e = jnp.exp(logits - jnp.max(logits, -1, keepdims=True))
    p = e / jnp.sum(e, -1, keepdims=True)
    lane = lax.broadcasted_iota(jnp.int32, p.shape, 1)
    lanef = lane.astype(F32)
    grp = lane >> 2
    best = None
    for gidx in range(N_EXPERTS // EXPERTS_PER_GROUP):
        gm = jnp.max(jnp.where(grp == gidx, p, -1.0), -1, keepdims=True)
        if best is None:
            best, gi = gm, jnp.zeros(gm.shape, jnp.int32)
        else:
            upd = gm > best
            gi = jnp.where(upd, gidx, gi)
            best = jnp.where(upd, gm, best)
    cand = jnp.where(grp == gi, p, -1.0)
    m1 = jnp.max(cand, -1, keepdims=True)
    e1 = jnp.min(jnp.where(cand == m1, lanef, 1e9), -1, keepdims=True)
    cand2 = jnp.where(lanef == e1, -2.0, cand)
    m2 = jnp.max(cand2, -1, keepdims=True)
    e2 = jnp.min(jnp.where(cand2 == m2, lanef, 1e9), -1, keepdims=True)
    tot = m1 + m2
    comb_ref[...] = jnp.where(lanef == e1, m1 / tot, 0.0) + jnp.where(lanef == e2, m2 / tot, 0.0)


def _post(x1, o, wo, ln, wr, br, layer, tm):
    m = x1.shape[0]
    tok = lambda w: pl.BlockSpec((tm, w), lambda i: (i, 0))
    return pl.pallas_call(
        _post_kernel,
        grid=(m // tm,),
        in_specs=[tok(D_MODEL), tok(D_MODEL),
                  pl.BlockSpec((1, D_MODEL, D_MODEL), lambda i: (layer, 0, 0)),
                  pl.BlockSpec((1, 2, D_MODEL), lambda i: (layer, 0, 0)),
                  pl.BlockSpec((D_MODEL, LANES), lambda i: (0, 0)),
                  pl.BlockSpec((1, LANES), lambda i: (0, 0))],
        out_specs=[tok(D_MODEL), tok(LANES)],
        out_shape=[jax.ShapeDtypeStruct((m, D_MODEL), F32), jax.ShapeDtypeStruct((m, LANES), F32)],
        compiler_params=_cparams("parallel"),
        name="attn_out_router",
    )(x1, o, wo, ln, wr, br)


def _moe_kernel(x_ref, comb_ref, w1_ref, w3_ref, w2_ref, ln_ref, o_ref, acc_ref, xb_ref):
    e = pl.program_id(1)

    @pl.when(e == 0)
    def _():
        acc_ref[...] = jnp.zeros_like(acc_ref)
        xb_ref[...] = x_ref[...].astype(BF16)

    comb = comb_ref[...]
    lane = lax.broadcasted_iota(jnp.int32, comb.shape, 1)
    ce = jnp.sum(jnp.where(lane == e, comb, 0.0), -1, keepdims=True)
    xb = xb_ref[...]
    h1 = jnp.dot(xb, w1_ref[0, 0].astype(BF16), preferred_element_type=F32)
    h3 = jnp.dot(xb, w3_ref[0, 0].astype(BF16), preferred_element_type=F32)
    hid = _silu(h1) * h3 * ce
    acc_ref[...] += jnp.dot(hid.astype(BF16), w2_ref[0, 0].astype(BF16),
                            preferred_element_type=F32)

    @pl.when(e == pl.num_programs(1) - 1)
    def _():
        o_ref[...] = _layer_norm(DN_ALPHA * x_ref[...] + acc_ref[...],
                                 ln_ref[0, 0:1, :], ln_ref[0, 1:2, :])


def _moe(x2, comb, w1, w3, w2, ln, layer, tm):
    m = x2.shape[0]
    w_up = pl.BlockSpec((1, 1, D_MODEL, D_EXPERT), lambda i, e: (layer, e, 0, 0))
    return pl.pallas_call(
        _moe_kernel,
        grid=(m // tm, N_EXPERTS),
        in_specs=[pl.BlockSpec((tm, D_MODEL), lambda i, e: (i, 0)),
                  pl.BlockSpec((tm, LANES), lambda i, e: (i, 0)),
                  w_up, w_up,
                  pl.BlockSpec((1, 1, D_EXPERT, D_MODEL), lambda i, e: (layer, e, 0, 0)),
                  pl.BlockSpec((1, 2, D_MODEL), lambda i, e: (layer, 0, 0))],
        out_specs=pl.BlockSpec((tm, D_MODEL), lambda i, e: (i, 0)),
        out_shape=jax.ShapeDtypeStruct((m, D_MODEL), F32),
        scratch_shapes=[pltpu.VMEM((tm, D_MODEL), F32), pltpu.VMEM((tm, D_MODEL), BF16)],
        compiler_params=_cparams("parallel", "arbitrary"),
        name="moe_experts",
    )(x2, comb, w1, w3, w2, ln)


def _prep_w_in(w_in):
    o = np.cumsum((0, GDN_QKV, 512, 4, 4, 256, 256, 512, 512, 512, 512, 4096))
    seg = lambda i: w_in[:, :, o[i]:o[i + 1]]
    swap = np.arange(256) ^ 1
    main = jnp.concatenate([seg(0), seg(1), seg(4), seg(5), seg(4)[:, :, swap], seg(5)[:, :, swap],
                            seg(6), seg(7), seg(10), seg(8), seg(9)], axis=-1).astype(BF16)
    ba = jnp.concatenate([seg(2), seg(3), jnp.zeros((DEPTH, D_MODEL, LANES - 8), w_in.dtype)],
                         axis=-1).astype(BF16)
    return main, ba


def _rope_tables(pos):
    inv_freq = 1.0 / (ROPE_BASE ** jnp.linspace(0.0, 1.0, RET_DK // 2, dtype=F32))
    ang = pos.astype(F32)[:, None] * inv_freq
    cos = jnp.repeat(jnp.cos(ang), 2, axis=1)
    sin = jnp.repeat(jnp.sin(ang), 2, axis=1)
    sign = jnp.tile(jnp.array([-1.0, 1.0], F32), RET_DK // 2)
    return jnp.tile(cos, (1, RET_HEADS)), jnp.tile(sin * sign, (1, RET_HEADS))


def _prep_s5(a_re, a_im, log_dt, b_re, b_im, c_re, c_im):
    dt = jnp.exp(log_dt.astype(F32))[..., None]
    mag = jnp.exp(a_re * dt)
    ab_re, ab_im = mag * jnp.cos(a_im * dt), mag * jnp.sin(a_im * dt)
    den = a_re * a_re + a_im * a_im
    coef_re = ((ab_re - 1.0) * a_re + ab_im * a_im) / den
    coef_im = (ab_im * a_re - (ab_re - 1.0) * a_im) / den
    bb_re = coef_re[..., None] * b_re - coef_im[..., None] * b_im
    bb_im = coef_re[..., None] * b_im + coef_im[..., None] * b_re
    eye = jnp.eye(16, dtype=F32)

    def pack_b(bb):
        x = bb.reshape(DEPTH, 2, 16, S5_STATE, S5_GROUP)
        return jnp.einsum('dkgpc,gh->dkgchp', x, eye).reshape(DEPTH, 2, 256, 1024)

    def pack_c(cm):
        x = cm.reshape(DEPTH, 2, 16, S5_GROUP, S5_STATE)
        return jnp.einsum('dkgcp,gh->dkgphc', x, eye).reshape(DEPTH, 2, 1024, 256)

    bd = jnp.concatenate([pack_b(bb_re), pack_b(bb_im)], axis=-1).astype(BF16)
    cd = jnp.concatenate([pack_c(c_re.astype(F32)), -pack_c(c_im.astype(F32))], axis=-2).astype(BF16)
    ar = ab_re.reshape(DEPTH, 1, S5_CH)
    ai = ab_im.reshape(DEPTH, 1, S5_CH)
    return bd, cd, ar, ai


def kernel(x_prompt, x_sample, mem_prompt, state_gdn_conv, state_gdn, state_ret, state_pool,
           state_s5_re, state_s5_im, cache_mem_k, cache_mem_v, w_in, gdn_conv_w, gdn_a_log,
           gdn_dt_bias, gdn_norm_w, pool_w, pool_scale, s5_a_re, s5_a_im, s5_log_dt, s5_b_re,
           s5_b_im, s5_c_re, s5_c_im, s5_d, s5_w_glu, w_branch, w_out, xa_w_q, xa_w_k, xa_w_v,
           xa_w_o, ln_g, ln_b, w_router, b_router, moe_w1, moe_w3, moe_w2):
    bp, seq, _ = x_prompt.shape
    bs, dseq, _ = x_sample.shape
    past = 16384

    w_main, w_ba = _prep_w_in(w_in)
    gp = jnp.zeros((DEPTH, 2, LANES), F32)
    gp = gp.at[:, 0, 4:8].set(gdn_a_log.astype(F32)).at[:, 1, 4:8].set(gdn_dt_bias.astype(F32))
    nw = gdn_norm_w.astype(F32).reshape(DEPTH, 1, GDN_DK)
    cw = gdn_conv_w.astype(F32)
    pw = pool_w.astype(BF16)
    psc = pool_scale.astype(F32).reshape(DEPTH, 1, BRANCH)
    bd, cd, ar, ai = _prep_s5(s5_a_re.astype(F32), s5_a_im.astype(F32), s5_log_dt,
                              s5_b_re.astype(F32), s5_b_im.astype(F32), s5_c_re, s5_c_im)
    d_skip = s5_d.astype(F32).reshape(DEPTH, 1, BRANCH)
    glu = s5_w_glu.astype(BF16)
    wb = w_branch.astype(BF16)
    wo = w_out.astype(BF16)
    wq, wk, wv, wxo = (w.astype(BF16) for w in (xa_w_q, xa_w_k, xa_w_v, xa_w_o))
    ln = jnp.stack([ln_g.astype(F32), ln_b.astype(F32)], axis=2)
    ln1, ln2, ln3 = ln[:, 0], ln[:, 1], ln[:, 2]
    wr = jnp.concatenate([w_router, jnp.zeros((D_MODEL, LANES - N_EXPERTS), w_router.dtype)],
                         axis=1).astype(BF16)
    br = jnp.concatenate([b_router.astype(F32), jnp.full((LANES - N_EXPERTS,), -1e30, F32)])[None]

    cos_p, sin_p = _rope_tables(jnp.arange(seq))
    cos_s, sin_s = _rope_tables(past + jnp.arange(dseq))
    ns_s = 16
    cos_s, sin_s = jnp.tile(cos_s, (ns_s, 1)), jnp.tile(sin_s, (ns_s, 1))

    zeros = lambda *s: jnp.zeros((1,) + s, F32)
    z_conv, z_gdn = zeros(bp, GDN_CONV - 1, GDN_QKV), zeros(bp, GDN_HEADS, 128, 128)
    z_ret, z_pool, z_s5 = zeros(bp, 256, RET_DV), zeros(bp, POOL_BUF, BRANCH), zeros(bp, S5_CH)
    st_ret = state_ret.reshape(DEPTH, bs, 256, RET_DV)
    st_s5r = state_s5_re.reshape(DEPTH, bs, S5_CH)
    st_s5i = state_s5_im.reshape(DEPTH, bs, S5_CH)
    mem2d = mem_prompt.reshape(bp * MEM_LEN, D_MODEL)
    ck = cache_mem_k.reshape(DEPTH, bs, MEM_LEN, D_MODEL)
    cv = cache_mem_v.reshape(DEPTH, bs, MEM_LEN, D_MODEL)

    def block(x2d, nseq, length, layer, mem_k, mem_v, layer_mem, states, layer_state, cfg):
        conv0, gdn0, ret0, pool0, s5r0, s5i0 = states
        tm = cfg["tm"]
        h, hba = _in_proj(x2d, w_main, w_ba, layer, tm, 1024)
        h3 = h.reshape(nseq, length, H_COLS)
        ba3 = hba.reshape(nseq, length, LANES)
        br_a, new_gdn = _gdn(h3, ba3, conv0, gdn0, layer_state, cw, gp, nw, layer,
                             cfg["gdn_nb"], cfg["ns"], cfg["gdn_rl"], cfg["gdn_sl"])
        br_b, new_ret = _retention(h3, cfg["cos"], cfg["sin"], ret0, layer_state,
                                   cfg["ns"], cfg["ret_sl"])
        br_c = _pool(h3, pool0, layer_state, pw, psc, layer, cfg["ns"], cfg["pool_sl"], cfg["pos0"])
        br_d, new_re, new_im = _s5(h3, s5r0, s5i0, layer_state, bd, cd, ar, ai, d_skip, glu,
                                   layer, cfg["s5_tl"])
        brs = [br_a, br_b, br_c, br_d.reshape(nseq * length, BRANCH)]
        x1, q = _merge(x2d, h, brs, wb, wo, wq, ln1, layer, cfg["tm_merge"])
        o = _attention(q.reshape(nseq, length, D_MODEL), mem_k, mem_v, layer_mem, cfg["tq"])
        x2, comb = _post(x1, o.reshape(nseq * length, D_MODEL), wxo, ln2, wr, br, layer, tm)
        x3 = _moe(x2, comb, moe_w1, moe_w3, moe_w2, ln3, layer, tm)
        new_conv = h3[:, length - (GDN_CONV - 1):, COL_QKV:COL_QKV + GDN_QKV]
        pool_u = h3[:, :, COL_POOL:COL_POOL + BRANCH]
        return x3, (new_conv, new_gdn, new_ret.reshape(nseq, RET_HEADS, RET_DK, RET_DV), pool_u,
                    new_re.reshape(nseq, S5_GROUPS, S5_STATE), new_im.reshape(nseq, S5_GROUPS, S5_STATE))

    cfg_p = dict(tm=1024, tm_merge=512, ns=1, gdn_nb=1, gdn_rl=256, gdn_sl=64, ret_sl=128,
                 pool_sl=512, s5_tl=128, tq=512, pos0=0, cos=cos_p, sin=sin_p)
    cfg_s = dict(tm=1024, tm_merge=512, ns=ns_s, gdn_nb=1, gdn_rl=dseq, gdn_sl=dseq, ret_sl=dseq, pool_sl=dseq, s5_tl=dseq,
                 tq=dseq, pos0=past, cos=cos_s, sin=sin_s)

    yp = x_prompt.reshape(bp * seq, D_MODEL)
    ys = x_sample.reshape(bs * dseq, D_MODEL)
    p_out, s_out = [], []
    for l in range(DEPTH):
        mem_k = _matmul(mem2d, wk, l, 1024, 1024)
        mem_v = _matmul(mem2d, wv, l, 1024, 1024)
        mk4 = mem_k.reshape(1, bp, MEM_LEN, D_MODEL)
        mv4 = mem_v.reshape(1, bp, MEM_LEN, D_MODEL)
        yp, st = block(yp, bp, seq, l, mk4, mv4, 0,
                       (z_conv, z_gdn, z_ret, z_pool, z_s5, z_s5), 0, cfg_p)
        conv, gdn, ret, pool_u, s5r, s5i = st
        p_out.append((conv, gdn, ret, pool_u[:, seq - POOL_BUF:], s5r, s5i,
                      mem_k.reshape(bp, MEM_LEN, XA_HEADS, XA_DH),
                      mem_v.reshape(bp, MEM_LEN, XA_HEADS, XA_DH)))
        ys, st = block(ys, bs, dseq, l, ck, cv, l,
                       (state_gdn_conv, state_gdn, st_ret, state_pool, st_s5r, st_s5i), l, cfg_s)
        conv, gdn, ret, pool_u, s5r, s5i = st
        new_pool = jnp.concatenate([state_pool[l][:, dseq:], pool_u], axis=1)
        s_out.append((conv, gdn, ret, new_pool, s5r, s5i))
    p_st = [jnp.stack(t) for t in zip(*p_out)]
    s_st = [jnp.stack(t) for t in zip(*s_out)]
    return (yp.reshape(bp, seq, D_MODEL), ys.reshape(bs, dseq, D_MODEL), *p_st, *s_st)
```

```python
import functools
import math

import jax
import jax.numpy as jnp
import numpy as np
from jax import lax
from jax.experimental import pallas as pl
from jax.experimental.pallas import tpu as pltpu

F32 = jnp.float32
BF16 = jnp.bfloat16

D_MODEL = 1024
DEPTH = 4
BRANCH = 512
N_BRANCH = 4
GDN_HEADS = 4
GDN_DK = 128
GDN_QKV = 1536
GDN_CONV = 4
RET_HEADS = 4
RET_DK = 64
RET_DV = 128
ROPE_BASE = 10000.0
POOL_WINDOWS = (2, 4, 8, 16)
POOL_BUF = 15
S5_GROUPS = 32
S5_GROUP = 16
S5_STATE = 64
S5_CH = S5_GROUPS * S5_STATE
MEM_LEN = 256
XA_HEADS = 4
XA_DH = 256
N_EXPERTS = 16
EXPERTS_PER_GROUP = 4
D_EXPERT = 512
DN_ALPHA = (2.0 * DEPTH) ** 0.25
LN_EPS = 1e-5
RMS_EPS = 1e-6

LANES = 128
SUBLANES = 8
VMEM_LIMIT = 52 * 1024 * 1024

COL_QKV = 0
COL_Z = 1536
COL_RET = 2048
COL_GATE = 4096
COL_POOL = 8192
COL_S5 = 8704
H_COLS = 9216


def _cparams(*sem):
    return pltpu.CompilerParams(dimension_semantics=sem, vmem_limit_bytes=VMEM_LIMIT)


def _bdot(a, b):
    return jnp.dot(a.astype(BF16), b.astype(BF16), preferred_element_type=F32)


def _bdot_nt(a, b):
    return lax.dot_general(a.astype(BF16), b.astype(BF16), (((1,), (1,)), ((), ())),
                           preferred_element_type=F32)


def _silu(x):
    return x * jax.nn.sigmoid(x)


def _layer_norm(x, g, b):
    mu = jnp.mean(x, -1, keepdims=True)
    xc = x - mu
    var = jnp.mean(xc * xc, -1, keepdims=True)
    return xc * lax.rsqrt(var + LN_EPS) * g + b


def _mm_kernel(x_ref, w_ref, o_ref, xb_ref):
    @pl.when(pl.program_id(1) == 0)
    def _():
        xb_ref[...] = x_ref[...].astype(BF16)

    o_ref[...] = jnp.dot(xb_ref[...], w_ref[0], preferred_element_type=F32)


def _matmul(x, w, layer, tm, tn):
    m, k = x.shape
    n = w.shape[-1]
    return pl.pallas_call(
        _mm_kernel,
        grid=(m // tm, n // tn),
        in_specs=[pl.BlockSpec((tm, k), lambda i, j: (i, 0)),
                  pl.BlockSpec((1, k, tn), lambda i, j: (layer, 0, j))],
        out_specs=pl.BlockSpec((tm, tn), lambda i, j: (i, j)),
        out_shape=jax.ShapeDtypeStruct((m, n), F32),
        scratch_shapes=[pltpu.VMEM((tm, k), BF16)],
        compiler_params=_cparams("parallel", "arbitrary"),
        name="token_matmul",
    )(x, w)


def _inproj_kernel(x_ref, w_ref, wba_ref, o_ref, ba_ref, xb_ref):
    @pl.when(pl.program_id(1) == 0)
    def _():
        xb_ref[...] = x_ref[...].astype(BF16)
        ba_ref[...] = jnp.dot(xb_ref[...], wba_ref[0], preferred_element_type=F32)

    o_ref[...] = jnp.dot(xb_ref[...], w_ref[0], preferred_element_type=F32)


def _in_proj(x, w, w_ba, layer, tm, tn):
    m, k = x.shape
    n = w.shape[-1]
    return pl.pallas_call(
        _inproj_kernel,
        grid=(m // tm, n // tn),
        in_specs=[pl.BlockSpec((tm, k), lambda i, j: (i, 0)),
                  pl.BlockSpec((1, k, tn), lambda i, j: (layer, 0, j)),
                  pl.BlockSpec((1, k, LANES), lambda i, j: (layer, 0, 0))],
        out_specs=[pl.BlockSpec((tm, tn), lambda i, j: (i, j)),
                   pl.BlockSpec((tm, LANES), lambda i, j: (i, 0))],
        out_shape=[jax.ShapeDtypeStruct((m, n), F32), jax.ShapeDtypeStruct((m, LANES), F32)],
        scratch_shapes=[pltpu.VMEM((tm, k), BF16)],
        compiler_params=_cparams("parallel", "arbitrary"),
        name="in_proj",
    )(x, w, w_ba)


def _branch_out_spec(ns, sl, n_chunks):
    return pl.BlockSpec((ns * sl, BRANCH), lambda b, n: (b * n_chunks + n, 0))


def _seg_ids(c, sl):
    r = lax.broadcasted_iota(jnp.int32, (c, 1), 0)
    sh = int(math.log2(sl))
    return r, r & (sl - 1), r >> sh


def _gdn_kernel(qkv_ref, z_ref, ba_ref, cs_ref, s0_ref, cw_ref, gp_ref, nw_ref,
                o_ref, s_ref, full_ref, *, nb, ns, rl, sl):
    n = pl.program_id(1)

    @pl.when(n == 0)
    def _():
        full_ref[:, 5:8, :] = cs_ref[0]
        s_ref[...] = s0_ref[0]

    for bi in range(nb):
        _gdn_block(qkv_ref, z_ref, ba_ref, cw_ref, gp_ref, nw_ref, o_ref, s_ref, full_ref,
                   bi=bi, ns=ns, rl=rl, sl=sl)


def _gdn_block(qkv_ref, z_ref, ba_ref, cw_ref, gp_ref, nw_ref, o_ref, s_ref, full_ref,
               *, bi, ns, rl, sl):
    c = ns * rl
    nsps = rl // sl
    sq = slice(bi * ns, (bi + 1) * ns)

    u3 = qkv_ref[sq]
    full_ref[sq, 8:8 + rl, :] = u3
    cw = cw_ref[0]
    acc = (cw[3:4] * u3 + cw[2:3] * full_ref[sq, 7:7 + rl, :] + cw[1:2] * full_ref[sq, 6:6 + rl, :]
           + cw[0:1] * full_ref[sq, 5:5 + rl, :])
    full_ref[sq, 5:8, :] = full_ref[sq, 5 + rl:8 + rl, :]
    qkv = _silu(acc).reshape(c, GDN_QKV)

    ba = ba_ref[sq].reshape(c, LANES)
    beta_t = jax.nn.sigmoid(ba)
    xs = ba + gp_ref[0, 1:2, :]
    softplus = jnp.maximum(xs, 0.0) + jnp.log1p(jnp.exp(-jnp.abs(xs)))
    g_t = -jnp.exp(gp_ref[0, 0:1, :]) * softplus

    r, t, seq = _seg_ids(c, sl)
    gc = g_t
    s = 1
    while s < sl:
        gc = gc + jnp.where(t >= s, pltpu.roll(gc, s, 0), 0.0)
        s *= 2
    tot = gc
    s = 1
    while s < sl:
        tot = jnp.where(t + s < sl, pltpu.roll(tot, c - s, 0), tot)
        s *= 2
    rc = tot - gc
    if c < LANES:
        gc_pad = jnp.concatenate([gc, jnp.zeros((LANES - c, LANES), F32)], axis=0)
    else:
        gc_pad = gc
    gc_t = gc_pad.T

    rr = lax.broadcasted_iota(jnp.int32, (c, c), 0)
    cc = lax.broadcasted_iota(jnp.int32, (c, c), 1)
    sh = int(math.log2(sl))
    same = (rr >> sh) == (cc >> sh)
    causal = jnp.logical_and(same, rr >= cc)
    strict = jnp.logical_and(same, rr > cc)
    eye = (rr == cc).astype(F32)

    z = z_ref[sq].reshape(c, BRANCH)
    nw = nw_ref[0]
    heads = range(GDN_HEADS)

    qs, ks, kbs, q_decs, k_dec_ts, decays, rhss = [], [], [], [], [], [], []
    for h in heads:
        q = qkv[:, h * 128:(h + 1) * 128]
        k = qkv[:, 512 + h * 128:512 + (h + 1) * 128]
        v = qkv[:, 1024 + h * 128:1024 + (h + 1) * 128]
        q = q * lax.rsqrt(jnp.sum(q * q, -1, keepdims=True) + RMS_EPS) * (GDN_DK ** -0.5)
        k = k * lax.rsqrt(jnp.sum(k * k, -1, keepdims=True) + RMS_EPS)
        beta = beta_t[:, h:h + 1]
        gcc = gc[:, 4 + h:5 + h]
        gcr = gc_t[4 + h:5 + h, 0:c]
        eg = jnp.exp(gcc)
        kb = k * beta
        qs.append(q)
        ks.append(k)
        kbs.append(kb)
        q_decs.append(q * eg)
        k_dec_ts.append((k * jnp.exp(rc[:, 4 + h:5 + h])).T)
        decays.append(jnp.where(causal, jnp.exp(jnp.where(causal, gcc - gcr, 0.0)), 0.0))
        rhss.append(jnp.concatenate([v * beta, kb * eg], axis=1))
    npows = [-jnp.where(strict, _bdot_nt(kbs[h], ks[h]) * decays[h], 0.0) for h in heads]
    sms = [eye + npows[h] for h in heads]
    if sl > 2:
        npows = [_bdot(npows[h], npows[h]) for h in heads]
        m = 2
        while 2 * m < sl:
            prods = [_bdot(npows[h], jnp.concatenate([npows[h], sms[h]], axis=1)) for h in heads]
            npows = [p[:, :c] for p in prods]
            sms = [sms[h] + prods[h][:, c:] for h in heads]
            m *= 2
        sms = [sms[h] + _bdot(npows[h], sms[h]) for h in heads]
    sols = [_bdot(sms[h], rhss[h]) for h in heads]
    attns = [jnp.where(causal, _bdot_nt(qs[h], ks[h]) * decays[h], 0.0) for h in heads]

    v_parts = [[] for _ in heads]
    o_parts = [[] for _ in heads]
    for si in range(ns):
        sts = [s_ref[bi * ns + si, h] for h in heads]
        for j in range(nsps):
            sg = si * nsps + j
            rows = slice(sg * sl, (sg + 1) * sl)
            for h in heads:
                v_j = sols[h][rows, :128] - _bdot(sols[h][rows, 128:], sts[h])
                o_parts[h].append(_bdot(q_decs[h][rows], sts[h]))
                v_parts[h].append(v_j)
                pieces = [v_j]
                if sg > 0:
                    pieces.insert(0, jnp.zeros((sg * sl, 128), F32))
                if (sg + 1) * sl < c:
                    pieces.append(jnp.zeros((c - (sg + 1) * sl, 128), F32))
                vz = v_j if len(pieces) == 1 else jnp.concatenate(pieces, axis=0)
                last = jnp.exp(tot[sg * sl:sg * sl + 1, 4 + h:5 + h])
                sts[h] = sts[h] * last + _bdot(k_dec_ts[h], vz)
        for h in heads:
            s_ref[bi * ns + si, h] = sts[h]

    outs = []
    for h in heads:
        v_new = v_parts[h][0] if len(v_parts[h]) == 1 else jnp.concatenate(v_parts[h], axis=0)
        o_cross = o_parts[h][0] if len(o_parts[h]) == 1 else jnp.concatenate(o_parts[h], axis=0)
        o = o_cross + _bdot(attns[h], v_new)
        o = o * lax.rsqrt(jnp.mean(o * o, -1, keepdims=True) + RMS_EPS) * nw
        outs.append(o * _silu(z[:, h * 128:(h + 1) * 128]))
    o_ref[bi] = jnp.concatenate(outs, axis=1).astype(o_ref.dtype)


def _gdn(h3, ba3, conv_state, s0, layer_state, cw, gp, nw, layer, nb, ns, rl, sl):
    nseq, length, _ = h3.shape
    assert ns == 1 or rl == length
    nq = nb * ns
    grid = (nseq // nq, length // rl)
    kern = functools.partial(_gdn_kernel, nb=nb, ns=ns, rl=rl, sl=sl)
    branch, new_state = pl.pallas_call(
        kern,
        grid=grid,
        in_specs=[
            pl.BlockSpec((nq, rl, GDN_QKV), lambda b, n: (b, n, COL_QKV // GDN_QKV)),
            pl.BlockSpec((nq, rl, BRANCH), lambda b, n: (b, n, COL_Z // BRANCH)),
            pl.BlockSpec((nq, rl, LANES), lambda b, n: (b, n, 0)),
            pl.BlockSpec((1, nq, GDN_CONV - 1, GDN_QKV), lambda b, n: (layer_state, b, 0, 0)),
            pl.BlockSpec((1, nq, GDN_HEADS, 128, 128), lambda b, n: (layer_state, b, 0, 0, 0)),
            pl.BlockSpec((1, GDN_CONV, GDN_QKV), lambda b, n: (layer, 0, 0)),
            pl.BlockSpec((1, 2, LANES), lambda b, n: (layer, 0, 0)),
            pl.BlockSpec((1, 1, LANES), lambda b, n: (layer, 0, 0)),
        ],
        out_specs=[
            pl.BlockSpec((nb, ns * rl, BRANCH), lambda b, n: (b, n, 0)),
            pl.BlockSpec((nq, GDN_HEADS, 128, 128), lambda b, n: (b, 0, 0, 0)),
        ],
        out_shape=[jax.ShapeDtypeStruct((nseq // ns, ns * length, BRANCH), BF16),
                   jax.ShapeDtypeStruct((nseq, GDN_HEADS, 128, 128), F32)],
        scratch_shapes=[pltpu.VMEM((nq, SUBLANES + rl, GDN_QKV), F32)],
        compiler_params=_cparams("parallel", "arbitrary"),
        name="gated_deltanet",
    )(h3, h3, ba3, conv_state, s0, cw, gp, nw)
    return branch.reshape(nseq * length, BRANCH), new_state


_RET_LOG_GAMMA = tuple(math.log(1.0 - 2.0 ** (-5.0 - h)) for h in range(RET_HEADS))


def _per_head(idx, vals):
    out = jnp.full(idx.shape, vals[3], F32)
    for h in (2, 1, 0):
        out = jnp.where(idx < (h + 1) * RET_DK, vals[h], out)
    return out


def _ret_kernel(hb_ref, cos_ref, sin_ref, s0_ref, o_ref, s_ref, *, ns, sl):
    c = ns * sl
    n = pl.program_id(1)

    @pl.when(n == 0)
    def _():
        s_ref[...] = s0_ref[0]

    hb = hb_ref[...].reshape(c, 2048)
    cos = cos_ref[...]
    sin = sin_ref[...]
    rq = (hb[:, 0:256] * cos + hb[:, 512:768] * sin) * (RET_DK ** -0.5)
    rk = hb[:, 256:512] * cos + hb[:, 768:1024] * sin
    v = hb[:, 1024:1536]
    g = hb[:, 1536:2048]

    r, t, seq = _seg_ids(c, sl)
    tf = t.astype(F32)
    lane = lax.broadcasted_iota(jnp.int32, (1, 256), 1)
    lgl = _per_head(lane, _RET_LOG_GAMMA)
    q_dec = rq * jnp.exp(lgl * (tf + 1.0))
    k_dec = rk * jnp.exp(lgl * (sl - 1.0 - tf))
    k_dec_t = k_dec.T

    rr = lax.broadcasted_iota(jnp.int32, (c, c), 0)
    cc = lax.broadcasted_iota(jnp.int32, (c, c), 1)
    sh = int(math.log2(sl))
    causal = jnp.logical_and((rr >> sh) == (cc >> sh), rr >= cc)
    rel = jnp.maximum(rr - cc, 0).astype(F32)
    cseq = lax.broadcasted_iota(jnp.int32, (1, c), 1) >> sh
    rowi = lax.broadcasted_iota(jnp.int32, (256, 1), 0)
    cd_rows = jnp.exp(_per_head(rowi, _RET_LOG_GAMMA) * float(sl))

    cross = []
    for si in range(ns):
        rows = slice(si * sl, (si + 1) * sl)
        st = s_ref[si]
        parts = []
        for h in range(RET_HEADS):
            mh = jnp.logical_and(lane >= h * RET_DK, lane < (h + 1) * RET_DK)
            parts.append(_bdot(jnp.where(mh, q_dec[rows], 0.0), st))
        cross.append(parts)
        kd = k_dec_t if ns == 1 else jnp.where(cseq == si, k_dec_t, 0.0)
        res = _bdot(kd, v)
        upd = jnp.zeros((256, RET_DV), F32)
        for h in range(RET_HEADS):
            rm = jnp.logical_and(rowi >= h * RET_DK, rowi < (h + 1) * RET_DK)
            upd = upd + jnp.where(rm, res[:, h * 128:(h + 1) * 128], 0.0)
        s_ref[si] = st * cd_rows + upd

    heads = range(RET_HEADS)
    mhs = [jnp.logical_and(lane >= h * RET_DK, lane < (h + 1) * RET_DK) for h in heads]
    scs = [_bdot_nt(jnp.where(mhs[h], rq, 0.0), rk)
           * jnp.where(causal, jnp.exp(_RET_LOG_GAMMA[h] * rel), 0.0) for h in heads]
    os_ = [_bdot(scs[h], v[:, h * 128:(h + 1) * 128]) for h in heads]
    outs = []
    for h in heads:
        oc = cross[0][h] if ns == 1 else jnp.concatenate([cross[si][h] for si in range(ns)], axis=0)
        o = os_[h] + oc
        mu = jnp.mean(o, -1, keepdims=True)
        oc2 = o - mu
        o = oc2 * lax.rsqrt(jnp.mean(oc2 * oc2, -1, keepdims=True) + LN_EPS)
        outs.append(_silu(g[:, h * 128:(h + 1) * 128]) * o)
    o_ref[...] = jnp.concatenate(outs, axis=1).astype(o_ref.dtype)


def _retention(h3, cos_tab, sin_tab, s0, layer_state, ns, sl):
    nseq, length, _ = h3.shape
    c = ns * sl
    grid = (nseq // ns, length // sl)
    kern = functools.partial(_ret_kernel, ns=ns, sl=sl)
    return pl.pallas_call(
        kern,
        grid=grid,
        in_specs=[
            pl.BlockSpec((ns, sl, 2048), lambda b, n: (b, n, COL_RET // 2048)),
            pl.BlockSpec((c, 256), lambda b, n: (n, 0)),
            pl.BlockSpec((c, 256), lambda b, n: (n, 0)),
            pl.BlockSpec((1, ns, 256, RET_DV), lambda b, n: (layer_state, b, 0, 0)),
        ],
        out_specs=[
            _branch_out_spec(ns, sl, grid[1]),
            pl.BlockSpec((ns, 256, RET_DV), lambda b, n: (b, 0, 0)),
        ],
        out_shape=[jax.ShapeDtypeStruct((nseq * length, BRANCH), BF16),
                   jax.ShapeDtypeStruct((nseq, 256, RET_DV), F32)],
        compiler_params=_cparams("parallel", "arbitrary"),
        name="retention",
    )(h3, cos_tab, sin_tab, s0)


def _pool_kernel(u_ref, st_ref, w_ref, sc_ref, o_ref, full_ref, *, ns, sl, pos0, carry):
    c = ns * sl
    n = pl.program_id(1)

    @pl.when(n == 0)
    def _():
        full_ref[:, 1:16, :] = st_ref[0]

    u3 = u_ref[...]
    full_ref[:, 16:16 + sl, :] = u3
    tpos = lax.broadcasted_iota(jnp.int32, (1, sl, 1), 1) + n * sl
    n_avail = (tpos + (pos0 + 1)).astype(F32)
    outs = []
    for gi, w in enumerate(POOL_WINDOWS):
        cols = slice(gi * 128, (gi + 1) * 128)
        ug = u3[:, :, cols]
        acc = ug
        for j in range(1, w):
            acc = acc + full_ref[:, 16 - j:16 - j + sl, cols]
        pooled = acc / jnp.minimum(n_avail, float(w)) - ug
        mixed = _bdot(pooled.reshape(c, 128), w_ref[0, gi])
        outs.append(mixed * sc_ref[0, :, cols])
    if carry:
        full_ref[:, 1:16, :] = full_ref[:, sl + 1:sl + 16, :]
    o_ref[...] = jnp.concatenate(outs, axis=1).astype(o_ref.dtype)


def _pool(h3, state, layer_state, w_pool, pool_scale, layer, ns, sl, pos0):
    nseq, length, _ = h3.shape
    grid = (nseq // ns, length // sl)
    kern = functools.partial(_pool_kernel, ns=ns, sl=sl, pos0=pos0, carry=grid[1] > 1)
    return pl.pallas_call(
        kern,
        grid=grid,
        in_specs=[
            pl.BlockSpec((ns, sl, BRANCH), lambda b, n: (b, n, COL_POOL // BRANCH)),
            pl.BlockSpec((1, ns, POOL_BUF, BRANCH), lambda b, n: (layer_state, b, 0, 0)),
            pl.BlockSpec((1, 4, 128, 128), lambda b, n: (layer, 0, 0, 0)),
            pl.BlockSpec((1, 1, BRANCH), lambda b, n: (layer, 0, 0)),
        ],
        out_specs=_branch_out_spec(ns, sl, grid[1]),
        out_shape=jax.ShapeDtypeStruct((nseq * length, BRANCH), BF16),
        scratch_shapes=[pltpu.VMEM((ns, 16 + sl, BRANCH), F32)],
        compiler_params=_cparams("parallel", "arbitrary"),
        name="multi_pool",
    )(h3, state, w_pool, pool_scale)


S5_SEQS = SUBLANES
S5_SLABS = 2 * S5_CH // LANES


def _s5_pitch(tl):
    return tl + 4


def _s5_kernel(u_ref, hr_ref, hi_ref, bd_ref, cd_ref, ar_ref, ai_ref, d_ref, glu_ref,
               o_ref, or_ref, oi_ref, bu_ref, h_ref, arb_ref, aib_ref, up_ref, *, tl):
    pitch = _s5_pitch(tl)
    n = pl.program_id(1)

    @pl.when(n == 0)
    def _():
        hr = hr_ref[0]
        hi = hi_ref[0]
        for kb in range(2):
            for j in range(8):
                src = slice(kb * 1024 + j * 128, kb * 1024 + (j + 1) * 128)
                h_ref[kb * 16 + j] = hr[:, src]
                h_ref[kb * 16 + 8 + j] = hi[:, src]
        for kb in range(2):
            for j in range(8):
                src = slice(kb * 1024 + j * 128, kb * 1024 + (j + 1) * 128)
                arb_ref[kb * 8 + j] = jnp.broadcast_to(ar_ref[0, :, src], (S5_SEQS, LANES))
                aib_ref[kb * 8 + j] = jnp.broadcast_to(ai_ref[0, :, src], (S5_SEQS, LANES))

        up_ref[...] = jnp.zeros_like(up_ref)

    for s in range(S5_SEQS):
        up_ref[s * pitch:s * pitch + tl, :] = u_ref[s]
    up = up_ref[...]
    upb = up.astype(BF16)
    for kb in range(2):
        bu = jnp.dot(upb[:, kb * 256:(kb + 1) * 256], bd_ref[0, kb],
                     preferred_element_type=F32)
        for j in range(16):
            bu_ref[kb * 16 + j] = bu[:, j * 128:(j + 1) * 128]

    def step(l, carry):
        new = []
        for kb in range(2):
            for j in range(8):
                re = carry[kb * 16 + j]
                im = carry[kb * 16 + 8 + j]
                ar = arb_ref[kb * 8 + j]
                ai = aib_ref[kb * 8 + j]
                idx = pl.ds(l, S5_SEQS, stride=pitch)
                b_re = bu_ref[kb * 16 + j, idx, :]
                b_im = bu_ref[kb * 16 + 8 + j, idx, :]
                n_re = ar * re - ai * im + b_re
                n_im = ar * im + ai * re + b_im
                bu_ref[kb * 16 + j, idx, :] = n_re
                bu_ref[kb * 16 + 8 + j, idx, :] = n_im
                new.append((kb * 16 + j, n_re))
                new.append((kb * 16 + 8 + j, n_im))
        new.sort(key=lambda p: p[0])
        return tuple(p[1] for p in new)

    h0 = tuple(h_ref[i] for i in range(S5_SLABS))
    hf = lax.fori_loop(0, tl, step, h0)
    for i in range(S5_SLABS):
        h_ref[i] = hf[i]

    @pl.when(n == pl.num_programs(1) - 1)
    def _():
        for kb in range(2):
            for j in range(8):
                dst = slice(kb * 1024 + j * 128, kb * 1024 + (j + 1) * 128)
                or_ref[:, dst] = hf[kb * 16 + j]
                oi_ref[:, dst] = hf[kb * 16 + 8 + j]

    ys = []
    for kb in range(2):
        hs = jnp.concatenate([bu_ref[kb * 16 + j].astype(BF16) for j in range(16)], axis=1)
        ys.append(jnp.dot(hs, cd_ref[0, kb], preferred_element_type=F32))
    y = jnp.concatenate(ys, axis=1) + d_ref[0] * up
    act = jax.nn.gelu(y)
    out = act * jax.nn.sigmoid(_bdot(act, glu_ref[0]))
    for s in range(S5_SEQS):
        o_ref[s] = out[s * pitch:s * pitch + tl, :]


def _s5(h3, h_re, h_im, layer_state, bd, cd, ar, ai, d_skip, w_glu, layer, tl):
    nseq, length, _ = h3.shape
    grid = (nseq // S5_SEQS, length // tl)
    kern = functools.partial(_s5_kernel, tl=tl)
    rows = S5_SEQS * _s5_pitch(tl)
    return pl.pallas_call(
        kern,
        grid=grid,
        in_specs=[
            pl.BlockSpec((S5_SEQS, tl, BRANCH), lambda b, n: (b, n, COL_S5 // BRANCH)),
            pl.BlockSpec((1, S5_SEQS, S5_CH), lambda b, n: (layer_state, b, 0)),
            pl.BlockSpec((1, S5_SEQS, S5_CH), lambda b, n: (layer_state, b, 0)),
            pl.BlockSpec((1, 2, 256, 2048), lambda b, n: (layer, 0, 0, 0)),
            pl.BlockSpec((1, 2, 2048, 256), lambda b, n: (layer, 0, 0, 0)),
            pl.BlockSpec((1, 1, S5_CH), lambda b, n: (layer, 0, 0)),
            pl.BlockSpec((1, 1, S5_CH), lambda b, n: (layer, 0, 0)),
            pl.BlockSpec((1, 1, BRANCH), lambda b, n: (layer, 0, 0)),
            pl.BlockSpec((1, BRANCH, BRANCH), lambda b, n: (layer, 0, 0)),
        ],
        out_specs=[
            pl.BlockSpec((S5_SEQS, tl, BRANCH), lambda b, n: (b, n, 0)),
            pl.BlockSpec((S5_SEQS, S5_CH), lambda b, n: (b, 0)),
            pl.BlockSpec((S5_SEQS, S5_CH), lambda b, n: (b, 0)),
        ],
        out_shape=[jax.ShapeDtypeStruct((nseq, length, BRANCH), F32),
                   jax.ShapeDtypeStruct((nseq, S5_CH), F32),
                   jax.ShapeDtypeStruct((nseq, S5_CH), F32)],
        scratch_shapes=[pltpu.VMEM((S5_SLABS, rows, LANES), F32),
                        pltpu.VMEM((S5_SLABS, S5_SEQS, LANES), F32),
                        pltpu.VMEM((S5_SLABS // 2, S5_SEQS, LANES), F32),
                        pltpu.VMEM((S5_SLABS // 2, S5_SEQS, LANES), F32),
                        pltpu.VMEM((rows, BRANCH), F32)],
        compiler_params=_cparams("parallel", "arbitrary"),
        name="s5_ssm",
    )(h3, h_re, h_im, bd, cd, ar, ai, d_skip, w_glu)


def _merge_kernel(x_ref, ba_ref, bb_ref, bc_ref, bd_ref, g_ref, wb_ref, wo_ref, wq_ref, ln_ref,
                  x1_ref, q_ref):
    mixed = None
    for i, br in enumerate((ba_ref, bb_ref, bc_ref, bd_ref)):
        proj = jnp.dot(br[...].astype(BF16), wb_ref[0, i], preferred_element_type=F32)
        term = jax.nn.sigmoid(g_ref[:, i * D_MODEL:(i + 1) * D_MODEL]) * proj
        mixed = term if mixed is None else mixed + term
    y = jnp.dot(mixed.astype(BF16), wo_ref[0], preferred_element_type=F32)
    x1 = _layer_norm(DN_ALPHA * x_ref[...] + y, ln_ref[0, 0:1, :], ln_ref[0, 1:2, :])
    x1_ref[...] = x1
    q_ref[...] = jnp.dot(x1.astype(BF16), wq_ref[0], preferred_element_type=F32)


def _merge(x, h, brs, wb, wo, wq, ln, layer, tm):
    m = x.shape[0]
    tok = lambda w: pl.BlockSpec((tm, w), lambda i: (i, 0))
    return pl.pallas_call(
        _merge_kernel,
        grid=(m // tm,),
        in_specs=[tok(D_MODEL), tok(BRANCH), tok(BRANCH), tok(BRANCH), tok(BRANCH),
                  pl.BlockSpec((tm, 4096), lambda i: (i, COL_GATE // 4096)),
                  pl.BlockSpec((1, N_BRANCH, BRANCH, D_MODEL), lambda i: (layer, 0, 0, 0)),
                  pl.BlockSpec((1, D_MODEL, D_MODEL), lambda i: (layer, 0, 0)),
                  pl.BlockSpec((1, D_MODEL, D_MODEL), lambda i: (layer, 0, 0)),
                  pl.BlockSpec((1, 2, D_MODEL), lambda i: (layer, 0, 0))],
        out_specs=[tok(D_MODEL), tok(D_MODEL)],
        out_shape=[jax.ShapeDtypeStruct((m, D_MODEL), F32)] * 2,
        compiler_params=_cparams("parallel"),
        name="branch_merge",
    )(x, *brs, h, wb, wo, wq, ln)


def _attn_kernel(q_ref, k_ref, v_ref, o_ref):
    q = q_ref[0]
    heads = range(XA_HEADS)
    cols = [slice(h * XA_DH, (h + 1) * XA_DH) for h in heads]
    ss = [_bdot_nt(q[:, cols[h]], k_ref[0, 0, :, cols[h]]) * (XA_DH ** -0.5) for h in heads]
    es = [jnp.exp(s - jnp.max(s, -1, keepdims=True)) for s in ss]
    ps = [e / jnp.sum(e, -1, keepdims=True) for e in es]
    o_ref[0] = jnp.concatenate([_bdot(ps[h], v_ref[0, 0, :, cols[h]]) for h in heads], axis=1)


def _attention(q3, mem_k, mem_v, layer_mem, tq):
    nseq, length, _ = q3.shape
    mem = pl.BlockSpec((1, 1, MEM_LEN, D_MODEL), lambda b, i: (layer_mem, b, 0, 0))
    return pl.pallas_call(
        _attn_kernel,
        grid=(nseq, length // tq),
        in_specs=[pl.BlockSpec((1, tq, D_MODEL), lambda b, i: (b, i, 0)), mem, mem],
        out_specs=pl.BlockSpec((1, tq, D_MODEL), lambda b, i: (b, i, 0)),
        out_shape=jax.ShapeDtypeStruct((nseq, length, D_MODEL), F32),
        compiler_params=_cparams("parallel", "arbitrary"),
        name="memory_attention",
    )(q3, mem_k, mem_v)


def _post_kernel(x1_ref, o_ref, wo_ref, ln_ref, wr_ref, br_ref, x2_ref, comb_ref):
    y = jnp.dot(o_ref[...].astype(BF16), wo_ref[0], preferred_element_type=F32)
    x2 = _layer_norm(DN_ALPHA * x1_ref[...] + y, ln_ref[0, 0:1, :], ln_ref[0, 1:2, :])
    x2_ref[...] = x2
    logits = jnp.dot(x2.astype(BF16), wr_ref[...], preferred_element_type=F32) + br_ref[...]
    e = jnp.exp(logits - jnp.max(logits, -1, keepdims=True))
    p = e / jnp.sum(e, -1, keepdims=True)
    lane = lax.broadcasted_iota(jnp.int32, p.shape, 1)
    lanef = lane.astype(F32)
    grp = lane >> 2
    best = None
    for gidx in range(N_EXPERTS // EXPERTS_PER_GROUP):
        gm = jnp.max(jnp.where(grp == gidx, p, -1.0), -1, keepdims=True)
        if best is None:
            best, gi = gm, jnp.zeros(gm.shape, jnp.int32)
        else:
            upd = gm > best
            gi = jnp.where(upd, gidx, gi)
            best = jnp.where(upd, gm, best)
    cand = jnp.where(grp == gi, p, -1.0)
    m1 = jnp.max(cand, -1, keepdims=True)
    e1 = jnp.min(jnp.where(cand == m1, lanef, 1e9), -1, keepdims=True)
    cand2 = jnp.where(lanef == e1, -2.0, cand)
    m2 = jnp.max(cand2, -1, keepdims=True)
    e2 = jnp.min(jnp.where(cand2 == m2, lanef, 1e9), -1, keepdims=True)
    tot = m1 + m2
    comb = jnp.where(lanef == e1, m1 / tot, 0.0) + jnp.where(lanef == e2, m2 / tot, 0.0)
    comb_ref[...] = comb + jnp.where(lane == GROUP_LANE, gi.astype(F32), 0.0)


def _post(x1, o, wo, ln, wr, br, layer, tm):
    m = x1.shape[0]
    tok = lambda w: pl.BlockSpec((tm, w), lambda i: (i, 0))
    return pl.pallas_call(
        _post_kernel,
        grid=(m // tm,),
        in_specs=[tok(D_MODEL), tok(D_MODEL),
                  pl.BlockSpec((1, D_MODEL, D_MODEL), lambda i: (layer, 0, 0)),
                  pl.BlockSpec((1, 2, D_MODEL), lambda i: (layer, 0, 0)),
                  pl.BlockSpec((D_MODEL, LANES), lambda i: (0, 0)),
                  pl.BlockSpec((1, LANES), lambda i: (0, 0))],
        out_specs=[tok(D_MODEL), tok(LANES)],
        out_shape=[jax.ShapeDtypeStruct((m, D_MODEL), F32), jax.ShapeDtypeStruct((m, LANES), F32)],
        compiler_params=_cparams("parallel"),
        name="attn_out_router",
    )(x1, o, wo, ln, wr, br)


def _moe_kernel(x_ref, comb_ref, w1_ref, w3_ref, w2_ref, ln_ref, o_ref, acc_ref, xb_ref):
    e = pl.program_id(1)

    @pl.when(e == 0)
    def _():
        acc_ref[...] = jnp.zeros_like(acc_ref)
        xb_ref[...] = x_ref[...].astype(BF16)

    comb = comb_ref[...]
    lane = lax.broadcasted_iota(jnp.int32, comb.shape, 1)
    ce = jnp.sum(jnp.where(lane == e, comb, 0.0), -1, keepdims=True)
    xb = xb_ref[...]
    h1 = jnp.dot(xb, w1_ref[0, 0].astype(BF16), preferred_element_type=F32)
    h3 = jnp.dot(xb, w3_ref[0, 0].astype(BF16), preferred_element_type=F32)
    hid = _silu(h1) * h3 * ce
    acc_ref[...] += jnp.dot(hid.astype(BF16), w2_ref[0, 0].astype(BF16),
                            preferred_element_type=F32)

    @pl.when(e == pl.num_programs(1) - 1)
    def _():
        o_ref[...] = _layer_norm(DN_ALPHA * x_ref[...] + acc_ref[...],
                                 ln_ref[0, 0:1, :], ln_ref[0, 1:2, :])


def _moe(x2, comb, w1, w3, w2, ln, layer, tm):
    m = x2.shape[0]
    w_up = pl.BlockSpec((1, 1, D_MODEL, D_EXPERT), lambda i, e: (layer, e, 0, 0))
    return pl.pallas_call(
        _moe_kernel,
        grid=(m // tm, N_EXPERTS),
        in_specs=[pl.BlockSpec((tm, D_MODEL), lambda i, e: (i, 0)),
                  pl.BlockSpec((tm, LANES), lambda i, e: (i, 0)),
                  w_up, w_up,
                  pl.BlockSpec((1, 1, D_EXPERT, D_MODEL), lambda i, e: (layer, e, 0, 0)),
                  pl.BlockSpec((1, 2, D_MODEL), lambda i, e: (layer, 0, 0))],
        out_specs=pl.BlockSpec((tm, D_MODEL), lambda i, e: (i, 0)),
        out_shape=jax.ShapeDtypeStruct((m, D_MODEL), F32),
        scratch_shapes=[pltpu.VMEM((tm, D_MODEL), F32), pltpu.VMEM((tm, D_MODEL), BF16)],
        compiler_params=_cparams("parallel", "arbitrary"),
        name="moe_experts",
    )(x2, comb, w1, w3, w2, ln)


N_GROUPS = N_EXPERTS // EXPERTS_PER_GROUP
GROUP_LANE = N_EXPERTS


def _bucket_cap(tm):
    return tm // N_GROUPS + tm // 16


def _split_bf16(x):
    hi = x.astype(BF16)
    return hi, (x - hi.astype(F32)).astype(BF16)


def _moe_sort_kernel(x_ref, comb_ref, xs_ref, gs_ref, dest_ref, cnt_ref, *, cap):
    tm = x_ref.shape[0]
    slots = N_GROUPS * cap
    comb = comb_ref[...]
    lanef = lax.broadcasted_iota(jnp.int32, comb.shape, 1).astype(F32)
    gid = comb[:, GROUP_LANE:GROUP_LANE + 1]
    onehot = jnp.where(lanef == gid, 1.0, 0.0)
    rr = lax.broadcasted_iota(jnp.int32, (tm, tm), 0)
    cc = lax.broadcasted_iota(jnp.int32, (tm, tm), 1)
    tril = jnp.where(rr >= cc, 1.0, 0.0).astype(BF16)
    cs = jnp.dot(tril, onehot.astype(BF16), preferred_element_type=F32)
    rank = jnp.sum(onehot * cs, -1, keepdims=True) - 1.0
    dest = gid * float(cap) + rank
    cnt_ref[0] = jnp.broadcast_to(cs[tm - 1:tm, :], (SUBLANES, LANES))
    dest_b = jnp.broadcast_to(dest, (tm, LANES))
    dest_ref[...] = dest_b
    dest_row = dest_b.T[0:1, :]
    srow = lax.broadcasted_iota(jnp.int32, (slots, 1), 0).astype(F32)
    perm = jnp.where(srow == dest_row, 1.0, 0.0).astype(BF16)
    xs = jnp.dot(perm, x_ref[...].astype(BF16), preferred_element_type=F32).astype(xs_ref.dtype)
    hi, lo = _split_bf16(comb)
    gs = (jnp.dot(perm, hi, preferred_element_type=F32)
          + jnp.dot(perm, lo, preferred_element_type=F32))
    for g in range(N_GROUPS):
        xs_ref[g, 0] = xs[g * cap:(g + 1) * cap, :]
        gs_ref[g, 0] = gs[g * cap:(g + 1) * cap, :]


def _moe_sort(x2, comb, tm):
    m = x2.shape[0]
    cap = _bucket_cap(tm)
    nt = m // tm
    return pl.pallas_call(
        functools.partial(_moe_sort_kernel, cap=cap),
        grid=(nt,),
        in_specs=[pl.BlockSpec((tm, D_MODEL), lambda i: (i, 0)),
                  pl.BlockSpec((tm, LANES), lambda i: (i, 0))],
        out_specs=[pl.BlockSpec((N_GROUPS, 1, cap, D_MODEL), lambda i: (0, i, 0, 0)),
                   pl.BlockSpec((N_GROUPS, 1, cap, LANES), lambda i: (0, i, 0, 0)),
                   pl.BlockSpec((tm, LANES), lambda i: (i, 0)),
                   pl.BlockSpec((1, SUBLANES, LANES), lambda i: (i, 0, 0))],
        out_shape=[jax.ShapeDtypeStruct((N_GROUPS, nt, cap, D_MODEL), BF16),
                   jax.ShapeDtypeStruct((N_GROUPS, nt, cap, LANES), F32),
                   jax.ShapeDtypeStruct((m, LANES), F32),
                   jax.ShapeDtypeStruct((nt, SUBLANES, LANES), F32)],
        compiler_params=_cparams("parallel"),
        name="moe_bucket_sort",
    )(x2, comb)


def _moe_bucket_kernel(xs_ref, gs_ref, w1_ref, w3_ref, w2_ref, o_ref, acc_ref):
    g = pl.program_id(0)
    j = pl.program_id(2)
    rows = acc_ref.shape[0]

    @pl.when(j == 0)
    def _():
        acc_ref[...] = jnp.zeros_like(acc_ref)

    gates = gs_ref[0].reshape(rows, LANES)
    lane = lax.broadcasted_iota(jnp.int32, gates.shape, 1)
    ce = jnp.sum(jnp.where(lane == g * EXPERTS_PER_GROUP + j, gates, 0.0), -1, keepdims=True)
    xb = xs_ref[0].reshape(rows, D_MODEL)
    h1 = jnp.dot(xb, w1_ref[0, 0].astype(BF16), preferred_element_type=F32)
    h3 = jnp.dot(xb, w3_ref[0, 0].astype(BF16), preferred_element_type=F32)
    hid = _silu(h1) * h3 * ce
    acc_ref[...] += jnp.dot(hid.astype(BF16), w2_ref[0, 0].astype(BF16),
                            preferred_element_type=F32)

    @pl.when(j == pl.num_programs(2) - 1)
    def _():
        o_ref[0] = acc_ref[...].reshape(o_ref.shape[1:])


def _moe_buckets(xs, gs, w1, w3, w2, layer, tpq):
    _, nt, cap, _ = xs.shape

    def bucket(width):
        return pl.BlockSpec((1, tpq, cap, width), lambda g, q, j: (g, q, 0, 0))

    def expert(rows, cols):
        return pl.BlockSpec((1, 1, rows, cols),
                            lambda g, q, j: (layer, g * EXPERTS_PER_GROUP + j, 0, 0))

    return pl.pallas_call(
        _moe_bucket_kernel,
        grid=(N_GROUPS, nt // tpq, EXPERTS_PER_GROUP),
        in_specs=[bucket(D_MODEL), bucket(LANES),
                  expert(D_MODEL, D_EXPERT), expert(D_MODEL, D_EXPERT), expert(D_EXPERT, D_MODEL)],
        out_specs=bucket(D_MODEL),
        out_shape=jax.ShapeDtypeStruct((N_GROUPS, nt, cap, D_MODEL), F32),
        scratch_shapes=[pltpu.VMEM((tpq * cap, D_MODEL), F32)],
        compiler_params=_cparams("parallel", "parallel", "arbitrary"),
        name="moe_bucket_experts",
    )(xs, gs, w1, w3, w2)


def _moe_unsort_kernel(x_ref, dest_ref, ys_ref, ln_ref, o_ref):
    _, _, cap, _ = ys_ref.shape
    slots = N_GROUPS * cap
    dest = dest_ref[:, 0:1]
    slot = lax.broadcasted_iota(jnp.int32, (1, slots), 1).astype(F32)
    perm_t = jnp.where(dest == slot, 1.0, 0.0).astype(BF16)
    hi, lo = _split_bf16(ys_ref[...].reshape(slots, D_MODEL))
    y = (jnp.dot(perm_t, hi, preferred_element_type=F32)
         + jnp.dot(perm_t, lo, preferred_element_type=F32))
    o_ref[...] = _layer_norm(DN_ALPHA * x_ref[...] + y, ln_ref[0, 0:1, :], ln_ref[0, 1:2, :])


def _moe_unsort(x2, dest, ys, ln, layer, tm):
    m = x2.shape[0]
    _, _, cap, _ = ys.shape
    return pl.pallas_call(
        _moe_unsort_kernel,
        grid=(m // tm,),
        in_specs=[pl.BlockSpec((tm, D_MODEL), lambda i: (i, 0)),
                  pl.BlockSpec((tm, LANES), lambda i: (i, 0)),
                  pl.BlockSpec((N_GROUPS, 1, cap, D_MODEL), lambda i: (0, i, 0, 0)),
                  pl.BlockSpec((1, 2, D_MODEL), lambda i: (layer, 0, 0))],
        out_specs=pl.BlockSpec((tm, D_MODEL), lambda i: (i, 0)),
        out_shape=jax.ShapeDtypeStruct((m, D_MODEL), F32),
        compiler_params=_cparams("parallel"),
        name="moe_unsort_norm",
    )(x2, dest, ys, ln)


def _moe_bucketed(x2, comb, w1, w3, w2, ln, layer, tm, tpq):
    xs, gs, dest, cnt = _moe_sort(x2, comb, tm)
    overflow = jnp.max(cnt[:, 0, :N_GROUPS]) > _bucket_cap(tm)

    def bucketed():
        return _moe_unsort(x2, dest, _moe_buckets(xs, gs, w1, w3, w2, layer, tpq), ln, layer, tm)

    def dense():
        return _moe(x2, comb, w1, w3, w2, ln, layer, tm)

    return lax.cond(overflow, dense, bucketed)


def _prep_w_in(w_in):
    o = np.cumsum((0, GDN_QKV, 512, 4, 4, 256, 256, 512, 512, 512, 512, 4096))
    seg = lambda i: w_in[:, :, o[i]:o[i + 1]]
    swap = np.arange(256) ^ 1
    main = jnp.concatenate([seg(0), seg(1), seg(4), seg(5), seg(4)[:, :, swap], seg(5)[:, :, swap],
                            seg(6), seg(7), seg(10), seg(8), seg(9)], axis=-1).astype(BF16)
    ba = jnp.concatenate([seg(2), seg(3), jnp.zeros((DEPTH, D_MODEL, LANES - 8), w_in.dtype)],
                         axis=-1).astype(BF16)
    return main, ba


def _rope_tables(pos):
    inv_freq = 1.0 / (ROPE_BASE ** jnp.linspace(0.0, 1.0, RET_DK // 2, dtype=F32))
    ang = pos.astype(F32)[:, None] * inv_freq
    cos = jnp.repeat(jnp.cos(ang), 2, axis=1)
    sin = jnp.repeat(jnp.sin(ang), 2, axis=1)
    sign = jnp.tile(jnp.array([-1.0, 1.0], F32), RET_DK // 2)
    return jnp.tile(cos, (1, RET_HEADS)), jnp.tile(sin * sign, (1, RET_HEADS))


def _prep_s5(a_re, a_im, log_dt, b_re, b_im, c_re, c_im):
    dt = jnp.exp(log_dt.astype(F32))[..., None]
    mag = jnp.exp(a_re * dt)
    ab_re, ab_im = mag * jnp.cos(a_im * dt), mag * jnp.sin(a_im * dt)
    den = a_re * a_re + a_im * a_im
    coef_re = ((ab_re - 1.0) * a_re + ab_im * a_im) / den
    coef_im = (ab_im * a_re - (ab_re - 1.0) * a_im) / den
    bb_re = coef_re[..., None] * b_re - coef_im[..., None] * b_im
    bb_im = coef_re[..., None] * b_im + coef_im[..., None] * b_re
    eye = jnp.eye(16, dtype=F32)

    def pack_b(bb):
        x = bb.reshape(DEPTH, 2, 16, S5_STATE, S5_GROUP)
        return jnp.einsum('dkgpc,gh->dkgchp', x, eye).reshape(DEPTH, 2, 256, 1024)

    def pack_c(cm):
        x = cm.reshape(DEPTH, 2, 16, S5_GROUP, S5_STATE)
        return jnp.einsum('dkgcp,gh->dkgphc', x, eye).reshape(DEPTH, 2, 1024, 256)

    bd = jnp.concatenate([pack_b(bb_re), pack_b(bb_im)], axis=-1).astype(BF16)
    cd = jnp.concatenate([pack_c(c_re.astype(F32)), -pack_c(c_im.astype(F32))], axis=-2).astype(BF16)
    ar = ab_re.reshape(DEPTH, 1, S5_CH)
    ai = ab_im.reshape(DEPTH, 1, S5_CH)
    return bd, cd, ar, ai


def kernel(x_prompt, x_sample, mem_prompt, state_gdn_conv, state_gdn, state_ret, state_pool,
           state_s5_re, state_s5_im, cache_mem_k, cache_mem_v, w_in, gdn_conv_w, gdn_a_log,
           gdn_dt_bias, gdn_norm_w, pool_w, pool_scale, s5_a_re, s5_a_im, s5_log_dt, s5_b_re,
           s5_b_im, s5_c_re, s5_c_im, s5_d, s5_w_glu, w_branch, w_out, xa_w_q, xa_w_k, xa_w_v,
           xa_w_o, ln_g, ln_b, w_router, b_router, moe_w1, moe_w3, moe_w2):
    bp, seq, _ = x_prompt.shape
    bs, dseq, _ = x_sample.shape
    past = 16384

    w_main, w_ba = _prep_w_in(w_in)
    gp = jnp.zeros((DEPTH, 2, LANES), F32)
    gp = gp.at[:, 0, 4:8].set(gdn_a_log.astype(F32)).at[:, 1, 4:8].set(gdn_dt_bias.astype(F32))
    nw = gdn_norm_w.astype(F32).reshape(DEPTH, 1, GDN_DK)
    cw = gdn_conv_w.astype(F32)
    pw = pool_w.astype(BF16)
    psc = pool_scale.astype(F32).reshape(DEPTH, 1, BRANCH)
    bd, cd, ar, ai = _prep_s5(s5_a_re.astype(F32), s5_a_im.astype(F32), s5_log_dt,
                              s5_b_re.astype(F32), s5_b_im.astype(F32), s5_c_re, s5_c_im)
    d_skip = s5_d.astype(F32).reshape(DEPTH, 1, BRANCH)
    glu = s5_w_glu.astype(BF16)
    wb = w_branch.astype(BF16)
    wo = w_out.astype(BF16)
    wq, wk, wv, wxo = (w.astype(BF16) for w in (xa_w_q, xa_w_k, xa_w_v, xa_w_o))
    ln = jnp.stack([ln_g.astype(F32), ln_b.astype(F32)], axis=2)
    ln1, ln2, ln3 = ln[:, 0], ln[:, 1], ln[:, 2]
    wr = jnp.concatenate([w_router, jnp.zeros((D_MODEL, LANES - N_EXPERTS), w_router.dtype)],
                         axis=1).astype(BF16)
    br = jnp.concatenate([b_router.astype(F32), jnp.full((LANES - N_EXPERTS,), -1e30, F32)])[None]

    cos_p, sin_p = _rope_tables(jnp.arange(seq))
    cos_s, sin_s = _rope_tables(past + jnp.arange(dseq))
    ns_s = 16
    cos_s, sin_s = jnp.tile(cos_s, (ns_s, 1)), jnp.tile(sin_s, (ns_s, 1))

    zeros = lambda *s: jnp.zeros((1,) + s, F32)
    z_conv, z_gdn = zeros(bp, GDN_CONV - 1, GDN_QKV), zeros(bp, GDN_HEADS, 128, 128)
    z_ret, z_pool, z_s5 = zeros(bp, 256, RET_DV), zeros(bp, POOL_BUF, BRANCH), zeros(bp, S5_CH)
    st_ret = state_ret.reshape(DEPTH, bs, 256, RET_DV)
    st_s5r = state_s5_re.reshape(DEPTH, bs, S5_CH)
    st_s5i = state_s5_im.reshape(DEPTH, bs, S5_CH)
    mem2d = mem_prompt.reshape(bp * MEM_LEN, D_MODEL)
    ck = cache_mem_k.reshape(DEPTH, bs, MEM_LEN, D_MODEL)
    cv = cache_mem_v.reshape(DEPTH, bs, MEM_LEN, D_MODEL)

    def block(x2d, nseq, length, layer, mem_k, mem_v, layer_mem, states, layer_state, cfg):
        conv0, gdn0, ret0, pool0, s5r0, s5i0 = states
        tm = cfg["tm"]
        h, hba = _in_proj(x2d, w_main, w_ba, layer, tm, 1024)
        h3 = h.reshape(nseq, length, H_COLS)
        ba3 = hba.reshape(nseq, length, LANES)
        br_a, new_gdn = _gdn(h3, ba3, conv0, gdn0, layer_state, cw, gp, nw, layer,
                             cfg["gdn_nb"], cfg["ns"], cfg["gdn_rl"], cfg["gdn_sl"])
        br_b, new_ret = _retention(h3, cfg["cos"], cfg["sin"], ret0, layer_state,
                                   cfg["ns"], cfg["ret_sl"])
        br_c = _pool(h3, pool0, layer_state, pw, psc, layer, cfg["ns"], cfg["pool_sl"], cfg["pos0"])
        br_d, new_re, new_im = _s5(h3, s5r0, s5i0, layer_state, bd, cd, ar, ai, d_skip, glu,
                                   layer, cfg["s5_tl"])
        brs = [br_a, br_b, br_c, br_d.reshape(nseq * length, BRANCH)]
        x1, q = _merge(x2d, h, brs, wb, wo, wq, ln1, layer, cfg["tm_merge"])
        o = _attention(q.reshape(nseq, length, D_MODEL), mem_k, mem_v, layer_mem, cfg["tq"])
        x2, comb = _post(x1, o.reshape(nseq * length, D_MODEL), wxo, ln2, wr, br, layer, tm)
        if cfg["moe_tpq"]:
            x3 = _moe_bucketed(x2, comb, moe_w1, moe_w3, moe_w2, ln3, layer, tm, cfg["moe_tpq"])
        else:
            x3 = _moe(x2, comb, moe_w1, moe_w3, moe_w2, ln3, layer, tm)
        new_conv = h3[:, length - (GDN_CONV - 1):, COL_QKV:COL_QKV + GDN_QKV]
        pool_u = h3[:, :, COL_POOL:COL_POOL + BRANCH]
        return x3, (new_conv, new_gdn, new_ret.reshape(nseq, RET_HEADS, RET_DK, RET_DV), pool_u,
                    new_re.reshape(nseq, S5_GROUPS, S5_STATE), new_im.reshape(nseq, S5_GROUPS, S5_STATE))

    cfg_p = dict(tm=1024, tm_merge=512, ns=1, gdn_nb=1, gdn_rl=256, gdn_sl=64, ret_sl=128,
                 pool_sl=512, s5_tl=128, tq=512, pos0=0, cos=cos_p, sin=sin_p, moe_tpq=4)
    cfg_s = dict(tm=1024, tm_merge=512, ns=ns_s, gdn_nb=1, gdn_rl=dseq, gdn_sl=dseq, ret_sl=dseq,
                 pool_sl=dseq, s5_tl=dseq, tq=dseq, pos0=past, cos=cos_s, sin=sin_s, moe_tpq=0)

    yp = x_prompt.reshape(bp * seq, D_MODEL)
    ys = x_sample.reshape(bs * dseq, D_MODEL)
    p_out, s_out = [], []
    for l in range(DEPTH):
        mem_k = _matmul(mem2d, wk, l, 1024, 1024)
        mem_v = _matmul(mem2d, wv, l, 1024, 1024)
        mk4 = mem_k.reshape(1, bp, MEM_LEN, D_MODEL)
        mv4 = mem_v.reshape(1, bp, MEM_LEN, D_MODEL)
        yp, st = block(yp, bp, seq, l, mk4, mv4, 0,
                       (z_conv, z_gdn, z_ret, z_pool, z_s5, z_s5), 0, cfg_p)
        conv, gdn, ret, pool_u, s5r, s5i = st
        p_out.append((conv, gdn, ret, pool_u[:, seq - POOL_BUF:], s5r, s5i,
                      mem_k.reshape(bp, MEM_LEN, XA_HEADS, XA_DH),
                      mem_v.reshape(bp, MEM_LEN, XA_HEADS, XA_DH)))
        ys, st = block(ys, bs, dseq, l, ck, cv, l,
                       (state_gdn_conv, state_gdn, st_ret, state_pool, st_s5r, st_s5i), l, cfg_s)
        conv, gdn, ret, pool_u, s5r, s5i = st
        new_pool = jnp.concatenate([state_pool[l][:, dseq:], pool_u], axis=1)
        s_out.append((conv, gdn, ret, new_pool, s5r, s5i))
    p_st = [jnp.stack(t) for t in zip(*p_out)]
    s_st = [jnp.stack(t) for t in zip(*s_out)]
    return (yp.reshape(bp, seq, D_MODEL), ys.reshape(bs, dseq, D_MODEL), *p_st, *s_st)
```

```python
import functools
import math

import jax
import jax.numpy as jnp
import numpy as np
from jax import lax
from jax.experimental import pallas as pl
from jax.experimental.pallas import tpu as pltpu

F32 = jnp.float32
BF16 = jnp.bfloat16

D_MODEL = 1024
DEPTH = 4
BRANCH = 512
N_BRANCH = 4
GDN_HEADS = 4
GDN_DK = 128
GDN_QKV = 1536
GDN_CONV = 4
RET_HEADS = 4
RET_DK = 64
RET_DV = 128
ROPE_BASE = 10000.0
POOL_WINDOWS = (2, 4, 8, 16)
POOL_BUF = 15
S5_GROUPS = 32
S5_GROUP = 16
S5_STATE = 64
S5_CH = S5_GROUPS * S5_STATE
MEM_LEN = 256
XA_HEADS = 4
XA_DH = 256
N_EXPERTS = 16
EXPERTS_PER_GROUP = 4
D_EXPERT = 512
DN_ALPHA = (2.0 * DEPTH) ** 0.25
LN_EPS = 1e-5
RMS_EPS = 1e-6

LANES = 128
SUBLANES = 8
VMEM_LIMIT = 52 * 1024 * 1024

COL_QKV = 0
COL_Z = 1536
COL_RET = 2048
COL_GATE = 4096
COL_POOL = 8192
COL_S5 = 8704
H_COLS = 9216


def _cparams(*sem):
    return pltpu.CompilerParams(dimension_semantics=sem, vmem_limit_bytes=VMEM_LIMIT)


def _bdot(a, b):
    return jnp.dot(a.astype(BF16), b.astype(BF16), preferred_element_type=F32)


def _bdot_nt(a, b):
    return lax.dot_general(a.astype(BF16), b.astype(BF16), (((1,), (1,)), ((), ())),
                           preferred_element_type=F32)


def _silu(x):
    return x * jax.nn.sigmoid(x)


def _layer_norm(x, g, b):
    mu = jnp.mean(x, -1, keepdims=True)
    xc = x - mu
    var = jnp.mean(xc * xc, -1, keepdims=True)
    return xc * lax.rsqrt(var + LN_EPS) * g + b


def _mm_kernel(x_ref, w_ref, o_ref, xb_ref):
    @pl.when(pl.program_id(1) == 0)
    def _():
        xb_ref[...] = x_ref[...].astype(BF16)

    o_ref[...] = jnp.dot(xb_ref[...], w_ref[0], preferred_element_type=F32)


def _matmul(x, w, layer, tm, tn):
    m, k = x.shape
    n = w.shape[-1]
    return pl.pallas_call(
        _mm_kernel,
        grid=(m // tm, n // tn),
        in_specs=[pl.BlockSpec((tm, k), lambda i, j: (i, 0)),
                  pl.BlockSpec((1, k, tn), lambda i, j: (layer, 0, j))],
        out_specs=pl.BlockSpec((tm, tn), lambda i, j: (i, j)),
        out_shape=jax.ShapeDtypeStruct((m, n), F32),
        scratch_shapes=[pltpu.VMEM((tm, k), BF16)],
        compiler_params=_cparams("parallel", "arbitrary"),
        name="token_matmul",
    )(x, w)


def _inproj_kernel(x_ref, w_ref, wba_ref, o_ref, ba_ref, xb_ref):
    @pl.when(pl.program_id(1) == 0)
    def _():
        xb_ref[...] = x_ref[...].astype(BF16)
        ba_ref[...] = jnp.dot(xb_ref[...], wba_ref[0], preferred_element_type=F32)

    o_ref[...] = jnp.dot(xb_ref[...], w_ref[0], preferred_element_type=F32)


def _in_proj(x, w, w_ba, layer, tm, tn):
    m, k = x.shape
    n = w.shape[-1]
    return pl.pallas_call(
        _inproj_kernel,
        grid=(m // tm, n // tn),
        in_specs=[pl.BlockSpec((tm, k), lambda i, j: (i, 0)),
                  pl.BlockSpec((1, k, tn), lambda i, j: (layer, 0, j)),
                  pl.BlockSpec((1, k, LANES), lambda i, j: (layer, 0, 0))],
        out_specs=[pl.BlockSpec((tm, tn), lambda i, j: (i, j)),
                   pl.BlockSpec((tm, LANES), lambda i, j: (i, 0))],
        out_shape=[jax.ShapeDtypeStruct((m, n), F32), jax.ShapeDtypeStruct((m, LANES), F32)],
        scratch_shapes=[pltpu.VMEM((tm, k), BF16)],
        compiler_params=_cparams("parallel", "arbitrary"),
        name="in_proj",
    )(x, w, w_ba)


def _branch_out_spec(ns, sl, n_chunks):
    return pl.BlockSpec((ns * sl, BRANCH), lambda b, n: (b * n_chunks + n, 0))


def _seg_ids(c, sl):
    r = lax.broadcasted_iota(jnp.int32, (c, 1), 0)
    sh = int(math.log2(sl))
    return r, r & (sl - 1), r >> sh


def _gdn_kernel(qkv_ref, z_ref, ba_ref, cs_ref, s0_ref, cw_ref, gp_ref, nw_ref,
                o_ref, s_ref, full_ref, *, nb, ns, rl, sl):
    n = pl.program_id(1)

    @pl.when(n == 0)
    def _():
        full_ref[:, 5:8, :] = cs_ref[0]
        s_ref[...] = s0_ref[0]

    for bi in range(nb):
        _gdn_block(qkv_ref, z_ref, ba_ref, cw_ref, gp_ref, nw_ref, o_ref, s_ref, full_ref,
                   bi=bi, ns=ns, rl=rl, sl=sl)


def _gdn_block(qkv_ref, z_ref, ba_ref, cw_ref, gp_ref, nw_ref, o_ref, s_ref, full_ref,
               *, bi, ns, rl, sl):
    c = ns * rl
    nsps = rl // sl
    sq = slice(bi * ns, (bi + 1) * ns)

    u3 = qkv_ref[sq]
    full_ref[sq, 8:8 + rl, :] = u3
    cw = cw_ref[0]
    acc = (cw[3:4] * u3 + cw[2:3] * full_ref[sq, 7:7 + rl, :] + cw[1:2] * full_ref[sq, 6:6 + rl, :]
           + cw[0:1] * full_ref[sq, 5:5 + rl, :])
    full_ref[sq, 5:8, :] = full_ref[sq, 5 + rl:8 + rl, :]
    qkv = _silu(acc).reshape(c, GDN_QKV)

    ba = ba_ref[sq].reshape(c, LANES)
    beta_t = jax.nn.sigmoid(ba)
    xs = ba + gp_ref[0, 1:2, :]
    softplus = jnp.maximum(xs, 0.0) + jnp.log1p(jnp.exp(-jnp.abs(xs)))
    g_t = -jnp.exp(gp_ref[0, 0:1, :]) * softplus

    r, t, seq = _seg_ids(c, sl)
    gc = g_t
    s = 1
    while s < sl:
        gc = gc + jnp.where(t >= s, pltpu.roll(gc, s, 0), 0.0)
        s *= 2
    tot = gc
    s = 1
    while s < sl:
        tot = jnp.where(t + s < sl, pltpu.roll(tot, c - s, 0), tot)
        s *= 2
    rc = tot - gc
    if c < LANES:
        gc_pad = jnp.concatenate([gc, jnp.zeros((LANES - c, LANES), F32)], axis=0)
    else:
        gc_pad = gc
    gc_t = gc_pad.T

    rr = lax.broadcasted_iota(jnp.int32, (c, c), 0)
    cc = lax.broadcasted_iota(jnp.int32, (c, c), 1)
    sh = int(math.log2(sl))
    same = (rr >> sh) == (cc >> sh)
    causal = jnp.logical_and(same, rr >= cc)
    strict = jnp.logical_and(same, rr > cc)
    eye = (rr == cc).astype(F32)

    z = z_ref[sq].reshape(c, BRANCH)
    nw = nw_ref[0]
    heads = range(GDN_HEADS)

    qs, ks, kbs, q_decs, k_dec_ts, decays, rhss = [], [], [], [], [], [], []
    for h in heads:
        q = qkv[:, h * 128:(h + 1) * 128]
        k = qkv[:, 512 + h * 128:512 + (h + 1) * 128]
        v = qkv[:, 1024 + h * 128:1024 + (h + 1) * 128]
        q = q * lax.rsqrt(jnp.sum(q * q, -1, keepdims=True) + RMS_EPS) * (GDN_DK ** -0.5)
        k = k * lax.rsqrt(jnp.sum(k * k, -1, keepdims=True) + RMS_EPS)
        beta = beta_t[:, h:h + 1]
        gcc = gc[:, 4 + h:5 + h]
        gcr = gc_t[4 + h:5 + h, 0:c]
        eg = jnp.exp(gcc)
        kb = k * beta
        qs.append(q)
        ks.append(k)
        kbs.append(kb)
        q_decs.append(q * eg)
        k_dec_ts.append((k * jnp.exp(rc[:, 4 + h:5 + h])).T)
        decays.append(jnp.where(causal, jnp.exp(jnp.where(causal, gcc - gcr, 0.0)), 0.0))
        rhss.append(jnp.concatenate([v * beta, kb * eg], axis=1))
    npows = [-jnp.where(strict, _bdot_nt(kbs[h], ks[h]) * decays[h], 0.0) for h in heads]
    sms = [eye + npows[h] for h in heads]
    if sl > 2:
        npows = [_bdot(npows[h], npows[h]) for h in heads]
        m = 2
        while 2 * m < sl:
            prods = [_bdot(npows[h], jnp.concatenate([npows[h], sms[h]], axis=1)) for h in heads]
            npows = [p[:, :c] for p in prods]
            sms = [sms[h] + prods[h][:, c:] for h in heads]
            m *= 2
        sms = [sms[h] + _bdot(npows[h], sms[h]) for h in heads]
    sols = [_bdot(sms[h], rhss[h]) for h in heads]
    attns = [jnp.where(causal, _bdot_nt(qs[h], ks[h]) * decays[h], 0.0) for h in heads]

    v_parts = [[] for _ in heads]
    o_parts = [[] for _ in heads]
    for si in range(ns):
        sts = [s_ref[bi * ns + si, h] for h in heads]
        for j in range(nsps):
            sg = si * nsps + j
            rows = slice(sg * sl, (sg + 1) * sl)
            for h in heads:
                v_j = sols[h][rows, :128] - _bdot(sols[h][rows, 128:], sts[h])
                o_parts[h].append(_bdot(q_decs[h][rows], sts[h]))
                v_parts[h].append(v_j)
                pieces = [v_j]
                if sg > 0:
                    pieces.insert(0, jnp.zeros((sg * sl, 128), F32))
                if (sg + 1) * sl < c:
                    pieces.append(jnp.zeros((c - (sg + 1) * sl, 128), F32))
                vz = v_j if len(pieces) == 1 else jnp.concatenate(pieces, axis=0)
                last = jnp.exp(tot[sg * sl:sg * sl + 1, 4 + h:5 + h])
                sts[h] = sts[h] * last + _bdot(k_dec_ts[h], vz)
        for h in heads:
            s_ref[bi * ns + si, h] = sts[h]

    outs = []
    for h in heads:
        v_new = v_parts[h][0] if len(v_parts[h]) == 1 else jnp.concatenate(v_parts[h], axis=0)
        o_cross = o_parts[h][0] if len(o_parts[h]) == 1 else jnp.concatenate(o_parts[h], axis=0)
        o = o_cross + _bdot(attns[h], v_new)
        o = o * lax.rsqrt(jnp.mean(o * o, -1, keepdims=True) + RMS_EPS) * nw
        outs.append(o * _silu(z[:, h * 128:(h + 1) * 128]))
    o_ref[bi] = jnp.concatenate(outs, axis=1).astype(o_ref.dtype)


def _gdn(h3, ba3, conv_state, s0, layer_state, cw, gp, nw, layer, nb, ns, rl, sl):
    nseq, length, _ = h3.shape
    assert ns == 1 or rl == length
    nq = nb * ns
    grid = (nseq // nq, length // rl)
    kern = functools.partial(_gdn_kernel, nb=nb, ns=ns, rl=rl, sl=sl)
    branch, new_state = pl.pallas_call(
        kern,
        grid=grid,
        in_specs=[
            pl.BlockSpec((nq, rl, GDN_QKV), lambda b, n: (b, n, COL_QKV // GDN_QKV)),
            pl.BlockSpec((nq, rl, BRANCH), lambda b, n: (b, n, COL_Z // BRANCH)),
            pl.BlockSpec((nq, rl, LANES), lambda b, n: (b, n, 0)),
            pl.BlockSpec((1, nq, GDN_CONV - 1, GDN_QKV), lambda b, n: (layer_state, b, 0, 0)),
            pl.BlockSpec((1, nq, GDN_HEADS, 128, 128), lambda b, n: (layer_state, b, 0, 0, 0)),
            pl.BlockSpec((1, GDN_CONV, GDN_QKV), lambda b, n: (layer, 0, 0)),
            pl.BlockSpec((1, 2, LANES), lambda b, n: (layer, 0, 0)),
            pl.BlockSpec((1, 1, LANES), lambda b, n: (layer, 0, 0)),
        ],
        out_specs=[
            pl.BlockSpec((nb, ns * rl, BRANCH), lambda b, n: (b, n, 0)),
            pl.BlockSpec((nq, GDN_HEADS, 128, 128), lambda b, n: (b, 0, 0, 0)),
        ],
        out_shape=[jax.ShapeDtypeStruct((nseq // ns, ns * length, BRANCH), BF16),
                   jax.ShapeDtypeStruct((nseq, GDN_HEADS, 128, 128), F32)],
        scratch_shapes=[pltpu.VMEM((nq, SUBLANES + rl, GDN_QKV), F32)],
        compiler_params=_cparams("parallel", "arbitrary"),
        name="gated_deltanet",
    )(h3, h3, ba3, conv_state, s0, cw, gp, nw)
    return branch.reshape(nseq * length, BRANCH), new_state


_RET_LOG_GAMMA = tuple(math.log(1.0 - 2.0 ** (-5.0 - h)) for h in range(RET_HEADS))


def _per_head(idx, vals):
    out = jnp.full(idx.shape, vals[3], F32)
    for h in (2, 1, 0):
        out = jnp.where(idx < (h + 1) * RET_DK, vals[h], out)
    return out


def _ret_kernel(hb_ref, cos_ref, sin_ref, s0_ref, o_ref, s_ref, *, ns, sl):
    c = ns * sl
    n = pl.program_id(1)

    @pl.when(n == 0)
    def _():
        s_ref[...] = s0_ref[0]

    hb = hb_ref[...].reshape(c, 2048)
    cos = cos_ref[...]
    sin = sin_ref[...]
    rq = (hb[:, 0:256] * cos + hb[:, 512:768] * sin) * (RET_DK ** -0.5)
    rk = hb[:, 256:512] * cos + hb[:, 768:1024] * sin
    v = hb[:, 1024:1536]
    g = hb[:, 1536:2048]

    r, t, seq = _seg_ids(c, sl)
    tf = t.astype(F32)
    lane = lax.broadcasted_iota(jnp.int32, (1, 256), 1)
    lgl = _per_head(lane, _RET_LOG_GAMMA)
    q_dec = rq * jnp.exp(lgl * (tf + 1.0))
    k_dec = rk * jnp.exp(lgl * (sl - 1.0 - tf))
    k_dec_t = k_dec.T

    rr = lax.broadcasted_iota(jnp.int32, (c, c), 0)
    cc = lax.broadcasted_iota(jnp.int32, (c, c), 1)
    sh = int(math.log2(sl))
    causal = jnp.logical_and((rr >> sh) == (cc >> sh), rr >= cc)
    rel = jnp.maximum(rr - cc, 0).astype(F32)
    cseq = lax.broadcasted_iota(jnp.int32, (1, c), 1) >> sh
    rowi = lax.broadcasted_iota(jnp.int32, (256, 1), 0)
    cd_rows = jnp.exp(_per_head(rowi, _RET_LOG_GAMMA) * float(sl))

    cross = []
    for si in range(ns):
        rows = slice(si * sl, (si + 1) * sl)
        st = s_ref[si]
        parts = []
        for h in range(RET_HEADS):
            mh = jnp.logical_and(lane >= h * RET_DK, lane < (h + 1) * RET_DK)
            parts.append(_bdot(jnp.where(mh, q_dec[rows], 0.0), st))
        cross.append(parts)
        kd = k_dec_t if ns == 1 else jnp.where(cseq == si, k_dec_t, 0.0)
        res = _bdot(kd, v)
        upd = jnp.zeros((256, RET_DV), F32)
        for h in range(RET_HEADS):
            rm = jnp.logical_and(rowi >= h * RET_DK, rowi < (h + 1) * RET_DK)
            upd = upd + jnp.where(rm, res[:, h * 128:(h + 1) * 128], 0.0)
        s_ref[si] = st * cd_rows + upd

    heads = range(RET_HEADS)
    mhs = [jnp.logical_and(lane >= h * RET_DK, lane < (h + 1) * RET_DK) for h in heads]
    scs = [_bdot_nt(jnp.where(mhs[h], rq, 0.0), rk)
           * jnp.where(causal, jnp.exp(_RET_LOG_GAMMA[h] * rel), 0.0) for h in heads]
    os_ = [_bdot(scs[h], v[:, h * 128:(h + 1) * 128]) for h in heads]
    outs = []
    for h in heads:
        oc = cross[0][h] if ns == 1 else jnp.concatenate([cross[si][h] for si in range(ns)], axis=0)
        o = os_[h] + oc
        mu = jnp.mean(o, -1, keepdims=True)
        oc2 = o - mu
        o = oc2 * lax.rsqrt(jnp.mean(oc2 * oc2, -1, keepdims=True) + LN_EPS)
        outs.append(_silu(g[:, h * 128:(h + 1) * 128]) * o)
    o_ref[...] = jnp.concatenate(outs, axis=1).astype(o_ref.dtype)


def _retention(h3, cos_tab, sin_tab, s0, layer_state, ns, sl):
    nseq, length, _ = h3.shape
    c = ns * sl
    grid = (nseq // ns, length // sl)
    kern = functools.partial(_ret_kernel, ns=ns, sl=sl)
    return pl.pallas_call(
        kern,
        grid=grid,
        in_specs=[
            pl.BlockSpec((ns, sl, 2048), lambda b, n: (b, n, COL_RET // 2048)),
            pl.BlockSpec((c, 256), lambda b, n: (n, 0)),
            pl.BlockSpec((c, 256), lambda b, n: (n, 0)),
            pl.BlockSpec((1, ns, 256, RET_DV), lambda b, n: (layer_state, b, 0, 0)),
        ],
        out_specs=[
            _branch_out_spec(ns, sl, grid[1]),
            pl.BlockSpec((ns, 256, RET_DV), lambda b, n: (b, 0, 0)),
        ],
        out_shape=[jax.ShapeDtypeStruct((nseq * length, BRANCH), BF16),
                   jax.ShapeDtypeStruct((nseq, 256, RET_DV), F32)],
        compiler_params=_cparams("parallel", "arbitrary"),
        name="retention",
    )(h3, cos_tab, sin_tab, s0)


def _pool_kernel(u_ref, st_ref, w_ref, sc_ref, o_ref, full_ref, *, ns, sl, pos0, carry):
    c = ns * sl
    n = pl.program_id(1)

    @pl.when(n == 0)
    def _():
        full_ref[:, 1:16, :] = st_ref[0]

    u3 = u_ref[...]
    full_ref[:, 16:16 + sl, :] = u3
    tpos = lax.broadcasted_iota(jnp.int32, (1, sl, 1), 1) + n * sl
    n_avail = (tpos + (pos0 + 1)).astype(F32)
    outs = []
    for gi, w in enumerate(POOL_WINDOWS):
        cols = slice(gi * 128, (gi + 1) * 128)
        ug = u3[:, :, cols]
        acc = ug
        for j in range(1, w):
            acc = acc + full_ref[:, 16 - j:16 - j + sl, cols]
        pooled = acc / jnp.minimum(n_avail, float(w)) - ug
        mixed = _bdot(pooled.reshape(c, 128), w_ref[0, gi])
        outs.append(mixed * sc_ref[0, :, cols])
    if carry:
        full_ref[:, 1:16, :] = full_ref[:, sl + 1:sl + 16, :]
    o_ref[...] = jnp.concatenate(outs, axis=1).astype(o_ref.dtype)


def _pool(h3, state, layer_state, w_pool, pool_scale, layer, ns, sl, pos0):
    nseq, length, _ = h3.shape
    grid = (nseq // ns, length // sl)
    kern = functools.partial(_pool_kernel, ns=ns, sl=sl, pos0=pos0, carry=grid[1] > 1)
    return pl.pallas_call(
        kern,
        grid=grid,
        in_specs=[
            pl.BlockSpec((ns, sl, BRANCH), lambda b, n: (b, n, COL_POOL // BRANCH)),
            pl.BlockSpec((1, ns, POOL_BUF, BRANCH), lambda b, n: (layer_state, b, 0, 0)),
            pl.BlockSpec((1, 4, 128, 128), lambda b, n: (layer, 0, 0, 0)),
            pl.BlockSpec((1, 1, BRANCH), lambda b, n: (layer, 0, 0)),
        ],
        out_specs=_branch_out_spec(ns, sl, grid[1]),
        out_shape=jax.ShapeDtypeStruct((nseq * length, BRANCH), BF16),
        scratch_shapes=[pltpu.VMEM((ns, 16 + sl, BRANCH), F32)],
        compiler_params=_cparams("parallel", "arbitrary"),
        name="multi_pool",
    )(h3, state, w_pool, pool_scale)


S5_SEQS = SUBLANES
S5_SLABS = 2 * S5_CH // LANES


def _s5_pitch(tl):
    return tl + 4


def _s5_kernel(u_ref, hr_ref, hi_ref, bd_ref, cd_ref, ar_ref, ai_ref, d_ref, glu_ref,
               o_ref, or_ref, oi_ref, bu_ref, h_ref, arb_ref, aib_ref, up_ref, *, tl):
    pitch = _s5_pitch(tl)
    n = pl.program_id(1)

    @pl.when(n == 0)
    def _():
        hr = hr_ref[0]
        hi = hi_ref[0]
        for kb in range(2):
            for j in range(8):
                src = slice(kb * 1024 + j * 128, kb * 1024 + (j + 1) * 128)
                h_ref[kb * 16 + j] = hr[:, src]
                h_ref[kb * 16 + 8 + j] = hi[:, src]
        for kb in range(2):
            for j in range(8):
                src = slice(kb * 1024 + j * 128, kb * 1024 + (j + 1) * 128)
                arb_ref[kb * 8 + j] = jnp.broadcast_to(ar_ref[0, :, src], (S5_SEQS, LANES))
                aib_ref[kb * 8 + j] = jnp.broadcast_to(ai_ref[0, :, src], (S5_SEQS, LANES))

        up_ref[...] = jnp.zeros_like(up_ref)

    for s in range(S5_SEQS):
        up_ref[s * pitch:s * pitch + tl, :] = u_ref[s]
    up = up_ref[...]
    upb = up.astype(BF16)
    for kb in range(2):
        bu = jnp.dot(upb[:, kb * 256:(kb + 1) * 256], bd_ref[0, kb],
                     preferred_element_type=F32)
        for j in range(16):
            bu_ref[kb * 16 + j] = bu[:, j * 128:(j + 1) * 128]

    def step(l, carry):
        new = []
        for kb in range(2):
            for j in range(8):
                re = carry[kb * 16 + j]
                im = carry[kb * 16 + 8 + j]
                ar = arb_ref[kb * 8 + j]
                ai = aib_ref[kb * 8 + j]
                idx = pl.ds(l, S5_SEQS, stride=pitch)
                b_re = bu_ref[kb * 16 + j, idx, :]
                b_im = bu_ref[kb * 16 + 8 + j, idx, :]
                n_re = ar * re - ai * im + b_re
                n_im = ar * im + ai * re + b_im
                bu_ref[kb * 16 + j, idx, :] = n_re
                bu_ref[kb * 16 + 8 + j, idx, :] = n_im
                new.append((kb * 16 + j, n_re))
                new.append((kb * 16 + 8 + j, n_im))
        new.sort(key=lambda p: p[0])
        return tuple(p[1] for p in new)

    h0 = tuple(h_ref[i] for i in range(S5_SLABS))
    hf = lax.fori_loop(0, tl, step, h0)
    for i in range(S5_SLABS):
        h_ref[i] = hf[i]

    @pl.when(n == pl.num_programs(1) - 1)
    def _():
        for kb in range(2):
            for j in range(8):
                dst = slice(kb * 1024 + j * 128, kb * 1024 + (j + 1) * 128)
                or_ref[:, dst] = hf[kb * 16 + j]
                oi_ref[:, dst] = hf[kb * 16 + 8 + j]

    ys = []
    for kb in range(2):
        hs = jnp.concatenate([bu_ref[kb * 16 + j].astype(BF16) for j in range(16)], axis=1)
        ys.append(jnp.dot(hs, cd_ref[0, kb], preferred_element_type=F32))
    y = jnp.concatenate(ys, axis=1) + d_ref[0] * up
    act = jax.nn.gelu(y)
    out = act * jax.nn.sigmoid(_bdot(act, glu_ref[0]))
    for s in range(S5_SEQS):
        o_ref[s] = out[s * pitch:s * pitch + tl, :]


def _s5(h3, h_re, h_im, layer_state, bd, cd, ar, ai, d_skip, w_glu, layer, tl):
    nseq, length, _ = h3.shape
    grid = (nseq // S5_SEQS, length // tl)
    kern = functools.partial(_s5_kernel, tl=tl)
    rows = S5_SEQS * _s5_pitch(tl)
    return pl.pallas_call(
        kern,
        grid=grid,
        in_specs=[
            pl.BlockSpec((S5_SEQS, tl, BRANCH), lambda b, n: (b, n, COL_S5 // BRANCH)),
            pl.BlockSpec((1, S5_SEQS, S5_CH), lambda b, n: (layer_state, b, 0)),
            pl.BlockSpec((1, S5_SEQS, S5_CH), lambda b, n: (layer_state, b, 0)),
            pl.BlockSpec((1, 2, 256, 2048), lambda b, n: (layer, 0, 0, 0)),
            pl.BlockSpec((1, 2, 2048, 256), lambda b, n: (layer, 0, 0, 0)),
            pl.BlockSpec((1, 1, S5_CH), lambda b, n: (layer, 0, 0)),
            pl.BlockSpec((1, 1, S5_CH), lambda b, n: (layer, 0, 0)),
            pl.BlockSpec((1, 1, BRANCH), lambda b, n: (layer, 0, 0)),
            pl.BlockSpec((1, BRANCH, BRANCH), lambda b, n: (layer, 0, 0)),
        ],
        out_specs=[
            pl.BlockSpec((S5_SEQS, tl, BRANCH), lambda b, n: (b, n, 0)),
            pl.BlockSpec((S5_SEQS, S5_CH), lambda b, n: (b, 0)),
            pl.BlockSpec((S5_SEQS, S5_CH), lambda b, n: (b, 0)),
        ],
        out_shape=[jax.ShapeDtypeStruct((nseq, length, BRANCH), F32),
                   jax.ShapeDtypeStruct((nseq, S5_CH), F32),
                   jax.ShapeDtypeStruct((nseq, S5_CH), F32)],
        scratch_shapes=[pltpu.VMEM((S5_SLABS, rows, LANES), F32),
                        pltpu.VMEM((S5_SLABS, S5_SEQS, LANES), F32),
                        pltpu.VMEM((S5_SLABS // 2, S5_SEQS, LANES), F32),
                        pltpu.VMEM((S5_SLABS // 2, S5_SEQS, LANES), F32),
                        pltpu.VMEM((rows, BRANCH), F32)],
        compiler_params=_cparams("parallel", "arbitrary"),
        name="s5_ssm",
    )(h3, h_re, h_im, bd, cd, ar, ai, d_skip, w_glu)


def _merge_kernel(x_ref, ba_ref, bb_ref, bc_ref, bd_ref, g_ref, wb_ref, wo_ref, wq_ref, ln_ref,
                  x1_ref, q_ref):
    mixed = None
    for i, br in enumerate((ba_ref, bb_ref, bc_ref, bd_ref)):
        proj = jnp.dot(br[...].astype(BF16), wb_ref[0, i], preferred_element_type=F32)
        term = jax.nn.sigmoid(g_ref[:, i * D_MODEL:(i + 1) * D_MODEL]) * proj
        mixed = term if mixed is None else mixed + term
    y = jnp.dot(mixed.astype(BF16), wo_ref[0], preferred_element_type=F32)
    x1 = _layer_norm(DN_ALPHA * x_ref[...] + y, ln_ref[0, 0:1, :], ln_ref[0, 1:2, :])
    x1_ref[...] = x1
    q_ref[...] = jnp.dot(x1.astype(BF16), wq_ref[0], preferred_element_type=F32)


def _merge(x, h, brs, wb, wo, wq, ln, layer, tm):
    m = x.shape[0]
    tok = lambda w: pl.BlockSpec((tm, w), lambda i: (i, 0))
    return pl.pallas_call(
        _merge_kernel,
        grid=(m // tm,),
        in_specs=[tok(D_MODEL), tok(BRANCH), tok(BRANCH), tok(BRANCH), tok(BRANCH),
                  pl.BlockSpec((tm, 4096), lambda i: (i, COL_GATE // 4096)),
                  pl.BlockSpec((1, N_BRANCH, BRANCH, D_MODEL), lambda i: (layer, 0, 0, 0)),
                  pl.BlockSpec((1, D_MODEL, D_MODEL), lambda i: (layer, 0, 0)),
                  pl.BlockSpec((1, D_MODEL, D_MODEL), lambda i: (layer, 0, 0)),
                  pl.BlockSpec((1, 2, D_MODEL), lambda i: (layer, 0, 0))],
        out_specs=[tok(D_MODEL), tok(D_MODEL)],
        out_shape=[jax.ShapeDtypeStruct((m, D_MODEL), F32)] * 2,
        compiler_params=_cparams("parallel"),
        name="branch_merge",
    )(x, *brs, h, wb, wo, wq, ln)


def _attn_kernel(q_ref, k_ref, v_ref, o_ref, *, heads_split):
    q = q_ref[0]
    heads = range(XA_HEADS)
    cols = [slice(h * XA_DH, (h + 1) * XA_DH) for h in heads]
    if heads_split:
        k2 = k_ref[0, 0].reshape(MEM_LEN * XA_HEADS, XA_DH)
        v2 = v_ref[0, 0].reshape(MEM_LEN * XA_HEADS, XA_DH)
        colh = lax.broadcasted_iota(jnp.int32, (1, MEM_LEN * XA_HEADS), 1) & (XA_HEADS - 1)
        ss = [jnp.where(colh == h, _bdot_nt(q[:, cols[h]], k2) * (XA_DH ** -0.5), -1e30)
              for h in heads]
        vs = [v2] * XA_HEADS
    else:
        ss = [_bdot_nt(q[:, cols[h]], k_ref[0, 0, :, cols[h]]) * (XA_DH ** -0.5) for h in heads]
        vs = [v_ref[0, 0, :, cols[h]] for h in heads]
    es = [jnp.exp(s - jnp.max(s, -1, keepdims=True)) for s in ss]
    ps = [e / jnp.sum(e, -1, keepdims=True) for e in es]
    o_ref[0] = jnp.concatenate([_bdot(ps[h], vs[h]) for h in heads], axis=1)


def _attention(q3, mem_k, mem_v, layer_mem, tq):
    nseq, length, _ = q3.shape
    heads_split = mem_k.ndim == 5
    if heads_split:
        mem = pl.BlockSpec((1, 1, MEM_LEN, XA_HEADS, XA_DH), lambda b, i: (layer_mem, b, 0, 0, 0))
    else:
        mem = pl.BlockSpec((1, 1, MEM_LEN, D_MODEL), lambda b, i: (layer_mem, b, 0, 0))
    return pl.pallas_call(
        functools.partial(_attn_kernel, heads_split=heads_split),
        grid=(nseq, length // tq),
        in_specs=[pl.BlockSpec((1, tq, D_MODEL), lambda b, i: (b, i, 0)), mem, mem],
        out_specs=pl.BlockSpec((1, tq, D_MODEL), lambda b, i: (b, i, 0)),
        out_shape=jax.ShapeDtypeStruct((nseq, length, D_MODEL), F32),
        compiler_params=_cparams("parallel", "arbitrary"),
        name="memory_attention",
    )(q3, mem_k, mem_v)


def _post_kernel(x1_ref, o_ref, wo_ref, ln_ref, wr_ref, br_ref, x2_ref, comb_ref):
    y = jnp.dot(o_ref[...].astype(BF16), wo_ref[0], preferred_element_type=F32)
    x2 = _layer_norm(DN_ALPHA * x1_ref[...] + y, ln_ref[0, 0:1, :], ln_ref[0, 1:2, :])
    x2_ref[...] = x2
    logits = jnp.dot(x2.astype(BF16), wr_ref[...], preferred_element_type=F32) + br_ref[...]
    e = jnp.exp(logits - jnp.max(logits, -1, keepdims=True))
    p = e / jnp.sum(e, -1, keepdims=True)
    lane = lax.broadcasted_iota(jnp.int32, p.shape, 1)
    lanef = lane.astype(F32)
    grp = lane >> 2
    best = None
    for gidx in range(N_EXPERTS // EXPERTS_PER_GROUP):
        gm = jnp.max(jnp.where(grp == gidx, p, -1.0), -1, keepdims=True)
        if best is None:
            best, gi = gm, jnp.zeros(gm.shape, jnp.int32)
        else:
            upd = gm > best
            gi = jnp.where(upd, gidx, gi)
            best = jnp.where(upd, gm, best)
    cand = jnp.where(grp == gi, p, -1.0)
    m1 = jnp.max(cand, -1, keepdims=True)
    e1 = jnp.min(jnp.where(cand == m1, lanef, 1e9), -1, keepdims=True)
    cand2 = jnp.where(lanef == e1, -2.0, cand)
    m2 = jnp.max(cand2, -1, keepdims=True)
    e2 = jnp.min(jnp.where(cand2 == m2, lanef, 1e9), -1, keepdims=True)
    tot = m1 + m2
    comb = jnp.where(lanef == e1, m1 / tot, 0.0) + jnp.where(lanef == e2, m2 / tot, 0.0)
    comb_ref[...] = comb + jnp.where(lane == GROUP_LANE, gi.astype(F32), 0.0)


def _post(x1, o, wo, ln, wr, br, layer, tm):
    m = x1.shape[0]
    tok = lambda w: pl.BlockSpec((tm, w), lambda i: (i, 0))
    return pl.pallas_call(
        _post_kernel,
        grid=(m // tm,),
        in_specs=[tok(D_MODEL), tok(D_MODEL),
                  pl.BlockSpec((1, D_MODEL, D_MODEL), lambda i: (layer, 0, 0)),
                  pl.BlockSpec((1, 2, D_MODEL), lambda i: (layer, 0, 0)),
                  pl.BlockSpec((D_MODEL, LANES), lambda i: (0, 0)),
                  pl.BlockSpec((1, LANES), lambda i: (0, 0))],
        out_specs=[tok(D_MODEL), tok(LANES)],
        out_shape=[jax.ShapeDtypeStruct((m, D_MODEL), F32), jax.ShapeDtypeStruct((m, LANES), F32)],
        compiler_params=_cparams("parallel"),
        name="attn_out_router",
    )(x1, o, wo, ln, wr, br)


def _moe_kernel(x_ref, comb_ref, w1_ref, w3_ref, w2_ref, ln_ref, o_ref, acc_ref, xb_ref):
    e = pl.program_id(1)

    @pl.when(e == 0)
    def _():
        acc_ref[...] = jnp.zeros_like(acc_ref)
        xb_ref[...] = x_ref[...].astype(BF16)

    comb = comb_ref[...]
    lane = lax.broadcasted_iota(jnp.int32, comb.shape, 1)
    ce = jnp.sum(jnp.where(lane == e, comb, 0.0), -1, keepdims=True)
    xb = xb_ref[...]
    h1 = jnp.dot(xb, w1_ref[0, 0].astype(BF16), preferred_element_type=F32)
    h3 = jnp.dot(xb, w3_ref[0, 0].astype(BF16), preferred_element_type=F32)
    hid = _silu(h1) * h3 * ce
    acc_ref[...] += jnp.dot(hid.astype(BF16), w2_ref[0, 0].astype(BF16),
                            preferred_element_type=F32)

    @pl.when(e == pl.num_programs(1) - 1)
    def _():
        o_ref[...] = _layer_norm(DN_ALPHA * x_ref[...] + acc_ref[...],
                                 ln_ref[0, 0:1, :], ln_ref[0, 1:2, :])


def _moe(x2, comb, w1, w3, w2, ln, layer, tm):
    m = x2.shape[0]
    w_up = pl.BlockSpec((1, 1, D_MODEL, D_EXPERT), lambda i, e: (layer, e, 0, 0))
    return pl.pallas_call(
        _moe_kernel,
        grid=(m // tm, N_EXPERTS),
        in_specs=[pl.BlockSpec((tm, D_MODEL), lambda i, e: (i, 0)),
                  pl.BlockSpec((tm, LANES), lambda i, e: (i, 0)),
                  w_up, w_up,
                  pl.BlockSpec((1, 1, D_EXPERT, D_MODEL), lambda i, e: (layer, e, 0, 0)),
                  pl.BlockSpec((1, 2, D_MODEL), lambda i, e: (layer, 0, 0))],
        out_specs=pl.BlockSpec((tm, D_MODEL), lambda i, e: (i, 0)),
        out_shape=jax.ShapeDtypeStruct((m, D_MODEL), F32),
        scratch_shapes=[pltpu.VMEM((tm, D_MODEL), F32), pltpu.VMEM((tm, D_MODEL), BF16)],
        compiler_params=_cparams("parallel", "arbitrary"),
        name="moe_experts",
    )(x2, comb, w1, w3, w2, ln)


N_GROUPS = N_EXPERTS // EXPERTS_PER_GROUP
GROUP_LANE = N_EXPERTS


def _bucket_cap(tm):
    return tm // N_GROUPS + tm // 16


def _split_bf16(x):
    hi = x.astype(BF16)
    return hi, (x - hi.astype(F32)).astype(BF16)


def _moe_sort_kernel(x_ref, comb_ref, xs_ref, gs_ref, dest_ref, cnt_ref, *, cap):
    tm = x_ref.shape[0]
    slots = N_GROUPS * cap
    comb = comb_ref[...]
    lanef = lax.broadcasted_iota(jnp.int32, comb.shape, 1).astype(F32)
    gid = comb[:, GROUP_LANE:GROUP_LANE + 1]
    onehot = jnp.where(lanef == gid, 1.0, 0.0)
    rr = lax.broadcasted_iota(jnp.int32, (tm, tm), 0)
    cc = lax.broadcasted_iota(jnp.int32, (tm, tm), 1)
    tril = jnp.where(rr >= cc, 1.0, 0.0).astype(BF16)
    cs = jnp.dot(tril, onehot.astype(BF16), preferred_element_type=F32)
    rank = jnp.sum(onehot * cs, -1, keepdims=True) - 1.0
    dest = gid * float(cap) + rank
    cnt_ref[0] = jnp.broadcast_to(cs[tm - 1:tm, :], (SUBLANES, LANES))
    dest_b = jnp.broadcast_to(dest, (tm, LANES))
    dest_ref[...] = dest_b
    dest_row = dest_b.T[0:1, :]
    srow = lax.broadcasted_iota(jnp.int32, (slots, 1), 0).astype(F32)
    perm = jnp.where(srow == dest_row, 1.0, 0.0).astype(BF16)
    xs = jnp.dot(perm, x_ref[...].astype(BF16), preferred_element_type=F32).astype(xs_ref.dtype)
    hi, lo = _split_bf16(comb)
    gs = (jnp.dot(perm, hi, preferred_element_type=F32)
          + jnp.dot(perm, lo, preferred_element_type=F32))
    for g in range(N_GROUPS):
        xs_ref[g, 0] = xs[g * cap:(g + 1) * cap, :]
        gs_ref[g, 0] = gs[g * cap:(g + 1) * cap, :]


def _moe_sort(x2, comb, tm):
    m = x2.shape[0]
    cap = _bucket_cap(tm)
    nt = m // tm
    return pl.pallas_call(
        functools.partial(_moe_sort_kernel, cap=cap),
        grid=(nt,),
        in_specs=[pl.BlockSpec((tm, D_MODEL), lambda i: (i, 0)),
                  pl.BlockSpec((tm, LANES), lambda i: (i, 0))],
        out_specs=[pl.BlockSpec((N_GROUPS, 1, cap, D_MODEL), lambda i: (0, i, 0, 0)),
                   pl.BlockSpec((N_GROUPS, 1, cap, LANES), lambda i: (0, i, 0, 0)),
                   pl.BlockSpec((tm, LANES), lambda i: (i, 0)),
                   pl.BlockSpec((1, SUBLANES, LANES), lambda i: (i, 0, 0))],
        out_shape=[jax.ShapeDtypeStruct((N_GROUPS, nt, cap, D_MODEL), BF16),
                   jax.ShapeDtypeStruct((N_GROUPS, nt, cap, LANES), F32),
                   jax.ShapeDtypeStruct((m, LANES), F32),
                   jax.ShapeDtypeStruct((nt, SUBLANES, LANES), F32)],
        compiler_params=_cparams("parallel"),
        name="moe_bucket_sort",
    )(x2, comb)


def _moe_bucket_kernel(xs_ref, gs_ref, w1_ref, w3_ref, w2_ref, o_ref, acc_ref):
    g = pl.program_id(0)
    j = pl.program_id(2)
    rows = acc_ref.shape[0]

    @pl.when(j == 0)
    def _():
        acc_ref[...] = jnp.zeros_like(acc_ref)

    gates = gs_ref[0].reshape(rows, LANES)
    lane = lax.broadcasted_iota(jnp.int32, gates.shape, 1)
    ce = jnp.sum(jnp.where(lane == g * EXPERTS_PER_GROUP + j, gates, 0.0), -1, keepdims=True)
    xb = xs_ref[0].reshape(rows, D_MODEL)
    h1 = jnp.dot(xb, w1_ref[0, 0].astype(BF16), preferred_element_type=F32)
    h3 = jnp.dot(xb, w3_ref[0, 0].astype(BF16), preferred_element_type=F32)
    hid = _silu(h1) * h3 * ce
    acc_ref[...] += jnp.dot(hid.astype(BF16), w2_ref[0, 0].astype(BF16),
                            preferred_element_type=F32)

    @pl.when(j == pl.num_programs(2) - 1)
    def _():
        o_ref[0] = acc_ref[...].reshape(o_ref.shape[1:])


def _moe_buckets(xs, gs, w1, w3, w2, layer, tpq):
    _, nt, cap, _ = xs.shape

    def bucket(width):
        return pl.BlockSpec((1, tpq, cap, width), lambda g, q, j: (g, q, 0, 0))

    def expert(rows, cols):
        return pl.BlockSpec((1, 1, rows, cols),
                            lambda g, q, j: (layer, g * EXPERTS_PER_GROUP + j, 0, 0))

    return pl.pallas_call(
        _moe_bucket_kernel,
        grid=(N_GROUPS, nt // tpq, EXPERTS_PER_GROUP),
        in_specs=[bucket(D_MODEL), bucket(LANES),
                  expert(D_MODEL, D_EXPERT), expert(D_MODEL, D_EXPERT), expert(D_EXPERT, D_MODEL)],
        out_specs=bucket(D_MODEL),
        out_shape=jax.ShapeDtypeStruct((N_GROUPS, nt, cap, D_MODEL), F32),
        scratch_shapes=[pltpu.VMEM((tpq * cap, D_MODEL), F32)],
        compiler_params=_cparams("parallel", "parallel", "arbitrary"),
        name="moe_bucket_experts",
    )(xs, gs, w1, w3, w2)


def _moe_unsort_kernel(x_ref, dest_ref, ys_ref, ln_ref, o_ref):
    _, _, cap, _ = ys_ref.shape
    slots = N_GROUPS * cap
    dest = dest_ref[:, 0:1]
    slot = lax.broadcasted_iota(jnp.int32, (1, slots), 1).astype(F32)
    perm_t = jnp.where(dest == slot, 1.0, 0.0).astype(BF16)
    hi, lo = _split_bf16(ys_ref[...].reshape(slots, D_MODEL))
    y = (jnp.dot(perm_t, hi, preferred_element_type=F32)
         + jnp.dot(perm_t, lo, preferred_element_type=F32))
    o_ref[...] = _layer_norm(DN_ALPHA * x_ref[...] + y, ln_ref[0, 0:1, :], ln_ref[0, 1:2, :])


def _moe_unsort(x2, dest, ys, ln, layer, tm):
    m = x2.shape[0]
    _, _, cap, _ = ys.shape
    return pl.pallas_call(
        _moe_unsort_kernel,
        grid=(m // tm,),
        in_specs=[pl.BlockSpec((tm, D_MODEL), lambda i: (i, 0)),
                  pl.BlockSpec((tm, LANES), lambda i: (i, 0)),
                  pl.BlockSpec((N_GROUPS, 1, cap, D_MODEL), lambda i: (0, i, 0, 0)),
                  pl.BlockSpec((1, 2, D_MODEL), lambda i: (layer, 0, 0))],
        out_specs=pl.BlockSpec((tm, D_MODEL), lambda i: (i, 0)),
        out_shape=jax.ShapeDtypeStruct((m, D_MODEL), F32),
        compiler_params=_cparams("parallel"),
        name="moe_unsort_norm",
    )(x2, dest, ys, ln)


def _moe_bucketed(x2, comb, w1, w3, w2, ln, layer, tm, tpq):
    xs, gs, dest, cnt = _moe_sort(x2, comb, tm)
    overflow = jnp.max(cnt[:, 0, :N_GROUPS]) > _bucket_cap(tm)

    def bucketed():
        return _moe_unsort(x2, dest, _moe_buckets(xs, gs, w1, w3, w2, layer, tpq), ln, layer, tm)

    def dense():
        return _moe(x2, comb, w1, w3, w2, ln, layer, tm)

    return lax.cond(overflow, dense, bucketed)


def _prep_w_in(w_in):
    o = np.cumsum((0, GDN_QKV, 512, 4, 4, 256, 256, 512, 512, 512, 512, 4096))
    seg = lambda i: w_in[:, :, o[i]:o[i + 1]]
    swap = np.arange(256) ^ 1
    main = jnp.concatenate([seg(0), seg(1), seg(4), seg(5), seg(4)[:, :, swap], seg(5)[:, :, swap],
                            seg(6), seg(7), seg(10), seg(8), seg(9)], axis=-1).astype(BF16)
    ba = jnp.concatenate([seg(2), seg(3), jnp.zeros((DEPTH, D_MODEL, LANES - 8), w_in.dtype)],
                         axis=-1).astype(BF16)
    return main, ba


def _rope_tables(pos):
    inv_freq = 1.0 / (ROPE_BASE ** jnp.linspace(0.0, 1.0, RET_DK // 2, dtype=F32))
    ang = pos.astype(F32)[:, None] * inv_freq
    cos = jnp.repeat(jnp.cos(ang), 2, axis=1)
    sin = jnp.repeat(jnp.sin(ang), 2, axis=1)
    sign = jnp.tile(jnp.array([-1.0, 1.0], F32), RET_DK // 2)
    return jnp.tile(cos, (1, RET_HEADS)), jnp.tile(sin * sign, (1, RET_HEADS))


def _prep_s5(a_re, a_im, log_dt, b_re, b_im, c_re, c_im):
    dt = jnp.exp(log_dt.astype(F32))[..., None]
    mag = jnp.exp(a_re * dt)
    ab_re, ab_im = mag * jnp.cos(a_im * dt), mag * jnp.sin(a_im * dt)
    den = a_re * a_re + a_im * a_im
    coef_re = ((ab_re - 1.0) * a_re + ab_im * a_im) / den
    coef_im = (ab_im * a_re - (ab_re - 1.0) * a_im) / den
    bb_re = coef_re[..., None] * b_re - coef_im[..., None] * b_im
    bb_im = coef_re[..., None] * b_im + coef_im[..., None] * b_re
    eye = jnp.eye(16, dtype=F32)

    def pack_b(bb):
        x = bb.reshape(DEPTH, 2, 16, S5_STATE, S5_GROUP)
        return jnp.einsum('dkgpc,gh->dkgchp', x, eye).reshape(DEPTH, 2, 256, 1024)

    def pack_c(cm):
        x = cm.reshape(DEPTH, 2, 16, S5_GROUP, S5_STATE)
        return jnp.einsum('dkgcp,gh->dkgphc', x, eye).reshape(DEPTH, 2, 1024, 256)

    bd = jnp.concatenate([pack_b(bb_re), pack_b(bb_im)], axis=-1).astype(BF16)
    cd = jnp.concatenate([pack_c(c_re.astype(F32)), -pack_c(c_im.astype(F32))], axis=-2).astype(BF16)
    ar = ab_re.reshape(DEPTH, 1, S5_CH)
    ai = ab_im.reshape(DEPTH, 1, S5_CH)
    return bd, cd, ar, ai


def kernel(x_prompt, x_sample, mem_prompt, state_gdn_conv, state_gdn, state_ret, state_pool,
           state_s5_re, state_s5_im, cache_mem_k, cache_mem_v, w_in, gdn_conv_w, gdn_a_log,
           gdn_dt_bias, gdn_norm_w, pool_w, pool_scale, s5_a_re, s5_a_im, s5_log_dt, s5_b_re,
           s5_b_im, s5_c_re, s5_c_im, s5_d, s5_w_glu, w_branch, w_out, xa_w_q, xa_w_k, xa_w_v,
           xa_w_o, ln_g, ln_b, w_router, b_router, moe_w1, moe_w3, moe_w2):
    bp, seq, _ = x_prompt.shape
    bs, dseq, _ = x_sample.shape
    past = 16384

    w_main, w_ba = _prep_w_in(w_in)
    gp = jnp.zeros((DEPTH, 2, LANES), F32)
    gp = gp.at[:, 0, 4:8].set(gdn_a_log.astype(F32)).at[:, 1, 4:8].set(gdn_dt_bias.astype(F32))
    nw = gdn_norm_w.astype(F32).reshape(DEPTH, 1, GDN_DK)
    cw = gdn_conv_w.astype(F32)
    pw = pool_w.astype(BF16)
    psc = pool_scale.astype(F32).reshape(DEPTH, 1, BRANCH)
    bd, cd, ar, ai = _prep_s5(s5_a_re.astype(F32), s5_a_im.astype(F32), s5_log_dt,
                              s5_b_re.astype(F32), s5_b_im.astype(F32), s5_c_re, s5_c_im)
    d_skip = s5_d.astype(F32).reshape(DEPTH, 1, BRANCH)
    glu = s5_w_glu.astype(BF16)
    wb = w_branch.astype(BF16)
    wo = w_out.astype(BF16)
    wq, wk, wv, wxo = (w.astype(BF16) for w in (xa_w_q, xa_w_k, xa_w_v, xa_w_o))
    ln = jnp.stack([ln_g.astype(F32), ln_b.astype(F32)], axis=2)
    ln1, ln2, ln3 = ln[:, 0], ln[:, 1], ln[:, 2]
    wr = jnp.concatenate([w_router, jnp.zeros((D_MODEL, LANES - N_EXPERTS), w_router.dtype)],
                         axis=1).astype(BF16)
    br = jnp.concatenate([b_router.astype(F32), jnp.full((LANES - N_EXPERTS,), -1e30, F32)])[None]

    cos_p, sin_p = _rope_tables(jnp.arange(seq))
    cos_s, sin_s = _rope_tables(past + jnp.arange(dseq))
    ns_s = 16
    cos_s, sin_s = jnp.tile(cos_s, (ns_s, 1)), jnp.tile(sin_s, (ns_s, 1))

    zeros = lambda *s: jnp.zeros((1,) + s, F32)
    z_conv, z_gdn = zeros(bp, GDN_CONV - 1, GDN_QKV), zeros(bp, GDN_HEADS, 128, 128)
    z_ret, z_pool, z_s5 = zeros(bp, 256, RET_DV), zeros(bp, POOL_BUF, BRANCH), zeros(bp, S5_CH)
    st_ret = state_ret.reshape(DEPTH, bs, 256, RET_DV)
    st_s5r = state_s5_re.reshape(DEPTH, bs, S5_CH)
    st_s5i = state_s5_im.reshape(DEPTH, bs, S5_CH)
    mem2d = mem_prompt.reshape(bp * MEM_LEN, D_MODEL)
    ck, cv = cache_mem_k, cache_mem_v

    def block(x2d, nseq, length, layer, mem_k, mem_v, layer_mem, states, layer_state, cfg):
        conv0, gdn0, ret0, pool0, s5r0, s5i0 = states
        tm = cfg["tm"]
        h, hba = _in_proj(x2d, w_main, w_ba, layer, tm, 2304)
        h3 = h.reshape(nseq, length, H_COLS)
        ba3 = hba.reshape(nseq, length, LANES)
        br_a, new_gdn = _gdn(h3, ba3, conv0, gdn0, layer_state, cw, gp, nw, layer,
                             cfg["gdn_nb"], cfg["ns"], cfg["gdn_rl"], cfg["gdn_sl"])
        br_b, new_ret = _retention(h3, cfg["cos"], cfg["sin"], ret0, layer_state,
                                   cfg["ns"], cfg["ret_sl"])
        br_c = _pool(h3, pool0, layer_state, pw, psc, layer, cfg["ns"], cfg["pool_sl"], cfg["pos0"])
        br_d, new_re, new_im = _s5(h3, s5r0, s5i0, layer_state, bd, cd, ar, ai, d_skip, glu,
                                   layer, cfg["s5_tl"])
        brs = [br_a, br_b, br_c, br_d.reshape(nseq * length, BRANCH)]
        x1, q = _merge(x2d, h, brs, wb, wo, wq, ln1, layer, cfg["tm_merge"])
        o = _attention(q.reshape(nseq, length, D_MODEL), mem_k, mem_v, layer_mem, cfg["tq"])
        x2, comb = _post(x1, o.reshape(nseq * length, D_MODEL), wxo, ln2, wr, br, layer, tm)
        if cfg["moe_tpq"]:
            x3 = _moe_bucketed(x2, comb, moe_w1, moe_w3, moe_w2, ln3, layer, tm, cfg["moe_tpq"])
        else:
            x3 = _moe(x2, comb, moe_w1, moe_w3, moe_w2, ln3, layer, tm)
        new_conv = h3[:, length - (GDN_CONV - 1):, COL_QKV:COL_QKV + GDN_QKV]
        pool_u = h3[:, :, COL_POOL:COL_POOL + BRANCH]
        return x3, (new_conv, new_gdn, new_ret.reshape(nseq, RET_HEADS, RET_DK, RET_DV), pool_u,
                    new_re.reshape(nseq, S5_GROUPS, S5_STATE), new_im.reshape(nseq, S5_GROUPS, S5_STATE))

    cfg_p = dict(tm=1024, tm_merge=512, ns=1, gdn_nb=1, gdn_rl=256, gdn_sl=64, ret_sl=128,
                 pool_sl=512, s5_tl=128, tq=512, pos0=0, cos=cos_p, sin=sin_p, moe_tpq=4)
    cfg_s = dict(tm=1024, tm_merge=512, ns=ns_s, gdn_nb=1, gdn_rl=dseq, gdn_sl=dseq, ret_sl=dseq,
                 pool_sl=dseq, s5_tl=dseq, tq=dseq, pos0=past, cos=cos_s, sin=sin_s, moe_tpq=0)

    yp = x_prompt.reshape(bp * seq, D_MODEL)
    ys = x_sample.reshape(bs * dseq, D_MODEL)
    p_out, s_out = [], []
    for l in range(DEPTH):
        mem_k = _matmul(mem2d, wk, l, 1024, 1024)
        mem_v = _matmul(mem2d, wv, l, 1024, 1024)
        mk4 = mem_k.reshape(1, bp, MEM_LEN, D_MODEL)
        mv4 = mem_v.reshape(1, bp, MEM_LEN, D_MODEL)
        yp, st = block(yp, bp, seq, l, mk4, mv4, 0,
                       (z_conv, z_gdn, z_ret, z_pool, z_s5, z_s5), 0, cfg_p)
        conv, gdn, ret, pool_u, s5r, s5i = st
        p_out.append((conv, gdn, ret, pool_u[:, seq - POOL_BUF:], s5r, s5i,
                      mem_k.reshape(bp, MEM_LEN, XA_HEADS, XA_DH),
                      mem_v.reshape(bp, MEM_LEN, XA_HEADS, XA_DH)))
        ys, st = block(ys, bs, dseq, l, ck, cv, l,
                       (state_gdn_conv, state_gdn, st_ret, state_pool, st_s5r, st_s5i), l, cfg_s)
        conv, gdn, ret, pool_u, s5r, s5i = st
        new_pool = jnp.concatenate([state_pool[l][:, dseq:], pool_u], axis=1)
        s_out.append((conv, gdn, ret, new_pool, s5r, s5i))
    p_st = [jnp.stack(t) for t in zip(*p_out)]
    s_st = [jnp.stack(t) for t in zip(*s_out)]
    return (yp.reshape(bp, seq, D_MODEL), ys.reshape(bs, dseq, D_MODEL), *p_st, *s_st)
```

```python
import functools
import math

import jax
import jax.numpy as jnp
import numpy as np
from jax import lax
from jax.experimental import pallas as pl
from jax.experimental.pallas import tpu as pltpu

F32 = jnp.float32
BF16 = jnp.bfloat16

D_MODEL = 1024
DEPTH = 4
BRANCH = 512
N_BRANCH = 4
GDN_HEADS = 4
GDN_DK = 128
GDN_QKV = 1536
GDN_CONV = 4
RET_HEADS = 4
RET_DK = 64
RET_DV = 128
ROPE_BASE = 10000.0
POOL_WINDOWS = (2, 4, 8, 16)
POOL_BUF = 15
S5_GROUPS = 32
S5_GROUP = 16
S5_STATE = 64
S5_CH = S5_GROUPS * S5_STATE
MEM_LEN = 256
XA_HEADS = 4
XA_DH = 256
N_EXPERTS = 16
EXPERTS_PER_GROUP = 4
D_EXPERT = 512
DN_ALPHA = (2.0 * DEPTH) ** 0.25
LN_EPS = 1e-5
RMS_EPS = 1e-6

LANES = 128
SUBLANES = 8
VMEM_LIMIT = 52 * 1024 * 1024

COL_QKV = 0
COL_Z = 1536
COL_RET = 2048
COL_GATE = 4096
COL_POOL = 8192
COL_S5 = 8704
H_COLS = 9216


def _cparams(*sem):
    return pltpu.CompilerParams(dimension_semantics=sem, vmem_limit_bytes=VMEM_LIMIT)


def _bdot(a, b):
    return jnp.dot(a.astype(BF16), b.astype(BF16), preferred_element_type=F32)


def _bdot_nt(a, b):
    return lax.dot_general(a.astype(BF16), b.astype(BF16), (((1,), (1,)), ((), ())),
                           preferred_element_type=F32)


def _silu(x):
    return x * jax.nn.sigmoid(x)


def _layer_norm(x, g, b):
    mu = jnp.mean(x, -1, keepdims=True)
    xc = x - mu
    var = jnp.mean(xc * xc, -1, keepdims=True)
    return xc * lax.rsqrt(var + LN_EPS) * g + b


def _mm_kernel(x_ref, w_ref, o_ref, xb_ref):
    @pl.when(pl.program_id(1) == 0)
    def _():
        xb_ref[...] = x_ref[...].astype(BF16)

    o_ref[...] = jnp.dot(xb_ref[...], w_ref[0], preferred_element_type=F32)


def _matmul(x, w, layer, tm, tn):
    m, k = x.shape
    n = w.shape[-1]
    return pl.pallas_call(
        _mm_kernel,
        grid=(m // tm, n // tn),
        in_specs=[pl.BlockSpec((tm, k), lambda i, j: (i, 0)),
                  pl.BlockSpec((1, k, tn), lambda i, j: (layer, 0, j))],
        out_specs=pl.BlockSpec((tm, tn), lambda i, j: (i, j)),
        out_shape=jax.ShapeDtypeStruct((m, n), F32),
        scratch_shapes=[pltpu.VMEM((tm, k), BF16)],
        compiler_params=_cparams("parallel", "arbitrary"),
        name="token_matmul",
    )(x, w)


def _inproj_kernel(x_ref, w_ref, wba_ref, o_ref, ba_ref, xb_ref):
    @pl.when(pl.program_id(1) == 0)
    def _():
        xb_ref[...] = x_ref[...].astype(BF16)
        ba_ref[...] = jnp.dot(xb_ref[...], wba_ref[0], preferred_element_type=F32)

    o_ref[...] = jnp.dot(xb_ref[...], w_ref[0], preferred_element_type=F32)


def _in_proj(x, w, w_ba, layer, tm, tn):
    m, k = x.shape
    n = w.shape[-1]
    return pl.pallas_call(
        _inproj_kernel,
        grid=(m // tm, n // tn),
        in_specs=[pl.BlockSpec((tm, k), lambda i, j: (i, 0)),
                  pl.BlockSpec((1, k, tn), lambda i, j: (layer, 0, j)),
                  pl.BlockSpec((1, k, LANES), lambda i, j: (layer, 0, 0))],
        out_specs=[pl.BlockSpec((tm, tn), lambda i, j: (i, j)),
                   pl.BlockSpec((tm, LANES), lambda i, j: (i, 0))],
        out_shape=[jax.ShapeDtypeStruct((m, n), F32), jax.ShapeDtypeStruct((m, LANES), F32)],
        scratch_shapes=[pltpu.VMEM((tm, k), BF16)],
        compiler_params=_cparams("parallel", "arbitrary"),
        name="in_proj",
    )(x, w, w_ba)


def _branch_out_spec(ns, sl, n_chunks):
    return pl.BlockSpec((ns * sl, BRANCH), lambda b, n: (b * n_chunks + n, 0))


def _seg_ids(c, sl):
    r = lax.broadcasted_iota(jnp.int32, (c, 1), 0)
    sh = int(math.log2(sl))
    return r, r & (sl - 1), r >> sh


def _gdn_kernel(qkv_ref, z_ref, ba_ref, cs_ref, s0_ref, cw_ref, gp_ref, nw_ref,
                o_ref, s_ref, full_ref, *, nb, ns, rl, sl):
    n = pl.program_id(1)

    @pl.when(n == 0)
    def _():
        full_ref[:, 5:8, :] = cs_ref[0]
        s_ref[...] = s0_ref[0]

    for bi in range(nb):
        _gdn_block(qkv_ref, z_ref, ba_ref, cw_ref, gp_ref, nw_ref, o_ref, s_ref, full_ref,
                   bi=bi, ns=ns, rl=rl, sl=sl)


def _gdn_block(qkv_ref, z_ref, ba_ref, cw_ref, gp_ref, nw_ref, o_ref, s_ref, full_ref,
               *, bi, ns, rl, sl):
    c = ns * rl
    nsps = rl // sl
    sq = slice(bi * ns, (bi + 1) * ns)

    u3 = qkv_ref[sq]
    full_ref[sq, 8:8 + rl, :] = u3
    cw = cw_ref[0]
    acc = (cw[3:4] * u3 + cw[2:3] * full_ref[sq, 7:7 + rl, :] + cw[1:2] * full_ref[sq, 6:6 + rl, :]
           + cw[0:1] * full_ref[sq, 5:5 + rl, :])
    full_ref[sq, 5:8, :] = full_ref[sq, 5 + rl:8 + rl, :]
    qkv = _silu(acc).reshape(c, GDN_QKV)

    ba = ba_ref[sq].reshape(c, LANES)
    beta_t = jax.nn.sigmoid(ba)
    xs = ba + gp_ref[0, 1:2, :]
    softplus = jnp.maximum(xs, 0.0) + jnp.log1p(jnp.exp(-jnp.abs(xs)))
    g_t = -jnp.exp(gp_ref[0, 0:1, :]) * softplus

    r, t, seq = _seg_ids(c, sl)
    gc = g_t
    s = 1
    while s < sl:
        gc = gc + jnp.where(t >= s, pltpu.roll(gc, s, 0), 0.0)
        s *= 2
    tot = gc
    s = 1
    while s < sl:
        tot = jnp.where(t + s < sl, pltpu.roll(tot, c - s, 0), tot)
        s *= 2
    rc = tot - gc
    if c < LANES:
        gc_pad = jnp.concatenate([gc, jnp.zeros((LANES - c, LANES), F32)], axis=0)
    else:
        gc_pad = gc
    gc_t = gc_pad.T

    rr = lax.broadcasted_iota(jnp.int32, (c, c), 0)
    cc = lax.broadcasted_iota(jnp.int32, (c, c), 1)
    sh = int(math.log2(sl))
    same = (rr >> sh) == (cc >> sh)
    causal = jnp.logical_and(same, rr >= cc)
    strict = jnp.logical_and(same, rr > cc)
    eye = (rr == cc).astype(F32)

    z = z_ref[sq].reshape(c, BRANCH)
    nw = nw_ref[0]
    heads = range(GDN_HEADS)

    qs, ks, kbs, q_decs, k_dec_ts, decays, rhss = [], [], [], [], [], [], []
    for h in heads:
        q = qkv[:, h * 128:(h + 1) * 128]
        k = qkv[:, 512 + h * 128:512 + (h + 1) * 128]
        v = qkv[:, 1024 + h * 128:1024 + (h + 1) * 128]
        q = q * lax.rsqrt(jnp.sum(q * q, -1, keepdims=True) + RMS_EPS) * (GDN_DK ** -0.5)
        k = k * lax.rsqrt(jnp.sum(k * k, -1, keepdims=True) + RMS_EPS)
        beta = beta_t[:, h:h + 1]
        gcc = gc[:, 4 + h:5 + h]
        gcr = gc_t[4 + h:5 + h, 0:c]
        eg = jnp.exp(gcc)
        kb = k * beta
        qs.append(q)
        ks.append(k)
        kbs.append(kb)
        q_decs.append(q * eg)
        k_dec_ts.append((k * jnp.exp(rc[:, 4 + h:5 + h])).T)
        decays.append(jnp.where(causal, jnp.exp(jnp.where(causal, gcc - gcr, 0.0)), 0.0))
        rhss.append(jnp.concatenate([v * beta, kb * eg], axis=1))
    npows = [-jnp.where(strict, _bdot_nt(kbs[h], ks[h]) * decays[h], 0.0) for h in heads]
    sms = [eye + npows[h] for h in heads]
    if sl > 2:
        npows = [_bdot(npows[h], npows[h]) for h in heads]
        m = 2
        while 2 * m < sl:
            prods = [_bdot(npows[h], jnp.concatenate([npows[h], sms[h]], axis=1)) for h in heads]
            npows = [p[:, :c] for p in prods]
            sms = [sms[h] + prods[h][:, c:] for h in heads]
            m *= 2
        sms = [sms[h] + _bdot(npows[h], sms[h]) for h in heads]
    sols = [_bdot(sms[h], rhss[h]) for h in heads]
    attns = [jnp.where(causal, _bdot_nt(qs[h], ks[h]) * decays[h], 0.0) for h in heads]

    v_parts = [[] for _ in heads]
    o_parts = [[] for _ in heads]
    for si in range(ns):
        sts = [s_ref[bi * ns + si, h] for h in heads]
        for j in range(nsps):
            sg = si * nsps + j
            rows = slice(sg * sl, (sg + 1) * sl)
            for h in heads:
                v_j = sols[h][rows, :128] - _bdot(sols[h][rows, 128:], sts[h])
                o_parts[h].append(_bdot(q_decs[h][rows], sts[h]))
                v_parts[h].append(v_j)
                pieces = [v_j]
                if sg > 0:
                    pieces.insert(0, jnp.zeros((sg * sl, 128), F32))
                if (sg + 1) * sl < c:
                    pieces.append(jnp.zeros((c - (sg + 1) * sl, 128), F32))
                vz = v_j if len(pieces) == 1 else jnp.concatenate(pieces, axis=0)
                last = jnp.exp(tot[sg * sl:sg * sl + 1, 4 + h:5 + h])
                sts[h] = sts[h] * last + _bdot(k_dec_ts[h], vz)
        for h in heads:
            s_ref[bi * ns + si, h] = sts[h]

    outs = []
    for h in heads:
        v_new = v_parts[h][0] if len(v_parts[h]) == 1 else jnp.concatenate(v_parts[h], axis=0)
        o_cross = o_parts[h][0] if len(o_parts[h]) == 1 else jnp.concatenate(o_parts[h], axis=0)
        o = o_cross + _bdot(attns[h], v_new)
        o = o * lax.rsqrt(jnp.mean(o * o, -1, keepdims=True) + RMS_EPS) * nw
        outs.append(o * _silu(z[:, h * 128:(h + 1) * 128]))
    o_ref[bi] = jnp.concatenate(outs, axis=1).astype(o_ref.dtype)


def _gdn(h3, ba3, conv_state, s0, layer_state, cw, gp, nw, layer, nb, ns, rl, sl):
    nseq, length, _ = h3.shape
    assert ns == 1 or rl == length
    nq = nb * ns
    grid = (nseq // nq, length // rl)
    kern = functools.partial(_gdn_kernel, nb=nb, ns=ns, rl=rl, sl=sl)
    branch, new_state = pl.pallas_call(
        kern,
        grid=grid,
        in_specs=[
            pl.BlockSpec((nq, rl, GDN_QKV), lambda b, n: (b, n, COL_QKV // GDN_QKV)),
            pl.BlockSpec((nq, rl, BRANCH), lambda b, n: (b, n, COL_Z // BRANCH)),
            pl.BlockSpec((nq, rl, LANES), lambda b, n: (b, n, 0)),
            pl.BlockSpec((1, nq, GDN_CONV - 1, GDN_QKV), lambda b, n: (layer_state, b, 0, 0)),
            pl.BlockSpec((1, nq, GDN_HEADS, 128, 128), lambda b, n: (layer_state, b, 0, 0, 0)),
            pl.BlockSpec((1, GDN_CONV, GDN_QKV), lambda b, n: (layer, 0, 0)),
            pl.BlockSpec((1, 2, LANES), lambda b, n: (layer, 0, 0)),
            pl.BlockSpec((1, 1, LANES), lambda b, n: (layer, 0, 0)),
        ],
        out_specs=[
            pl.BlockSpec((nb, ns * rl, BRANCH), lambda b, n: (b, n, 0)),
            pl.BlockSpec((nq, GDN_HEADS, 128, 128), lambda b, n: (b, 0, 0, 0)),
        ],
        out_shape=[jax.ShapeDtypeStruct((nseq // ns, ns * length, BRANCH), BF16),
                   jax.ShapeDtypeStruct((nseq, GDN_HEADS, 128, 128), F32)],
        scratch_shapes=[pltpu.VMEM((nq, SUBLANES + rl, GDN_QKV), F32)],
        compiler_params=_cparams("parallel", "arbitrary"),
        name="gated_deltanet",
    )(h3, h3, ba3, conv_state, s0, cw, gp, nw)
    return branch.reshape(nseq * length, BRANCH), new_state


_RET_LOG_GAMMA = tuple(math.log(1.0 - 2.0 ** (-5.0 - h)) for h in range(RET_HEADS))


def _per_head(idx, vals):
    out = jnp.full(idx.shape, vals[3], F32)
    for h in (2, 1, 0):
        out = jnp.where(idx < (h + 1) * RET_DK, vals[h], out)
    return out


def _ret_kernel(hb_ref, cos_ref, sin_ref, s0_ref, o_ref, s_ref, *, ns, sl):
    c = ns * sl
    n = pl.program_id(1)

    @pl.when(n == 0)
    def _():
        s_ref[...] = s0_ref[0]

    hb = hb_ref[...].reshape(c, 2048)
    cos = cos_ref[...]
    sin = sin_ref[...]
    rq = (hb[:, 0:256] * cos + hb[:, 512:768] * sin) * (RET_DK ** -0.5)
    rk = hb[:, 256:512] * cos + hb[:, 768:1024] * sin
    v = hb[:, 1024:1536]
    g = hb[:, 1536:2048]

    r, t, seq = _seg_ids(c, sl)
    tf = t.astype(F32)
    lane = lax.broadcasted_iota(jnp.int32, (1, 256), 1)
    lgl = _per_head(lane, _RET_LOG_GAMMA)
    q_dec = rq * jnp.exp(lgl * (tf + 1.0))
    k_dec = rk * jnp.exp(lgl * (sl - 1.0 - tf))
    k_dec_t = k_dec.T

    rr = lax.broadcasted_iota(jnp.int32, (c, c), 0)
    cc = lax.broadcasted_iota(jnp.int32, (c, c), 1)
    sh = int(math.log2(sl))
    causal = jnp.logical_and((rr >> sh) == (cc >> sh), rr >= cc)
    rel = jnp.maximum(rr - cc, 0).astype(F32)
    cseq = lax.broadcasted_iota(jnp.int32, (1, c), 1) >> sh
    rowi = lax.broadcasted_iota(jnp.int32, (256, 1), 0)
    cd_rows = jnp.exp(_per_head(rowi, _RET_LOG_GAMMA) * float(sl))

    cross = []
    for si in range(ns):
        rows = slice(si * sl, (si + 1) * sl)
        st = s_ref[si]
        parts = []
        for h in range(RET_HEADS):
            mh = jnp.logical_and(lane >= h * RET_DK, lane < (h + 1) * RET_DK)
            parts.append(_bdot(jnp.where(mh, q_dec[rows], 0.0), st))
        cross.append(parts)
        kd = k_dec_t if ns == 1 else jnp.where(cseq == si, k_dec_t, 0.0)
        res = _bdot(kd, v)
        upd = jnp.zeros((256, RET_DV), F32)
        for h in range(RET_HEADS):
            rm = jnp.logical_and(rowi >= h * RET_DK, rowi < (h + 1) * RET_DK)
            upd = upd + jnp.where(rm, res[:, h * 128:(h + 1) * 128], 0.0)
        s_ref[si] = st * cd_rows + upd

    heads = range(RET_HEADS)
    mhs = [jnp.logical_and(lane >= h * RET_DK, lane < (h + 1) * RET_DK) for h in heads]
    scs = [_bdot_nt(jnp.where(mhs[h], rq, 0.0), rk)
           * jnp.where(causal, jnp.exp(_RET_LOG_GAMMA[h] * rel), 0.0) for h in heads]
    os_ = [_bdot(scs[h], v[:, h * 128:(h + 1) * 128]) for h in heads]
    outs = []
    for h in heads:
        oc = cross[0][h] if ns == 1 else jnp.concatenate([cross[si][h] for si in range(ns)], axis=0)
        o = os_[h] + oc
        mu = jnp.mean(o, -1, keepdims=True)
        oc2 = o - mu
        o = oc2 * lax.rsqrt(jnp.mean(oc2 * oc2, -1, keepdims=True) + LN_EPS)
        outs.append(_silu(g[:, h * 128:(h + 1) * 128]) * o)
    o_ref[...] = jnp.concatenate(outs, axis=1).astype(o_ref.dtype)


def _retention(h3, cos_tab, sin_tab, s0, layer_state, ns, sl):
    nseq, length, _ = h3.shape
    c = ns * sl
    grid = (nseq // ns, length // sl)
    kern = functools.partial(_ret_kernel, ns=ns, sl=sl)
    return pl.pallas_call(
        kern,
        grid=grid,
        in_specs=[
            pl.BlockSpec((ns, sl, 2048), lambda b, n: (b, n, COL_RET // 2048)),
            pl.BlockSpec((c, 256), lambda b, n: (n, 0)),
            pl.BlockSpec((c, 256), lambda b, n: (n, 0)),
            pl.BlockSpec((1, ns, 256, RET_DV), lambda b, n: (layer_state, b, 0, 0)),
        ],
        out_specs=[
            _branch_out_spec(ns, sl, grid[1]),
            pl.BlockSpec((ns, 256, RET_DV), lambda b, n: (b, 0, 0)),
        ],
        out_shape=[jax.ShapeDtypeStruct((nseq * length, BRANCH), BF16),
                   jax.ShapeDtypeStruct((nseq, 256, RET_DV), F32)],
        compiler_params=_cparams("parallel", "arbitrary"),
        name="retention",
    )(h3, cos_tab, sin_tab, s0)


def _pool_kernel(u_ref, st_ref, w_ref, sc_ref, o_ref, full_ref, *, ns, sl, pos0, carry):
    c = ns * sl
    n = pl.program_id(1)

    @pl.when(n == 0)
    def _():
        full_ref[:, 1:16, :] = st_ref[0]

    u3 = u_ref[...]
    full_ref[:, 16:16 + sl, :] = u3
    tpos = lax.broadcasted_iota(jnp.int32, (1, sl, 1), 1) + n * sl
    n_avail = (tpos + (pos0 + 1)).astype(F32)
    outs = []
    for gi, w in enumerate(POOL_WINDOWS):
        cols = slice(gi * 128, (gi + 1) * 128)
        ug = u3[:, :, cols]
        acc = ug
        for j in range(1, w):
            acc = acc + full_ref[:, 16 - j:16 - j + sl, cols]
        pooled = acc / jnp.minimum(n_avail, float(w)) - ug
        mixed = _bdot(pooled.reshape(c, 128), w_ref[0, gi])
        outs.append(mixed * sc_ref[0, :, cols])
    if carry:
        full_ref[:, 1:16, :] = full_ref[:, sl + 1:sl + 16, :]
    o_ref[...] = jnp.concatenate(outs, axis=1).astype(o_ref.dtype)


def _pool(h3, state, layer_state, w_pool, pool_scale, layer, ns, sl, pos0):
    nseq, length, _ = h3.shape
    grid = (nseq // ns, length // sl)
    kern = functools.partial(_pool_kernel, ns=ns, sl=sl, pos0=pos0, carry=grid[1] > 1)
    return pl.pallas_call(
        kern,
        grid=grid,
        in_specs=[
            pl.BlockSpec((ns, sl, BRANCH), lambda b, n: (b, n, COL_POOL // BRANCH)),
            pl.BlockSpec((1, ns, POOL_BUF, BRANCH), lambda b, n: (layer_state, b, 0, 0)),
            pl.BlockSpec((1, 4, 128, 128), lambda b, n: (layer, 0, 0, 0)),
            pl.BlockSpec((1, 1, BRANCH), lambda b, n: (layer, 0, 0)),
        ],
        out_specs=_branch_out_spec(ns, sl, grid[1]),
        out_shape=jax.ShapeDtypeStruct((nseq * length, BRANCH), BF16),
        scratch_shapes=[pltpu.VMEM((ns, 16 + sl, BRANCH), F32)],
        compiler_params=_cparams("parallel", "arbitrary"),
        name="multi_pool",
    )(h3, state, w_pool, pool_scale)


S5_SEQS = SUBLANES
S5_SLABS = 2 * S5_CH // LANES


def _s5_pitch(tl):
    return tl + 4


def _s5_kernel(u_ref, hr_ref, hi_ref, bd_ref, cd_ref, ar_ref, ai_ref, d_ref, glu_ref,
               o_ref, or_ref, oi_ref, bu_ref, h_ref, arb_ref, aib_ref, up_ref, *, tl):
    pitch = _s5_pitch(tl)
    n = pl.program_id(1)

    @pl.when(n == 0)
    def _():
        hr = hr_ref[0]
        hi = hi_ref[0]
        for kb in range(2):
            for j in range(8):
                src = slice(kb * 1024 + j * 128, kb * 1024 + (j + 1) * 128)
                h_ref[kb * 16 + j] = hr[:, src]
                h_ref[kb * 16 + 8 + j] = hi[:, src]
        for kb in range(2):
            for j in range(8):
                src = slice(kb * 1024 + j * 128, kb * 1024 + (j + 1) * 128)
                arb_ref[kb * 8 + j] = jnp.broadcast_to(ar_ref[0, :, src], (S5_SEQS, LANES))
                aib_ref[kb * 8 + j] = jnp.broadcast_to(ai_ref[0, :, src], (S5_SEQS, LANES))

        up_ref[...] = jnp.zeros_like(up_ref)

    for s in range(S5_SEQS):
        up_ref[s * pitch:s * pitch + tl, :] = u_ref[s]
    up = up_ref[...]
    upb = up.astype(BF16)
    for kb in range(2):
        bu = jnp.dot(upb[:, kb * 256:(kb + 1) * 256], bd_ref[0, kb],
                     preferred_element_type=F32)
        for j in range(16):
            bu_ref[kb * 16 + j] = bu[:, j * 128:(j + 1) * 128]

    def step(l, carry):
        new = []
        for kb in range(2):
            for j in range(8):
                re = carry[kb * 16 + j]
                im = carry[kb * 16 + 8 + j]
                ar = arb_ref[kb * 8 + j]
                ai = aib_ref[kb * 8 + j]
                idx = pl.ds(l, S5_SEQS, stride=pitch)
                b_re = bu_ref[kb * 16 + j, idx, :]
                b_im = bu_ref[kb * 16 + 8 + j, idx, :]
                n_re = ar * re - ai * im + b_re
                n_im = ar * im + ai * re + b_im
                bu_ref[kb * 16 + j, idx, :] = n_re
                bu_ref[kb * 16 + 8 + j, idx, :] = n_im
                new.append((kb * 16 + j, n_re))
                new.append((kb * 16 + 8 + j, n_im))
        new.sort(key=lambda p: p[0])
        return tuple(p[1] for p in new)

    h0 = tuple(h_ref[i] for i in range(S5_SLABS))
    hf = lax.fori_loop(0, tl, step, h0, unroll=2)
    for i in range(S5_SLABS):
        h_ref[i] = hf[i]

    @pl.when(n == pl.num_programs(1) - 1)
    def _():
        for kb in range(2):
            for j in range(8):
                dst = slice(kb * 1024 + j * 128, kb * 1024 + (j + 1) * 128)
                or_ref[:, dst] = hf[kb * 16 + j]
                oi_ref[:, dst] = hf[kb * 16 + 8 + j]

    ys = []
    for kb in range(2):
        hs = jnp.concatenate([bu_ref[kb * 16 + j].astype(BF16) for j in range(16)], axis=1)
        ys.append(jnp.dot(hs, cd_ref[0, kb], preferred_element_type=F32))
    y = jnp.concatenate(ys, axis=1) + d_ref[0] * up
    act = jax.nn.gelu(y)
    out = act * jax.nn.sigmoid(_bdot(act, glu_ref[0]))
    for s in range(S5_SEQS):
        o_ref[s] = out[s * pitch:s * pitch + tl, :]


def _s5(h3, h_re, h_im, layer_state, bd, cd, ar, ai, d_skip, w_glu, layer, tl):
    nseq, length, _ = h3.shape
    grid = (nseq // S5_SEQS, length // tl)
    kern = functools.partial(_s5_kernel, tl=tl)
    rows = S5_SEQS * _s5_pitch(tl)
    return pl.pallas_call(
        kern,
        grid=grid,
        in_specs=[
            pl.BlockSpec((S5_SEQS, tl, BRANCH), lambda b, n: (b, n, COL_S5 // BRANCH)),
            pl.BlockSpec((1, S5_SEQS, S5_CH), lambda b, n: (layer_state, b, 0)),
            pl.BlockSpec((1, S5_SEQS, S5_CH), lambda b, n: (layer_state, b, 0)),
            pl.BlockSpec((1, 2, 256, 2048), lambda b, n: (layer, 0, 0, 0)),
            pl.BlockSpec((1, 2, 2048, 256), lambda b, n: (layer, 0, 0, 0)),
            pl.BlockSpec((1, 1, S5_CH), lambda b, n: (layer, 0, 0)),
            pl.BlockSpec((1, 1, S5_CH), lambda b, n: (layer, 0, 0)),
            pl.BlockSpec((1, 1, BRANCH), lambda b, n: (layer, 0, 0)),
            pl.BlockSpec((1, BRANCH, BRANCH), lambda b, n: (layer, 0, 0)),
        ],
        out_specs=[
            pl.BlockSpec((S5_SEQS, tl, BRANCH), lambda b, n: (b, n, 0)),
            pl.BlockSpec((S5_SEQS, S5_CH), lambda b, n: (b, 0)),
            pl.BlockSpec((S5_SEQS, S5_CH), lambda b, n: (b, 0)),
        ],
        out_shape=[jax.ShapeDtypeStruct((nseq, length, BRANCH), F32),
                   jax.ShapeDtypeStruct((nseq, S5_CH), F32),
                   jax.ShapeDtypeStruct((nseq, S5_CH), F32)],
        scratch_shapes=[pltpu.VMEM((S5_SLABS, rows, LANES), F32),
                        pltpu.VMEM((S5_SLABS, S5_SEQS, LANES), F32),
                        pltpu.VMEM((S5_SLABS // 2, S5_SEQS, LANES), F32),
                        pltpu.VMEM((S5_SLABS // 2, S5_SEQS, LANES), F32),
                        pltpu.VMEM((rows, BRANCH), F32)],
        compiler_params=_cparams("parallel", "arbitrary"),
        name="s5_ssm",
    )(h3, h_re, h_im, bd, cd, ar, ai, d_skip, w_glu)


def _merge_kernel(x_ref, ba_ref, bb_ref, bc_ref, bd_ref, g_ref, wb_ref, wo_ref, wq_ref, ln_ref,
                  x1_ref, q_ref):
    mixed = None
    for i, br in enumerate((ba_ref, bb_ref, bc_ref, bd_ref)):
        proj = jnp.dot(br[...].astype(BF16), wb_ref[0, i], preferred_element_type=F32)
        term = jax.nn.sigmoid(g_ref[:, i * D_MODEL:(i + 1) * D_MODEL]) * proj
        mixed = term if mixed is None else mixed + term
    y = jnp.dot(mixed.astype(BF16), wo_ref[0], preferred_element_type=F32)
    x1 = _layer_norm(DN_ALPHA * x_ref[...] + y, ln_ref[0, 0:1, :], ln_ref[0, 1:2, :])
    x1_ref[...] = x1
    q_ref[...] = jnp.dot(x1.astype(BF16), wq_ref[0], preferred_element_type=F32)


def _merge(x, h, brs, wb, wo, wq, ln, layer, tm):
    m = x.shape[0]
    tok = lambda w: pl.BlockSpec((tm, w), lambda i: (i, 0))
    return pl.pallas_call(
        _merge_kernel,
        grid=(m // tm,),
        in_specs=[tok(D_MODEL), tok(BRANCH), tok(BRANCH), tok(BRANCH), tok(BRANCH),
                  pl.BlockSpec((tm, 4096), lambda i: (i, COL_GATE // 4096)),
                  pl.BlockSpec((1, N_BRANCH, BRANCH, D_MODEL), lambda i: (layer, 0, 0, 0)),
                  pl.BlockSpec((1, D_MODEL, D_MODEL), lambda i: (layer, 0, 0)),
                  pl.BlockSpec((1, D_MODEL, D_MODEL), lambda i: (layer, 0, 0)),
                  pl.BlockSpec((1, 2, D_MODEL), lambda i: (layer, 0, 0))],
        out_specs=[tok(D_MODEL), tok(D_MODEL)],
        out_shape=[jax.ShapeDtypeStruct((m, D_MODEL), F32)] * 2,
        compiler_params=_cparams("parallel"),
        name="branch_merge",
    )(x, *brs, h, wb, wo, wq, ln)


def _attn_kernel(q_ref, k_ref, v_ref, o_ref, *, heads_split):
    q = q_ref[0]
    heads = range(XA_HEADS)
    cols = [slice(h * XA_DH, (h + 1) * XA_DH) for h in heads]
    if heads_split:
        tq = q.shape[0]
        k2 = k_ref[0, 0].reshape(MEM_LEN * XA_HEADS, XA_DH)
        v2 = v_ref[0, 0].reshape(MEM_LEN * XA_HEADS, XA_DH)
        qs = jnp.concatenate([q[:, cols[h]] for h in heads], axis=0)
        colh = lax.broadcasted_iota(jnp.int32, (1, MEM_LEN * XA_HEADS), 1) & (XA_HEADS - 1)
        rowh = lax.broadcasted_iota(jnp.int32, (XA_HEADS * tq, 1), 0) >> int(math.log2(tq))
        s = jnp.where(colh == rowh, _bdot_nt(qs, k2) * (XA_DH ** -0.5), -1e30)
        e = jnp.exp(s - jnp.max(s, -1, keepdims=True))
        o = _bdot(e / jnp.sum(e, -1, keepdims=True), v2)
        o_ref[0] = jnp.concatenate([o[h * tq:(h + 1) * tq] for h in heads], axis=1)
    else:
        ss = [_bdot_nt(q[:, cols[h]], k_ref[0, 0, :, cols[h]]) * (XA_DH ** -0.5) for h in heads]
        es = [jnp.exp(s - jnp.max(s, -1, keepdims=True)) for s in ss]
        ps = [e / jnp.sum(e, -1, keepdims=True) for e in es]
        o_ref[0] = jnp.concatenate([_bdot(ps[h], v_ref[0, 0, :, cols[h]]) for h in heads], axis=1)


def _attention(q3, mem_k, mem_v, layer_mem, tq):
    nseq, length, _ = q3.shape
    heads_split = mem_k.ndim == 5
    if heads_split:
        mem = pl.BlockSpec((1, 1, MEM_LEN, XA_HEADS, XA_DH), lambda b, i: (layer_mem, b, 0, 0, 0))
    else:
        mem = pl.BlockSpec((1, 1, MEM_LEN, D_MODEL), lambda b, i: (layer_mem, b, 0, 0))
    return pl.pallas_call(
        functools.partial(_attn_kernel, heads_split=heads_split),
        grid=(nseq, length // tq),
        in_specs=[pl.BlockSpec((1, tq, D_MODEL), lambda b, i: (b, i, 0)), mem, mem],
        out_specs=pl.BlockSpec((1, tq, D_MODEL), lambda b, i: (b, i, 0)),
        out_shape=jax.ShapeDtypeStruct((nseq, length, D_MODEL), F32),
        compiler_params=_cparams("parallel", "arbitrary"),
        name="memory_attention",
    )(q3, mem_k, mem_v)


def _post_kernel(x1_ref, o_ref, wo_ref, ln_ref, wr_ref, br_ref, x2_ref, comb_ref):
    y = jnp.dot(o_ref[...].astype(BF16), wo_ref[0], preferred_element_type=F32)
    x2 = _layer_norm(DN_ALPHA * x1_ref[...] + y, ln_ref[0, 0:1, :], ln_ref[0, 1:2, :])
    x2_ref[...] = x2
    logits = jnp.dot(x2.astype(BF16), wr_ref[...], preferred_element_type=F32) + br_ref[...]
    e = jnp.exp(logits - jnp.max(logits, -1, keepdims=True))
    p = e / jnp.sum(e, -1, keepdims=True)
    lane = lax.broadcasted_iota(jnp.int32, p.shape, 1)
    lanef = lane.astype(F32)
    grp = lane >> 2
    best = None
    for gidx in range(N_EXPERTS // EXPERTS_PER_GROUP):
        gm = jnp.max(jnp.where(grp == gidx, p, -1.0), -1, keepdims=True)
        if best is None:
            best, gi = gm, jnp.zeros(gm.shape, jnp.int32)
        else:
            upd = gm > best
            gi = jnp.where(upd, gidx, gi)
            best = jnp.where(upd, gm, best)
    cand = jnp.where(grp == gi, p, -1.0)
    m1 = jnp.max(cand, -1, keepdims=True)
    e1 = jnp.min(jnp.where(cand == m1, lanef, 1e9), -1, keepdims=True)
    cand2 = jnp.where(lanef == e1, -2.0, cand)
    m2 = jnp.max(cand2, -1, keepdims=True)
    e2 = jnp.min(jnp.where(cand2 == m2, lanef, 1e9), -1, keepdims=True)
    tot = m1 + m2
    comb = jnp.where(lanef == e1, m1 / tot, 0.0) + jnp.where(lanef == e2, m2 / tot, 0.0)
    comb_ref[...] = comb + jnp.where(lane == GROUP_LANE, gi.astype(F32), 0.0)


def _post(x1, o, wo, ln, wr, br, layer, tm):
    m = x1.shape[0]
    tok = lambda w: pl.BlockSpec((tm, w), lambda i: (i, 0))
    return pl.pallas_call(
        _post_kernel,
        grid=(m // tm,),
        in_specs=[tok(D_MODEL), tok(D_MODEL),
                  pl.BlockSpec((1, D_MODEL, D_MODEL), lambda i: (layer, 0, 0)),
                  pl.BlockSpec((1, 2, D_MODEL), lambda i: (layer, 0, 0)),
                  pl.BlockSpec((D_MODEL, LANES), lambda i: (0, 0)),
                  pl.BlockSpec((1, LANES), lambda i: (0, 0))],
        out_specs=[tok(D_MODEL), tok(LANES)],
        out_shape=[jax.ShapeDtypeStruct((m, D_MODEL), F32), jax.ShapeDtypeStruct((m, LANES), F32)],
        compiler_params=_cparams("parallel"),
        name="attn_out_router",
    )(x1, o, wo, ln, wr, br)


def _moe_kernel(x_ref, comb_ref, w1_ref, w3_ref, w2_ref, ln_ref, o_ref, acc_ref, xb_ref):
    e = pl.program_id(1)

    @pl.when(e == 0)
    def _():
        acc_ref[...] = jnp.zeros_like(acc_ref)
        xb_ref[...] = x_ref[...].astype(BF16)

    comb = comb_ref[...]
    lane = lax.broadcasted_iota(jnp.int32, comb.shape, 1)
    ce = jnp.sum(jnp.where(lane == e, comb, 0.0), -1, keepdims=True)
    xb = xb_ref[...]
    h1 = jnp.dot(xb, w1_ref[0, 0].astype(BF16), preferred_element_type=F32)
    h3 = jnp.dot(xb, w3_ref[0, 0].astype(BF16), preferred_element_type=F32)
    hid = _silu(h1) * h3 * ce
    acc_ref[...] += jnp.dot(hid.astype(BF16), w2_ref[0, 0].astype(BF16),
                            preferred_element_type=F32)

    @pl.when(e == pl.num_programs(1) - 1)
    def _():
        o_ref[...] = _layer_norm(DN_ALPHA * x_ref[...] + acc_ref[...],
                                 ln_ref[0, 0:1, :], ln_ref[0, 1:2, :])


def _moe(x2, comb, w1, w3, w2, ln, layer, tm):
    m = x2.shape[0]
    w_up = pl.BlockSpec((1, 1, D_MODEL, D_EXPERT), lambda i, e: (layer, e, 0, 0))
    return pl.pallas_call(
        _moe_kernel,
        grid=(m // tm, N_EXPERTS),
        in_specs=[pl.BlockSpec((tm, D_MODEL), lambda i, e: (i, 0)),
                  pl.BlockSpec((tm, LANES), lambda i, e: (i, 0)),
                  w_up, w_up,
                  pl.BlockSpec((1, 1, D_EXPERT, D_MODEL), lambda i, e: (layer, e, 0, 0)),
                  pl.BlockSpec((1, 2, D_MODEL), lambda i, e: (layer, 0, 0))],
        out_specs=pl.BlockSpec((tm, D_MODEL), lambda i, e: (i, 0)),
        out_shape=jax.ShapeDtypeStruct((m, D_MODEL), F32),
        scratch_shapes=[pltpu.VMEM((tm, D_MODEL), F32), pltpu.VMEM((tm, D_MODEL), BF16)],
        compiler_params=_cparams("parallel", "arbitrary"),
        name="moe_experts",
    )(x2, comb, w1, w3, w2, ln)


N_GROUPS = N_EXPERTS // EXPERTS_PER_GROUP
GROUP_LANE = N_EXPERTS


def _bucket_cap(tm):
    return tm // N_GROUPS + tm // 16


def _split_bf16(x):
    hi = x.astype(BF16)
    return hi, (x - hi.astype(F32)).astype(BF16)


def _moe_sort_kernel(x_ref, comb_ref, xs_ref, gs_ref, dest_ref, cnt_ref, *, cap):
    tm = x_ref.shape[0]
    slots = N_GROUPS * cap
    comb = comb_ref[...]
    lanef = lax.broadcasted_iota(jnp.int32, comb.shape, 1).astype(F32)
    gid = comb[:, GROUP_LANE:GROUP_LANE + 1]
    onehot = jnp.where(lanef == gid, 1.0, 0.0)
    rr = lax.broadcasted_iota(jnp.int32, (tm, tm), 0)
    cc = lax.broadcasted_iota(jnp.int32, (tm, tm), 1)
    tril = jnp.where(rr >= cc, 1.0, 0.0).astype(BF16)
    cs = jnp.dot(tril, onehot.astype(BF16), preferred_element_type=F32)
    rank = jnp.sum(onehot * cs, -1, keepdims=True) - 1.0
    dest = gid * float(cap) + rank
    cnt_ref[0] = jnp.broadcast_to(cs[tm - 1:tm, :], (SUBLANES, LANES))
    dest_b = jnp.broadcast_to(dest, (tm, LANES))
    dest_ref[...] = dest_b
    dest_row = dest_b.T[0:1, :]
    srow = lax.broadcasted_iota(jnp.int32, (slots, 1), 0).astype(F32)
    perm = jnp.where(srow == dest_row, 1.0, 0.0).astype(BF16)
    xs = jnp.dot(perm, x_ref[...].astype(BF16), preferred_element_type=F32).astype(xs_ref.dtype)
    hi, lo = _split_bf16(comb)
    gs = (jnp.dot(perm, hi, preferred_element_type=F32)
          + jnp.dot(perm, lo, preferred_element_type=F32))
    for g in range(N_GROUPS):
        xs_ref[g, 0] = xs[g * cap:(g + 1) * cap, :]
        gs_ref[g, 0] = gs[g * cap:(g + 1) * cap, :]


def _moe_sort(x2, comb, tm):
    m = x2.shape[0]
    cap = _bucket_cap(tm)
    nt = m // tm
    return pl.pallas_call(
        functools.partial(_moe_sort_kernel, cap=cap),
        grid=(nt,),
        in_specs=[pl.BlockSpec((tm, D_MODEL), lambda i: (i, 0)),
                  pl.BlockSpec((tm, LANES), lambda i: (i, 0))],
        out_specs=[pl.BlockSpec((N_GROUPS, 1, cap, D_MODEL), lambda i: (0, i, 0, 0)),
                   pl.BlockSpec((N_GROUPS, 1, cap, LANES), lambda i: (0, i, 0, 0)),
                   pl.BlockSpec((tm, LANES), lambda i: (i, 0)),
                   pl.BlockSpec((1, SUBLANES, LANES), lambda i: (i, 0, 0))],
        out_shape=[jax.ShapeDtypeStruct((N_GROUPS, nt, cap, D_MODEL), BF16),
                   jax.ShapeDtypeStruct((N_GROUPS, nt, cap, LANES), F32),
                   jax.ShapeDtypeStruct((m, LANES), F32),
                   jax.ShapeDtypeStruct((nt, SUBLANES, LANES), F32)],
        compiler_params=_cparams("parallel"),
        name="moe_bucket_sort",
    )(x2, comb)


def _moe_bucket_kernel(xs_ref, gs_ref, w1_ref, w3_ref, w2_ref, o_ref, acc_ref):
    g = pl.program_id(0)
    j = pl.program_id(2)
    rows = acc_ref.shape[0]

    @pl.when(j == 0)
    def _():
        acc_ref[...] = jnp.zeros_like(acc_ref)

    gates = gs_ref[0].reshape(rows, LANES)
    lane = lax.broadcasted_iota(jnp.int32, gates.shape, 1)
    ce = jnp.sum(jnp.where(lane == g * EXPERTS_PER_GROUP + j, gates, 0.0), -1, keepdims=True)
    xb = xs_ref[0].reshape(rows, D_MODEL)
    h1 = jnp.dot(xb, w1_ref[0, 0].astype(BF16), preferred_element_type=F32)
    h3 = jnp.dot(xb, w3_ref[0, 0].astype(BF16), preferred_element_type=F32)
    hid = _silu(h1) * h3 * ce
    acc_ref[...] += jnp.dot(hid.astype(BF16), w2_ref[0, 0].astype(BF16),
                            preferred_element_type=F32)

    @pl.when(j == pl.num_programs(2) - 1)
    def _():
        o_ref[0] = acc_ref[...].reshape(o_ref.shape[1:])


def _moe_buckets(xs, gs, w1, w3, w2, layer, tpq):
    _, nt, cap, _ = xs.shape

    def bucket(width):
        return pl.BlockSpec((1, tpq, cap, width), lambda g, q, j: (g, q, 0, 0))

    def expert(rows, cols):
        return pl.BlockSpec((1, 1, rows, cols),
                            lambda g, q, j: (layer, g * EXPERTS_PER_GROUP + j, 0, 0))

    return pl.pallas_call(
        _moe_bucket_kernel,
        grid=(N_GROUPS, nt // tpq, EXPERTS_PER_GROUP),
        in_specs=[bucket(D_MODEL), bucket(LANES),
                  expert(D_MODEL, D_EXPERT), expert(D_MODEL, D_EXPERT), expert(D_EXPERT, D_MODEL)],
        out_specs=bucket(D_MODEL),
        out_shape=jax.ShapeDtypeStruct((N_GROUPS, nt, cap, D_MODEL), F32),
        scratch_shapes=[pltpu.VMEM((tpq * cap, D_MODEL), F32)],
        compiler_params=_cparams("parallel", "parallel", "arbitrary"),
        name="moe_bucket_experts",
    )(xs, gs, w1, w3, w2)


def _moe_unsort_kernel(x_ref, dest_ref, ys_ref, ln_ref, o_ref):
    _, _, cap, _ = ys_ref.shape
    slots = N_GROUPS * cap
    dest = dest_ref[:, 0:1]
    slot = lax.broadcasted_iota(jnp.int32, (1, slots), 1).astype(F32)
    perm_t = jnp.where(dest == slot, 1.0, 0.0).astype(BF16)
    hi, lo = _split_bf16(ys_ref[...].reshape(slots, D_MODEL))
    y = (jnp.dot(perm_t, hi, preferred_element_type=F32)
         + jnp.dot(perm_t, lo, preferred_element_type=F32))
    o_ref[...] = _layer_norm(DN_ALPHA * x_ref[...] + y, ln_ref[0, 0:1, :], ln_ref[0, 1:2, :])


def _moe_unsort(x2, dest, ys, ln, layer, tm):
    m = x2.shape[0]
    _, _, cap, _ = ys.shape
    return pl.pallas_call(
        _moe_unsort_kernel,
        grid=(m // tm,),
        in_specs=[pl.BlockSpec((tm, D_MODEL), lambda i: (i, 0)),
                  pl.BlockSpec((tm, LANES), lambda i: (i, 0)),
                  pl.BlockSpec((N_GROUPS, 1, cap, D_MODEL), lambda i: (0, i, 0, 0)),
                  pl.BlockSpec((1, 2, D_MODEL), lambda i: (layer, 0, 0))],
        out_specs=pl.BlockSpec((tm, D_MODEL), lambda i: (i, 0)),
        out_shape=jax.ShapeDtypeStruct((m, D_MODEL), F32),
        compiler_params=_cparams("parallel"),
        name="moe_unsort_norm",
    )(x2, dest, ys, ln)


def _moe_bucketed(x2, comb, w1, w3, w2, ln, layer, tm, tpq):
    xs, gs, dest, cnt = _moe_sort(x2, comb, tm)
    overflow = jnp.max(cnt[:, 0, :N_GROUPS]) > _bucket_cap(tm)

    def bucketed():
        return _moe_unsort(x2, dest, _moe_buckets(xs, gs, w1, w3, w2, layer, tpq), ln, layer, tm)

    def dense():
        return _moe(x2, comb, w1, w3, w2, ln, layer, tm)

    return lax.cond(overflow, dense, bucketed)


def _prep_w_in(w_in):
    o = np.cumsum((0, GDN_QKV, 512, 4, 4, 256, 256, 512, 512, 512, 512, 4096))
    seg = lambda i: w_in[:, :, o[i]:o[i + 1]]
    swap = np.arange(256) ^ 1
    main = jnp.concatenate([seg(0), seg(1), seg(4), seg(5), seg(4)[:, :, swap], seg(5)[:, :, swap],
                            seg(6), seg(7), seg(10), seg(8), seg(9)], axis=-1).astype(BF16)
    ba = jnp.concatenate([seg(2), seg(3), jnp.zeros((DEPTH, D_MODEL, LANES - 8), w_in.dtype)],
                         axis=-1).astype(BF16)
    return main, ba


def _rope_tables(pos):
    inv_freq = 1.0 / (ROPE_BASE ** jnp.linspace(0.0, 1.0, RET_DK // 2, dtype=F32))
    ang = pos.astype(F32)[:, None] * inv_freq
    cos = jnp.repeat(jnp.cos(ang), 2, axis=1)
    sin = jnp.repeat(jnp.sin(ang), 2, axis=1)
    sign = jnp.tile(jnp.array([-1.0, 1.0], F32), RET_DK // 2)
    return jnp.tile(cos, (1, RET_HEADS)), jnp.tile(sin * sign, (1, RET_HEADS))


def _prep_s5(a_re, a_im, log_dt, b_re, b_im, c_re, c_im):
    dt = jnp.exp(log_dt.astype(F32))[..., None]
    mag = jnp.exp(a_re * dt)
    ab_re, ab_im = mag * jnp.cos(a_im * dt), mag * jnp.sin(a_im * dt)
    den = a_re * a_re + a_im * a_im
    coef_re = ((ab_re - 1.0) * a_re + ab_im * a_im) / den
    coef_im = (ab_im * a_re - (ab_re - 1.0) * a_im) / den
    bb_re = coef_re[..., None] * b_re - coef_im[..., None] * b_im
    bb_im = coef_re[..., None] * b_im + coef_im[..., None] * b_re
    eye = jnp.eye(16, dtype=F32)

    def pack_b(bb):
        x = bb.reshape(DEPTH, 2, 16, S5_STATE, S5_GROUP)
        return jnp.einsum('dkgpc,gh->dkgchp', x, eye).reshape(DEPTH, 2, 256, 1024)

    def pack_c(cm):
        x = cm.reshape(DEPTH, 2, 16, S5_GROUP, S5_STATE)
        return jnp.einsum('dkgcp,gh->dkgphc', x, eye).reshape(DEPTH, 2, 1024, 256)

    bd = jnp.concatenate([pack_b(bb_re), pack_b(bb_im)], axis=-1).astype(BF16)
    cd = jnp.concatenate([pack_c(c_re.astype(F32)), -pack_c(c_im.astype(F32))], axis=-2).astype(BF16)
    ar = ab_re.reshape(DEPTH, 1, S5_CH)
    ai = ab_im.reshape(DEPTH, 1, S5_CH)
    return bd, cd, ar, ai


def kernel(x_prompt, x_sample, mem_prompt, state_gdn_conv, state_gdn, state_ret, state_pool,
           state_s5_re, state_s5_im, cache_mem_k, cache_mem_v, w_in, gdn_conv_w, gdn_a_log,
           gdn_dt_bias, gdn_norm_w, pool_w, pool_scale, s5_a_re, s5_a_im, s5_log_dt, s5_b_re,
           s5_b_im, s5_c_re, s5_c_im, s5_d, s5_w_glu, w_branch, w_out, xa_w_q, xa_w_k, xa_w_v,
           xa_w_o, ln_g, ln_b, w_router, b_router, moe_w1, moe_w3, moe_w2):
    bp, seq, _ = x_prompt.shape
    bs, dseq, _ = x_sample.shape
    past = 16384

    w_main, w_ba = _prep_w_in(w_in)
    gp = jnp.zeros((DEPTH, 2, LANES), F32)
    gp = gp.at[:, 0, 4:8].set(gdn_a_log.astype(F32)).at[:, 1, 4:8].set(gdn_dt_bias.astype(F32))
    nw = gdn_norm_w.astype(F32).reshape(DEPTH, 1, GDN_DK)
    cw = gdn_conv_w.astype(F32)
    pw = pool_w.astype(BF16)
    psc = pool_scale.astype(F32).reshape(DEPTH, 1, BRANCH)
    bd, cd, ar, ai = _prep_s5(s5_a_re.astype(F32), s5_a_im.astype(F32), s5_log_dt,
                              s5_b_re.astype(F32), s5_b_im.astype(F32), s5_c_re, s5_c_im)
    d_skip = s5_d.astype(F32).reshape(DEPTH, 1, BRANCH)
    glu = s5_w_glu.astype(BF16)
    wb = w_branch.astype(BF16)
    wo = w_out.astype(BF16)
    wq, wk, wv, wxo = (w.astype(BF16) for w in (xa_w_q, xa_w_k, xa_w_v, xa_w_o))
    ln = jnp.stack([ln_g.astype(F32), ln_b.astype(F32)], axis=2)
    ln1, ln2, ln3 = ln[:, 0], ln[:, 1], ln[:, 2]
    wr = jnp.concatenate([w_router, jnp.zeros((D_MODEL, LANES - N_EXPERTS), w_router.dtype)],
                         axis=1).astype(BF16)
    br = jnp.concatenate([b_router.astype(F32), jnp.full((LANES - N_EXPERTS,), -1e30, F32)])[None]

    cos_p, sin_p = _rope_tables(jnp.arange(seq))
    cos_s, sin_s = _rope_tables(past + jnp.arange(dseq))
    ns_s = 16
    cos_s, sin_s = jnp.tile(cos_s, (ns_s, 1)), jnp.tile(sin_s, (ns_s, 1))

    zeros = lambda *s: jnp.zeros((1,) + s, F32)
    z_conv, z_gdn = zeros(bp, GDN_CONV - 1, GDN_QKV), zeros(bp, GDN_HEADS, 128, 128)
    z_ret, z_pool, z_s5 = zeros(bp, 256, RET_DV), zeros(bp, POOL_BUF, BRANCH), zeros(bp, S5_CH)
    st_ret = state_ret.reshape(DEPTH, bs, 256, RET_DV)
    st_s5r = state_s5_re.reshape(DEPTH, bs, S5_CH)
    st_s5i = state_s5_im.reshape(DEPTH, bs, S5_CH)
    mem2d = mem_prompt.reshape(bp * MEM_LEN, D_MODEL)
    ck, cv = cache_mem_k, cache_mem_v

    def block(x2d, nseq, length, layer, mem_k, mem_v, layer_mem, states, layer_state, cfg):
        conv0, gdn0, ret0, pool0, s5r0, s5i0 = states
        tm = cfg["tm"]
        h, hba = _in_proj(x2d, w_main, w_ba, layer, tm, 2304)
        h3 = h.reshape(nseq, length, H_COLS)
        ba3 = hba.reshape(nseq, length, LANES)
        br_a, new_gdn = _gdn(h3, ba3, conv0, gdn0, layer_state, cw, gp, nw, layer,
                             cfg["gdn_nb"], cfg["ns"], cfg["gdn_rl"], cfg["gdn_sl"])
        br_b, new_ret = _retention(h3, cfg["cos"], cfg["sin"], ret0, layer_state,
                                   cfg["ns"], cfg["ret_sl"])
        br_c = _pool(h3, pool0, layer_state, pw, psc, layer, cfg["ns"], cfg["pool_sl"], cfg["pos0"])
        br_d, new_re, new_im = _s5(h3, s5r0, s5i0, layer_state, bd, cd, ar, ai, d_skip, glu,
                                   layer, cfg["s5_tl"])
        brs = [br_a, br_b, br_c, br_d.reshape(nseq * length, BRANCH)]
        x1, q = _merge(x2d, h, brs, wb, wo, wq, ln1, layer, cfg["tm_merge"])
        o = _attention(q.reshape(nseq, length, D_MODEL), mem_k, mem_v, layer_mem, cfg["tq"])
        x2, comb = _post(x1, o.reshape(nseq * length, D_MODEL), wxo, ln2, wr, br, layer, tm)
        if cfg["moe_tpq"]:
            x3 = _moe_bucketed(x2, comb, moe_w1, moe_w3, moe_w2, ln3, layer, tm, cfg["moe_tpq"])
        else:
            x3 = _moe(x2, comb, moe_w1, moe_w3, moe_w2, ln3, layer, tm)
        new_conv = h3[:, length - (GDN_CONV - 1):, COL_QKV:COL_QKV + GDN_QKV]
        pool_u = h3[:, :, COL_POOL:COL_POOL + BRANCH]
        return x3, (new_conv, new_gdn, new_ret.reshape(nseq, RET_HEADS, RET_DK, RET_DV), pool_u,
                    new_re.reshape(nseq, S5_GROUPS, S5_STATE), new_im.reshape(nseq, S5_GROUPS, S5_STATE))

    cfg_p = dict(tm=1024, tm_merge=512, ns=1, gdn_nb=1, gdn_rl=256, gdn_sl=64, ret_sl=256,
                 pool_sl=512, s5_tl=128, tq=512, pos0=0, cos=cos_p, sin=sin_p, moe_tpq=4)
    cfg_s = dict(tm=1024, tm_merge=512, ns=ns_s, gdn_nb=1, gdn_rl=dseq, gdn_sl=dseq, ret_sl=dseq,
                 pool_sl=dseq, s5_tl=dseq, tq=dseq, pos0=past, cos=cos_s, sin=sin_s, moe_tpq=0)

    yp = x_prompt.reshape(bp * seq, D_MODEL)
    ys = x_sample.reshape(bs * dseq, D_MODEL)
    p_out, s_out = [], []
    for l in range(DEPTH):
        mem_k = _matmul(mem2d, wk, l, 1024, 1024)
        mem_v = _matmul(mem2d, wv, l, 1024, 1024)
        mk4 = mem_k.reshape(1, bp, MEM_LEN, D_MODEL)
        mv4 = mem_v.reshape(1, bp, MEM_LEN, D_MODEL)
        yp, st = block(yp, bp, seq, l, mk4, mv4, 0,
                       (z_conv, z_gdn, z_ret, z_pool, z_s5, z_s5), 0, cfg_p)
        conv, gdn, ret, pool_u, s5r, s5i = st
        p_out.append((conv, gdn, ret, pool_u[:, seq - POOL_BUF:], s5r, s5i,
                      mem_k.reshape(bp, MEM_LEN, XA_HEADS, XA_DH),
                      mem_v.reshape(bp, MEM_LEN, XA_HEADS, XA_DH)))
        ys, st = block(ys, bs, dseq, l, ck, cv, l,
                       (state_gdn_conv, state_gdn, st_ret, state_pool, st_s5r, st_s5i), l, cfg_s)
        conv, gdn, ret, pool_u, s5r, s5i = st
        new_pool = jnp.concatenate([state_pool[l][:, dseq:], pool_u], axis=1)
        s_out.append((conv, gdn, ret, new_pool, s5r, s5i))
    p_st = [jnp.stack(t) for t in zip(*p_out)]
    s_st = [jnp.stack(t) for t in zip(*s_out)]
    return (yp.reshape(bp, seq, D_MODEL), ys.reshape(bs, dseq, D_MODEL), *p_st, *s_st)
```

```python
import functools
import math

import jax
import jax.numpy as jnp
import numpy as np
from jax import lax
from jax.experimental import pallas as pl
from jax.experimental.pallas import tpu as pltpu

F32 = jnp.float32
BF16 = jnp.bfloat16

D_MODEL = 1024
DEPTH = 4
BRANCH = 512
N_BRANCH = 4
GDN_HEADS = 4
GDN_DK = 128
GDN_QKV = 1536
GDN_CONV = 4
RET_HEADS = 4
RET_DK = 64
RET_DV = 128
ROPE_BASE = 10000.0
POOL_WINDOWS = (2, 4, 8, 16)
POOL_BUF = 15
S5_GROUPS = 32
S5_GROUP = 16
S5_STATE = 64
S5_CH = S5_GROUPS * S5_STATE
MEM_LEN = 256
XA_HEADS = 4
XA_DH = 256
N_EXPERTS = 16
EXPERTS_PER_GROUP = 4
D_EXPERT = 512
DN_ALPHA = (2.0 * DEPTH) ** 0.25
LN_EPS = 1e-5
RMS_EPS = 1e-6

LANES = 128
SUBLANES = 8
VMEM_LIMIT = 52 * 1024 * 1024

COL_QKV = 0
COL_Z = 1536
COL_RET = 2048
COL_GATE = 4096
COL_POOL = 8192
COL_S5 = 8704
H_COLS = 9216


def _cparams(*sem):
    return pltpu.CompilerParams(dimension_semantics=sem, vmem_limit_bytes=VMEM_LIMIT)


def _bdot(a, b):
    return jnp.dot(a.astype(BF16), b.astype(BF16), preferred_element_type=F32)


def _bdot_nt(a, b):
    return lax.dot_general(a.astype(BF16), b.astype(BF16), (((1,), (1,)), ((), ())),
                           preferred_element_type=F32)


def _silu(x):
    return x * jax.nn.sigmoid(x)


def _layer_norm(x, g, b):
    mu = jnp.mean(x, -1, keepdims=True)
    xc = x - mu
    var = jnp.mean(xc * xc, -1, keepdims=True)
    return xc * lax.rsqrt(var + LN_EPS) * g + b


def _mm_kernel(x_ref, w_ref, o_ref, xb_ref):
    @pl.when(pl.program_id(1) == 0)
    def _():
        xb_ref[...] = x_ref[...].astype(BF16)

    o_ref[...] = jnp.dot(xb_ref[...], w_ref[0], preferred_element_type=F32)


def _matmul(x, w, layer, tm, tn):
    m, k = x.shape
    n = w.shape[-1]
    return pl.pallas_call(
        _mm_kernel,
        grid=(m // tm, n // tn),
        in_specs=[pl.BlockSpec((tm, k), lambda i, j: (i, 0)),
                  pl.BlockSpec((1, k, tn), lambda i, j: (layer, 0, j))],
        out_specs=pl.BlockSpec((tm, tn), lambda i, j: (i, j)),
        out_shape=jax.ShapeDtypeStruct((m, n), F32),
        scratch_shapes=[pltpu.VMEM((tm, k), BF16)],
        compiler_params=_cparams("parallel", "arbitrary"),
        name="token_matmul",
    )(x, w)


def _inproj_kernel(x_ref, w_ref, wba_ref, o_ref, ba_ref, xb_ref):
    @pl.when(pl.program_id(1) == 0)
    def _():
        xb_ref[...] = x_ref[...].astype(BF16)
        ba_ref[...] = jnp.dot(xb_ref[...], wba_ref[0], preferred_element_type=F32)

    o_ref[...] = jnp.dot(xb_ref[...], w_ref[0], preferred_element_type=F32)


def _in_proj(x, w, w_ba, layer, tm, tn):
    m, k = x.shape
    n = w.shape[-1]
    return pl.pallas_call(
        _inproj_kernel,
        grid=(m // tm, n // tn),
        in_specs=[pl.BlockSpec((tm, k), lambda i, j: (i, 0)),
                  pl.BlockSpec((1, k, tn), lambda i, j: (layer, 0, j)),
                  pl.BlockSpec((1, k, LANES), lambda i, j: (layer, 0, 0))],
        out_specs=[pl.BlockSpec((tm, tn), lambda i, j: (i, j)),
                   pl.BlockSpec((tm, LANES), lambda i, j: (i, 0))],
        out_shape=[jax.ShapeDtypeStruct((m, n), F32), jax.ShapeDtypeStruct((m, LANES), F32)],
        scratch_shapes=[pltpu.VMEM((tm, k), BF16)],
        compiler_params=_cparams("parallel", "arbitrary"),
        name="in_proj",
    )(x, w, w_ba)


def _branch_out_spec(ns, sl, n_chunks):
    return pl.BlockSpec((ns * sl, BRANCH), lambda b, n: (b * n_chunks + n, 0))


def _seg_ids(c, sl):
    r = lax.broadcasted_iota(jnp.int32, (c, 1), 0)
    sh = int(math.log2(sl))
    return r, r & (sl - 1), r >> sh


def _gdn_kernel(qkv_ref, z_ref, ba_ref, cs_ref, s0_ref, cw_ref, gp_ref, nw_ref,
                o_ref, s_ref, full_ref, *, nb, ns, rl, sl):
    n = pl.program_id(1)

    @pl.when(n == 0)
    def _():
        full_ref[:, 5:8, :] = cs_ref[0]
        s_ref[...] = s0_ref[0]

    for bi in range(nb):
        _gdn_block(qkv_ref, z_ref, ba_ref, cw_ref, gp_ref, nw_ref, o_ref, s_ref, full_ref,
                   bi=bi, ns=ns, rl=rl, sl=sl)


def _gdn_block(qkv_ref, z_ref, ba_ref, cw_ref, gp_ref, nw_ref, o_ref, s_ref, full_ref,
               *, bi, ns, rl, sl):
    c = ns * rl
    nsps = rl // sl
    sq = slice(bi * ns, (bi + 1) * ns)

    u3 = qkv_ref[sq]
    full_ref[sq, 8:8 + rl, :] = u3
    cw = cw_ref[0]
    acc = (cw[3:4] * u3 + cw[2:3] * full_ref[sq, 7:7 + rl, :] + cw[1:2] * full_ref[sq, 6:6 + rl, :]
           + cw[0:1] * full_ref[sq, 5:5 + rl, :])
    full_ref[sq, 5:8, :] = full_ref[sq, 5 + rl:8 + rl, :]
    qkv = _silu(acc).reshape(c, GDN_QKV)

    ba = ba_ref[sq].reshape(c, LANES)
    beta_t = jax.nn.sigmoid(ba)
    xs = ba + gp_ref[0, 1:2, :]
    softplus = jnp.maximum(xs, 0.0) + jnp.log1p(jnp.exp(-jnp.abs(xs)))
    g_t = -jnp.exp(gp_ref[0, 0:1, :]) * softplus

    r, t, seq = _seg_ids(c, sl)
    gc = g_t
    s = 1
    while s < sl:
        gc = gc + jnp.where(t >= s, pltpu.roll(gc, s, 0), 0.0)
        s *= 2
    tot = gc
    s = 1
    while s < sl:
        tot = jnp.where(t + s < sl, pltpu.roll(tot, c - s, 0), tot)
        s *= 2
    rc = tot - gc
    if c < LANES:
        gc_pad = jnp.concatenate([gc, jnp.zeros((LANES - c, LANES), F32)], axis=0)
    else:
        gc_pad = gc
    gc_t = gc_pad.T

    rr = lax.broadcasted_iota(jnp.int32, (c, c), 0)
    cc = lax.broadcasted_iota(jnp.int32, (c, c), 1)
    sh = int(math.log2(sl))
    same = (rr >> sh) == (cc >> sh)
    causal = jnp.logical_and(same, rr >= cc)
    strict = jnp.logical_and(same, rr > cc)
    eye = (rr == cc).astype(F32)

    z = z_ref[sq].reshape(c, BRANCH)
    nw = nw_ref[0]
    heads = range(GDN_HEADS)

    qs, ks, kbs, q_decs, k_dec_ts, decays, rhss = [], [], [], [], [], [], []
    for h in heads:
        q = qkv[:, h * 128:(h + 1) * 128]
        k = qkv[:, 512 + h * 128:512 + (h + 1) * 128]
        v = qkv[:, 1024 + h * 128:1024 + (h + 1) * 128]
        q = q * lax.rsqrt(jnp.sum(q * q, -1, keepdims=True) + RMS_EPS) * (GDN_DK ** -0.5)
        k = k * lax.rsqrt(jnp.sum(k * k, -1, keepdims=True) + RMS_EPS)
        beta = beta_t[:, h:h + 1]
        gcc = gc[:, 4 + h:5 + h]
        gcr = gc_t[4 + h:5 + h, 0:c]
        eg = jnp.exp(gcc)
        kb = k * beta
        qs.append(q)
        ks.append(k)
        kbs.append(kb)
        q_decs.append(q * eg)
        k_dec_ts.append((k * jnp.exp(rc[:, 4 + h:5 + h])).T)
        decays.append(jnp.where(causal, jnp.exp(jnp.where(causal, gcc - gcr, 0.0)), 0.0))
        rhss.append(jnp.concatenate([v * beta, kb * eg], axis=1))
    npows = [-jnp.where(strict, _bdot_nt(kbs[h], ks[h]) * decays[h], 0.0) for h in heads]
    sms = [eye + npows[h] for h in heads]
    if sl > 2:
        npows = [_bdot(npows[h], npows[h]) for h in heads]
        m = 2
        while 2 * m < sl:
            prods = [_bdot(npows[h], jnp.concatenate([npows[h], sms[h]], axis=1)) for h in heads]
            npows = [p[:, :c] for p in prods]
            sms = [sms[h] + prods[h][:, c:] for h in heads]
            m *= 2
        sms = [sms[h] + _bdot(npows[h], sms[h]) for h in heads]
    sols = [_bdot(sms[h], rhss[h]) for h in heads]
    attns = [jnp.where(causal, _bdot_nt(qs[h], ks[h]) * decays[h], 0.0) for h in heads]

    v_parts = [[] for _ in heads]
    o_parts = [[] for _ in heads]
    for si in range(ns):
        sts = [s_ref[bi * ns + si, h] for h in heads]
        for j in range(nsps):
            sg = si * nsps + j
            rows = slice(sg * sl, (sg + 1) * sl)
            for h in heads:
                v_j = sols[h][rows, :128] - _bdot(sols[h][rows, 128:], sts[h])
                o_parts[h].append(_bdot(q_decs[h][rows], sts[h]))
                v_parts[h].append(v_j)
                pieces = [v_j]
                if sg > 0:
                    pieces.insert(0, jnp.zeros((sg * sl, 128), F32))
                if (sg + 1) * sl < c:
                    pieces.append(jnp.zeros((c - (sg + 1) * sl, 128), F32))
                vz = v_j if len(pieces) == 1 else jnp.concatenate(pieces, axis=0)
                last = jnp.exp(tot[sg * sl:sg * sl + 1, 4 + h:5 + h])
                sts[h] = sts[h] * last + _bdot(k_dec_ts[h], vz)
        for h in heads:
            s_ref[bi * ns + si, h] = sts[h]

    outs = []
    for h in heads:
        v_new = v_parts[h][0] if len(v_parts[h]) == 1 else jnp.concatenate(v_parts[h], axis=0)
        o_cross = o_parts[h][0] if len(o_parts[h]) == 1 else jnp.concatenate(o_parts[h], axis=0)
        o = o_cross + _bdot(attns[h], v_new)
        o = o * lax.rsqrt(jnp.mean(o * o, -1, keepdims=True) + RMS_EPS) * nw
        outs.append(o * _silu(z[:, h * 128:(h + 1) * 128]))
    o_ref[bi] = jnp.concatenate(outs, axis=1).astype(o_ref.dtype)


def _gdn(h3, ba3, conv_state, s0, layer_state, cw, gp, nw, layer, nb, ns, rl, sl):
    nseq, length, _ = h3.shape
    assert ns == 1 or rl == length
    nq = nb * ns
    grid = (nseq // nq, length // rl)
    kern = functools.partial(_gdn_kernel, nb=nb, ns=ns, rl=rl, sl=sl)
    branch, new_state = pl.pallas_call(
        kern,
        grid=grid,
        in_specs=[
            pl.BlockSpec((nq, rl, GDN_QKV), lambda b, n: (b, n, COL_QKV // GDN_QKV)),
            pl.BlockSpec((nq, rl, BRANCH), lambda b, n: (b, n, COL_Z // BRANCH)),
            pl.BlockSpec((nq, rl, LANES), lambda b, n: (b, n, 0)),
            pl.BlockSpec((1, nq, GDN_CONV - 1, GDN_QKV), lambda b, n: (layer_state, b, 0, 0)),
            pl.BlockSpec((1, nq, GDN_HEADS, 128, 128), lambda b, n: (layer_state, b, 0, 0, 0)),
            pl.BlockSpec((1, GDN_CONV, GDN_QKV), lambda b, n: (layer, 0, 0)),
            pl.BlockSpec((1, 2, LANES), lambda b, n: (layer, 0, 0)),
            pl.BlockSpec((1, 1, LANES), lambda b, n: (layer, 0, 0)),
        ],
        out_specs=[
            pl.BlockSpec((nb, ns * rl, BRANCH), lambda b, n: (b, n, 0)),
            pl.BlockSpec((nq, GDN_HEADS, 128, 128), lambda b, n: (b, 0, 0, 0)),
        ],
        out_shape=[jax.ShapeDtypeStruct((nseq // ns, ns * length, BRANCH), BF16),
                   jax.ShapeDtypeStruct((nseq, GDN_HEADS, 128, 128), F32)],
        scratch_shapes=[pltpu.VMEM((nq, SUBLANES + rl, GDN_QKV), F32)],
        compiler_params=_cparams("parallel", "arbitrary"),
        name="gated_deltanet",
    )(h3, h3, ba3, conv_state, s0, cw, gp, nw)
    return branch.reshape(nseq * length, BRANCH), new_state


_RET_LOG_GAMMA = tuple(math.log(1.0 - 2.0 ** (-5.0 - h)) for h in range(RET_HEADS))


def _per_head(idx, vals):
    out = jnp.full(idx.shape, vals[3], F32)
    for h in (2, 1, 0):
        out = jnp.where(idx < (h + 1) * RET_DK, vals[h], out)
    return out


def _ret_kernel(hb_ref, cos_ref, sin_ref, s0_ref, o_ref, s_ref, *, ns, sl):
    c = ns * sl
    n = pl.program_id(1)

    @pl.when(n == 0)
    def _():
        s_ref[...] = s0_ref[0]

    hb = hb_ref[...].reshape(c, 2048)
    cos = cos_ref[...]
    sin = sin_ref[...]
    rq = (hb[:, 0:256] * cos + hb[:, 512:768] * sin) * (RET_DK ** -0.5)
    rk = hb[:, 256:512] * cos + hb[:, 768:1024] * sin
    v = hb[:, 1024:1536]
    g = hb[:, 1536:2048]

    r, t, seq = _seg_ids(c, sl)
    tf = t.astype(F32)
    lane = lax.broadcasted_iota(jnp.int32, (1, 256), 1)
    lgl = _per_head(lane, _RET_LOG_GAMMA)
    q_dec = rq * jnp.exp(lgl * (tf + 1.0))
    k_dec = rk * jnp.exp(lgl * (sl - 1.0 - tf))
    k_dec_t = k_dec.T

    rr = lax.broadcasted_iota(jnp.int32, (c, c), 0)
    cc = lax.broadcasted_iota(jnp.int32, (c, c), 1)
    sh = int(math.log2(sl))
    causal = jnp.logical_and((rr >> sh) == (cc >> sh), rr >= cc)
    rel = jnp.maximum(rr - cc, 0).astype(F32)
    cseq = lax.broadcasted_iota(jnp.int32, (1, c), 1) >> sh
    rowi = lax.broadcasted_iota(jnp.int32, (256, 1), 0)
    cd_rows = jnp.exp(_per_head(rowi, _RET_LOG_GAMMA) * float(sl))

    cross = []
    for si in range(ns):
        rows = slice(si * sl, (si + 1) * sl)
        st = s_ref[si]
        parts = []
        for h in range(RET_HEADS):
            mh = jnp.logical_and(lane >= h * RET_DK, lane < (h + 1) * RET_DK)
            parts.append(_bdot(jnp.where(mh, q_dec[rows], 0.0), st))
        cross.append(parts)
        kd = k_dec_t if ns == 1 else jnp.where(cseq == si, k_dec_t, 0.0)
        res = _bdot(kd, v)
        upd = jnp.zeros((256, RET_DV), F32)
        for h in range(RET_HEADS):
            rm = jnp.logical_and(rowi >= h * RET_DK, rowi < (h + 1) * RET_DK)
            upd = upd + jnp.where(rm, res[:, h * 128:(h + 1) * 128], 0.0)
        s_ref[si] = st * cd_rows + upd

    heads = range(RET_HEADS)
    mhs = [jnp.logical_and(lane >= h * RET_DK, lane < (h + 1) * RET_DK) for h in heads]
    scs = [_bdot_nt(jnp.where(mhs[h], rq, 0.0), rk)
           * jnp.where(causal, jnp.exp(_RET_LOG_GAMMA[h] * rel), 0.0) for h in heads]
    os_ = [_bdot(scs[h], v[:, h * 128:(h + 1) * 128]) for h in heads]
    outs = []
    for h in heads:
        oc = cross[0][h] if ns == 1 else jnp.concatenate([cross[si][h] for si in range(ns)], axis=0)
        o = os_[h] + oc
        mu = jnp.mean(o, -1, keepdims=True)
        oc2 = o - mu
        o = oc2 * lax.rsqrt(jnp.mean(oc2 * oc2, -1, keepdims=True) + LN_EPS)
        outs.append(_silu(g[:, h * 128:(h + 1) * 128]) * o)
    o_ref[...] = jnp.concatenate(outs, axis=1).astype(o_ref.dtype)


def _retention(h3, cos_tab, sin_tab, s0, layer_state, ns, sl):
    nseq, length, _ = h3.shape
    c = ns * sl
    grid = (nseq // ns, length // sl)
    kern = functools.partial(_ret_kernel, ns=ns, sl=sl)
    return pl.pallas_call(
        kern,
        grid=grid,
        in_specs=[
            pl.BlockSpec((ns, sl, 2048), lambda b, n: (b, n, COL_RET // 2048)),
            pl.BlockSpec((c, 256), lambda b, n: (n, 0)),
            pl.BlockSpec((c, 256), lambda b, n: (n, 0)),
            pl.BlockSpec((1, ns, 256, RET_DV), lambda b, n: (layer_state, b, 0, 0)),
        ],
        out_specs=[
            _branch_out_spec(ns, sl, grid[1]),
            pl.BlockSpec((ns, 256, RET_DV), lambda b, n: (b, 0, 0)),
        ],
        out_shape=[jax.ShapeDtypeStruct((nseq * length, BRANCH), BF16),
                   jax.ShapeDtypeStruct((nseq, 256, RET_DV), F32)],
        compiler_params=_cparams("parallel", "arbitrary"),
        name="retention",
    )(h3, cos_tab, sin_tab, s0)


def _pool_kernel(u_ref, st_ref, w_ref, sc_ref, o_ref, full_ref, *, ns, sl, pos0, carry):
    c = ns * sl
    n = pl.program_id(1)

    @pl.when(n == 0)
    def _():
        full_ref[:, 1:16, :] = st_ref[0]

    u3 = u_ref[...]
    full_ref[:, 16:16 + sl, :] = u3
    tpos = lax.broadcasted_iota(jnp.int32, (1, sl, 1), 1) + n * sl
    n_avail = (tpos + (pos0 + 1)).astype(F32)
    outs = []
    for gi, w in enumerate(POOL_WINDOWS):
        cols = slice(gi * 128, (gi + 1) * 128)
        ug = u3[:, :, cols]
        acc = ug
        for j in range(1, w):
            acc = acc + full_ref[:, 16 - j:16 - j + sl, cols]
        pooled = acc / jnp.minimum(n_avail, float(w)) - ug
        mixed = _bdot(pooled.reshape(c, 128), w_ref[0, gi])
        outs.append(mixed * sc_ref[0, :, cols])
    if carry:
        full_ref[:, 1:16, :] = full_ref[:, sl + 1:sl + 16, :]
    o_ref[...] = jnp.concatenate(outs, axis=1).astype(o_ref.dtype)


def _pool(h3, state, layer_state, w_pool, pool_scale, layer, ns, sl, pos0):
    nseq, length, _ = h3.shape
    grid = (nseq // ns, length // sl)
    kern = functools.partial(_pool_kernel, ns=ns, sl=sl, pos0=pos0, carry=grid[1] > 1)
    return pl.pallas_call(
        kern,
        grid=grid,
        in_specs=[
            pl.BlockSpec((ns, sl, BRANCH), lambda b, n: (b, n, COL_POOL // BRANCH)),
            pl.BlockSpec((1, ns, POOL_BUF, BRANCH), lambda b, n: (layer_state, b, 0, 0)),
            pl.BlockSpec((1, 4, 128, 128), lambda b, n: (layer, 0, 0, 0)),
            pl.BlockSpec((1, 1, BRANCH), lambda b, n: (layer, 0, 0)),
        ],
        out_specs=_branch_out_spec(ns, sl, grid[1]),
        out_shape=jax.ShapeDtypeStruct((nseq * length, BRANCH), BF16),
        scratch_shapes=[pltpu.VMEM((ns, 16 + sl, BRANCH), F32)],
        compiler_params=_cparams("parallel", "arbitrary"),
        name="multi_pool",
    )(h3, state, w_pool, pool_scale)


S5_SEQS = SUBLANES
S5_SLABS = 2 * S5_CH // LANES


def _s5_pitch(tl):
    return tl + 4


def _s5_kernel(u_ref, hr_ref, hi_ref, bd_ref, cd_ref, ar_ref, ai_ref, d_ref, glu_ref,
               o_ref, or_ref, oi_ref, bu_ref, h_ref, arb_ref, aib_ref, up_ref, *, tl):
    pitch = _s5_pitch(tl)
    n = pl.program_id(1)

    @pl.when(n == 0)
    def _():
        hr = hr_ref[0]
        hi = hi_ref[0]
        for kb in range(2):
            for j in range(8):
                src = slice(kb * 1024 + j * 128, kb * 1024 + (j + 1) * 128)
                h_ref[kb * 16 + j] = hr[:, src]
                h_ref[kb * 16 + 8 + j] = hi[:, src]
        for kb in range(2):
            for j in range(8):
                src = slice(kb * 1024 + j * 128, kb * 1024 + (j + 1) * 128)
                arb_ref[kb * 8 + j] = jnp.broadcast_to(ar_ref[0, :, src], (S5_SEQS, LANES))
                aib_ref[kb * 8 + j] = jnp.broadcast_to(ai_ref[0, :, src], (S5_SEQS, LANES))

        up_ref[...] = jnp.zeros_like(up_ref)

    for s in range(S5_SEQS):
        up_ref[s * pitch:s * pitch + tl, :] = u_ref[s]
    up = up_ref[...]
    upb = up.astype(BF16)
    for kb in range(2):
        bu = jnp.dot(upb[:, kb * 256:(kb + 1) * 256], bd_ref[0, kb],
                     preferred_element_type=F32)
        for j in range(16):
            bu_ref[kb * 16 + j] = bu[:, j * 128:(j + 1) * 128]

    def step(l, carry):
        new = []
        for kb in range(2):
            for j in range(8):
                re = carry[kb * 16 + j]
                im = carry[kb * 16 + 8 + j]
                ar = arb_ref[kb * 8 + j]
                ai = aib_ref[kb * 8 + j]
                idx = pl.ds(l, S5_SEQS, stride=pitch)
                b_re = bu_ref[kb * 16 + j, idx, :]
                b_im = bu_ref[kb * 16 + 8 + j, idx, :]
                n_re = ar * re - ai * im + b_re
                n_im = ar * im + ai * re + b_im
                bu_ref[kb * 16 + j, idx, :] = n_re
                bu_ref[kb * 16 + 8 + j, idx, :] = n_im
                new.append((kb * 16 + j, n_re))
                new.append((kb * 16 + 8 + j, n_im))
        new.sort(key=lambda p: p[0])
        return tuple(p[1] for p in new)

    h0 = tuple(h_ref[i] for i in range(S5_SLABS))
    hf = lax.fori_loop(0, tl, step, h0, unroll=2)
    for i in range(S5_SLABS):
        h_ref[i] = hf[i]

    @pl.when(n == pl.num_programs(1) - 1)
    def _():
        for kb in range(2):
            for j in range(8):
                dst = slice(kb * 1024 + j * 128, kb * 1024 + (j + 1) * 128)
                or_ref[:, dst] = hf[kb * 16 + j]
                oi_ref[:, dst] = hf[kb * 16 + 8 + j]

    ys = []
    for kb in range(2):
        hs = jnp.concatenate([bu_ref[kb * 16 + j].astype(BF16) for j in range(16)], axis=1)
        ys.append(jnp.dot(hs, cd_ref[0, kb], preferred_element_type=F32))
    y = jnp.concatenate(ys, axis=1) + d_ref[0] * up
    act = jax.nn.gelu(y)
    out = act * jax.nn.sigmoid(_bdot(act, glu_ref[0]))
    for s in range(S5_SEQS):
        o_ref[s] = out[s * pitch:s * pitch + tl, :]


def _s5(h3, h_re, h_im, layer_state, bd, cd, ar, ai, d_skip, w_glu, layer, tl):
    nseq, length, _ = h3.shape
    grid = (nseq // S5_SEQS, length // tl)
    kern = functools.partial(_s5_kernel, tl=tl)
    rows = S5_SEQS * _s5_pitch(tl)
    return pl.pallas_call(
        kern,
        grid=grid,
        in_specs=[
            pl.BlockSpec((S5_SEQS, tl, BRANCH), lambda b, n: (b, n, COL_S5 // BRANCH)),
            pl.BlockSpec((1, S5_SEQS, S5_CH), lambda b, n: (layer_state, b, 0)),
            pl.BlockSpec((1, S5_SEQS, S5_CH), lambda b, n: (layer_state, b, 0)),
            pl.BlockSpec((1, 2, 256, 2048), lambda b, n: (layer, 0, 0, 0)),
            pl.BlockSpec((1, 2, 2048, 256), lambda b, n: (layer, 0, 0, 0)),
            pl.BlockSpec((1, 1, S5_CH), lambda b, n: (layer, 0, 0)),
            pl.BlockSpec((1, 1, S5_CH), lambda b, n: (layer, 0, 0)),
            pl.BlockSpec((1, 1, BRANCH), lambda b, n: (layer, 0, 0)),
            pl.BlockSpec((1, BRANCH, BRANCH), lambda b, n: (layer, 0, 0)),
        ],
        out_specs=[
            pl.BlockSpec((S5_SEQS, tl, BRANCH), lambda b, n: (b, n, 0)),
            pl.BlockSpec((S5_SEQS, S5_CH), lambda b, n: (b, 0)),
            pl.BlockSpec((S5_SEQS, S5_CH), lambda b, n: (b, 0)),
        ],
        out_shape=[jax.ShapeDtypeStruct((nseq, length, BRANCH), F32),
                   jax.ShapeDtypeStruct((nseq, S5_CH), F32),
                   jax.ShapeDtypeStruct((nseq, S5_CH), F32)],
        scratch_shapes=[pltpu.VMEM((S5_SLABS, rows, LANES), F32),
                        pltpu.VMEM((S5_SLABS, S5_SEQS, LANES), F32),
                        pltpu.VMEM((S5_SLABS // 2, S5_SEQS, LANES), F32),
                        pltpu.VMEM((S5_SLABS // 2, S5_SEQS, LANES), F32),
                        pltpu.VMEM((rows, BRANCH), F32)],
        compiler_params=_cparams("parallel", "arbitrary"),
        name="s5_ssm",
    )(h3, h_re, h_im, bd, cd, ar, ai, d_skip, w_glu)


def _merge_kernel(x_ref, ba_ref, bb_ref, bc_ref, bd_ref, g_ref, wb_ref, wo_ref, wq_ref, ln_ref,
                  x1_ref, q_ref):
    mixed = None
    for i, br in enumerate((ba_ref, bb_ref, bc_ref, bd_ref)):
        proj = jnp.dot(br[...].astype(BF16), wb_ref[0, i], preferred_element_type=F32)
        term = jax.nn.sigmoid(g_ref[:, i * D_MODEL:(i + 1) * D_MODEL]) * proj
        mixed = term if mixed is None else mixed + term
    y = jnp.dot(mixed.astype(BF16), wo_ref[0], preferred_element_type=F32)
    x1 = _layer_norm(DN_ALPHA * x_ref[...] + y, ln_ref[0, 0:1, :], ln_ref[0, 1:2, :])
    x1_ref[...] = x1
    q_ref[...] = jnp.dot(x1.astype(BF16), wq_ref[0], preferred_element_type=F32)


def _merge(x, h, brs, wb, wo, wq, ln, layer, tm):
    m = x.shape[0]
    tok = lambda w: pl.BlockSpec((tm, w), lambda i: (i, 0))
    return pl.pallas_call(
        _merge_kernel,
        grid=(m // tm,),
        in_specs=[tok(D_MODEL), tok(BRANCH), tok(BRANCH), tok(BRANCH), tok(BRANCH),
                  pl.BlockSpec((tm, 4096), lambda i: (i, COL_GATE // 4096)),
                  pl.BlockSpec((1, N_BRANCH, BRANCH, D_MODEL), lambda i: (layer, 0, 0, 0)),
                  pl.BlockSpec((1, D_MODEL, D_MODEL), lambda i: (layer, 0, 0)),
                  pl.BlockSpec((1, D_MODEL, D_MODEL), lambda i: (layer, 0, 0)),
                  pl.BlockSpec((1, 2, D_MODEL), lambda i: (layer, 0, 0))],
        out_specs=[tok(D_MODEL), tok(D_MODEL)],
        out_shape=[jax.ShapeDtypeStruct((m, D_MODEL), F32)] * 2,
        compiler_params=_cparams("parallel"),
        name="branch_merge",
    )(x, *brs, h, wb, wo, wq, ln)


def _attn_kernel(q_ref, k_ref, v_ref, o_ref, *, heads_split):
    q = q_ref[0]
    heads = range(XA_HEADS)
    cols = [slice(h * XA_DH, (h + 1) * XA_DH) for h in heads]
    if heads_split:
        tq = q.shape[0]
        k2 = k_ref[0, 0].reshape(MEM_LEN * XA_HEADS, XA_DH)
        v2 = v_ref[0, 0].reshape(MEM_LEN * XA_HEADS, XA_DH)
        qs = jnp.concatenate([q[:, cols[h]] for h in heads], axis=0)
        colh = lax.broadcasted_iota(jnp.int32, (1, MEM_LEN * XA_HEADS), 1) & (XA_HEADS - 1)
        rowh = lax.broadcasted_iota(jnp.int32, (XA_HEADS * tq, 1), 0) >> int(math.log2(tq))
        s = jnp.where(colh == rowh, _bdot_nt(qs, k2) * (XA_DH ** -0.5), -1e30)
        e = jnp.exp(s - jnp.max(s, -1, keepdims=True))
        o = _bdot(e / jnp.sum(e, -1, keepdims=True), v2)
        o_ref[0] = jnp.concatenate([o[h * tq:(h + 1) * tq] for h in heads], axis=1)
    else:
        ss = [_bdot_nt(q[:, cols[h]], k_ref[0, 0, :, cols[h]]) * (XA_DH ** -0.5) for h in heads]
        es = [jnp.exp(s - jnp.max(s, -1, keepdims=True)) for s in ss]
        ps = [e / jnp.sum(e, -1, keepdims=True) for e in es]
        o_ref[0] = jnp.concatenate([_bdot(ps[h], v_ref[0, 0, :, cols[h]]) for h in heads], axis=1)


def _attention(q3, mem_k, mem_v, layer_mem, tq):
    nseq, length, _ = q3.shape
    heads_split = mem_k.ndim == 5
    if heads_split:
        mem = pl.BlockSpec((1, 1, MEM_LEN, XA_HEADS, XA_DH), lambda b, i: (layer_mem, b, 0, 0, 0))
    else:
        mem = pl.BlockSpec((1, 1, MEM_LEN, D_MODEL), lambda b, i: (layer_mem, b, 0, 0))
    return pl.pallas_call(
        functools.partial(_attn_kernel, heads_split=heads_split),
        grid=(nseq, length // tq),
        in_specs=[pl.BlockSpec((1, tq, D_MODEL), lambda b, i: (b, i, 0)), mem, mem],
        out_specs=pl.BlockSpec((1, tq, D_MODEL), lambda b, i: (b, i, 0)),
        out_shape=jax.ShapeDtypeStruct((nseq, length, D_MODEL), F32),
        compiler_params=_cparams("parallel", "arbitrary"),
        name="memory_attention",
    )(q3, mem_k, mem_v)


def _post_kernel(x1_ref, o_ref, wo_ref, ln_ref, wr_ref, br_ref, x2_ref, comb_ref):
    y = jnp.dot(o_ref[...].astype(BF16), wo_ref[0], preferred_element_type=F32)
    x2 = _layer_norm(DN_ALPHA * x1_ref[...] + y, ln_ref[0, 0:1, :], ln_ref[0, 1:2, :])
    x2_ref[...] = x2
    tm = x2.shape[0]
    logits = _bdot_nt(wr_ref[...], x2) + br_ref[:, 0:1]
    e = jnp.exp(logits - jnp.max(logits, 0, keepdims=True))
    p = e / jnp.sum(e, 0, keepdims=True)
    rowi = lax.broadcasted_iota(jnp.int32, (N_EXPERTS, 1), 0)
    rowf = rowi.astype(F32)
    grp = rowi >> 2
    best = None
    for gidx in range(N_GROUPS):
        gm = jnp.max(p[gidx * EXPERTS_PER_GROUP:(gidx + 1) * EXPERTS_PER_GROUP], 0, keepdims=True)
        if best is None:
            best, gi = gm, jnp.zeros(gm.shape, jnp.int32)
        else:
            upd = gm > best
            gi = jnp.where(upd, gidx, gi)
            best = jnp.where(upd, gm, best)
    cand = jnp.where(grp == gi, p, -1.0)
    m1 = jnp.max(cand, 0, keepdims=True)
    e1 = jnp.min(jnp.where(cand == m1, rowf, 1e9), 0, keepdims=True)
    cand2 = jnp.where(rowf == e1, -2.0, cand)
    m2 = jnp.max(cand2, 0, keepdims=True)
    e2 = jnp.min(jnp.where(cand2 == m2, rowf, 1e9), 0, keepdims=True)
    tot = m1 + m2
    comb_t = jnp.where(rowf == e1, m1 / tot, 0.0) + jnp.where(rowf == e2, m2 / tot, 0.0)
    sub = lax.broadcasted_iota(jnp.int32, (SUBLANES, 1), 0)
    grp_rows = jnp.where(sub == 0, gi.astype(F32), 0.0)
    rest = jnp.zeros((LANES - N_EXPERTS - SUBLANES, tm), F32)
    comb_ref[...] = jnp.concatenate([comb_t, grp_rows, rest], axis=0).T


def _post(x1, o, wo, ln, wr, br, layer, tm):
    m = x1.shape[0]
    tok = lambda w: pl.BlockSpec((tm, w), lambda i: (i, 0))
    return pl.pallas_call(
        _post_kernel,
        grid=(m // tm,),
        in_specs=[tok(D_MODEL), tok(D_MODEL),
                  pl.BlockSpec((1, D_MODEL, D_MODEL), lambda i: (layer, 0, 0)),
                  pl.BlockSpec((1, 2, D_MODEL), lambda i: (layer, 0, 0)),
                  pl.BlockSpec((N_EXPERTS, D_MODEL), lambda i: (0, 0)),
                  pl.BlockSpec((N_EXPERTS, LANES), lambda i: (0, 0))],
        out_specs=[tok(D_MODEL), tok(LANES)],
        out_shape=[jax.ShapeDtypeStruct((m, D_MODEL), F32), jax.ShapeDtypeStruct((m, LANES), F32)],
        compiler_params=_cparams("parallel"),
        name="attn_out_router",
    )(x1, o, wo, ln, wr, br)


def _moe_kernel(x_ref, comb_ref, w1_ref, w3_ref, w2_ref, ln_ref, o_ref, acc_ref, xb_ref):
    e = pl.program_id(1)

    @pl.when(e == 0)
    def _():
        acc_ref[...] = jnp.zeros_like(acc_ref)
        xb_ref[...] = x_ref[...].astype(BF16)

    comb = comb_ref[...]
    lane = lax.broadcasted_iota(jnp.int32, comb.shape, 1)
    ce = jnp.sum(jnp.where(lane == e, comb, 0.0), -1, keepdims=True)
    xb = xb_ref[...]
    h1 = jnp.dot(xb, w1_ref[0, 0].astype(BF16), preferred_element_type=F32)
    h3 = jnp.dot(xb, w3_ref[0, 0].astype(BF16), preferred_element_type=F32)
    hid = _silu(h1) * h3 * ce
    acc_ref[...] += jnp.dot(hid.astype(BF16), w2_ref[0, 0].astype(BF16),
                            preferred_element_type=F32)

    @pl.when(e == pl.num_programs(1) - 1)
    def _():
        o_ref[...] = _layer_norm(DN_ALPHA * x_ref[...] + acc_ref[...],
                                 ln_ref[0, 0:1, :], ln_ref[0, 1:2, :])


def _moe(x2, comb, w1, w3, w2, ln, layer, tm):
    m = x2.shape[0]
    w_up = pl.BlockSpec((1, 1, D_MODEL, D_EXPERT), lambda i, e: (layer, e, 0, 0))
    return pl.pallas_call(
        _moe_kernel,
        grid=(m // tm, N_EXPERTS),
        in_specs=[pl.BlockSpec((tm, D_MODEL), lambda i, e: (i, 0)),
                  pl.BlockSpec((tm, LANES), lambda i, e: (i, 0)),
                  w_up, w_up,
                  pl.BlockSpec((1, 1, D_EXPERT, D_MODEL), lambda i, e: (layer, e, 0, 0)),
                  pl.BlockSpec((1, 2, D_MODEL), lambda i, e: (layer, 0, 0))],
        out_specs=pl.BlockSpec((tm, D_MODEL), lambda i, e: (i, 0)),
        out_shape=jax.ShapeDtypeStruct((m, D_MODEL), F32),
        scratch_shapes=[pltpu.VMEM((tm, D_MODEL), F32), pltpu.VMEM((tm, D_MODEL), BF16)],
        compiler_params=_cparams("parallel", "arbitrary"),
        name="moe_experts",
    )(x2, comb, w1, w3, w2, ln)


N_GROUPS = N_EXPERTS // EXPERTS_PER_GROUP
GROUP_LANE = N_EXPERTS


def _bucket_cap(tm):
    return tm // N_GROUPS + tm // 16


def _split_bf16(x):
    hi = x.astype(BF16)
    return hi, (x - hi.astype(F32)).astype(BF16)


def _moe_sort_kernel(x_ref, comb_ref, xs_ref, gs_ref, dest_ref, cnt_ref, *, cap):
    tm = x_ref.shape[0]
    slots = N_GROUPS * cap
    comb = comb_ref[...]
    lanef = lax.broadcasted_iota(jnp.int32, comb.shape, 1).astype(F32)
    gid = comb[:, GROUP_LANE:GROUP_LANE + 1]
    onehot = jnp.where(lanef == gid, 1.0, 0.0)
    rr = lax.broadcasted_iota(jnp.int32, (tm, tm), 0)
    cc = lax.broadcasted_iota(jnp.int32, (tm, tm), 1)
    tril = jnp.where(rr >= cc, 1.0, 0.0).astype(BF16)
    cs = jnp.dot(tril, onehot.astype(BF16), preferred_element_type=F32)
    rank = jnp.sum(onehot * cs, -1, keepdims=True) - 1.0
    dest = gid * float(cap) + rank
    cnt_ref[0] = jnp.broadcast_to(cs[tm - 1:tm, :], (SUBLANES, LANES))
    dest_b = jnp.broadcast_to(dest, (tm, LANES))
    dest_ref[...] = dest_b
    dest_row = dest_b.T[0:1, :]
    srow = lax.broadcasted_iota(jnp.int32, (slots, 1), 0).astype(F32)
    perm = jnp.where(srow == dest_row, 1.0, 0.0).astype(BF16)
    xs = jnp.dot(perm, x_ref[...].astype(BF16), preferred_element_type=F32).astype(xs_ref.dtype)
    hi, lo = _split_bf16(comb)
    gs = (jnp.dot(perm, hi, preferred_element_type=F32)
          + jnp.dot(perm, lo, preferred_element_type=F32))
    for g in range(N_GROUPS):
        xs_ref[g, 0] = xs[g * cap:(g + 1) * cap, :]
        gs_ref[g, 0] = gs[g * cap:(g + 1) * cap, :]


def _moe_sort(x2, comb, tm):
    m = x2.shape[0]
    cap = _bucket_cap(tm)
    nt = m // tm
    return pl.pallas_call(
        functools.partial(_moe_sort_kernel, cap=cap),
        grid=(nt,),
        in_specs=[pl.BlockSpec((tm, D_MODEL), lambda i: (i, 0)),
                  pl.BlockSpec((tm, LANES), lambda i: (i, 0))],
        out_specs=[pl.BlockSpec((N_GROUPS, 1, cap, D_MODEL), lambda i: (0, i, 0, 0)),
                   pl.BlockSpec((N_GROUPS, 1, cap, LANES), lambda i: (0, i, 0, 0)),
                   pl.BlockSpec((tm, LANES), lambda i: (i, 0)),
                   pl.BlockSpec((1, SUBLANES, LANES), lambda i: (i, 0, 0))],
        out_shape=[jax.ShapeDtypeStruct((N_GROUPS, nt, cap, D_MODEL), BF16),
                   jax.ShapeDtypeStruct((N_GROUPS, nt, cap, LANES), F32),
                   jax.ShapeDtypeStruct((m, LANES), F32),
                   jax.ShapeDtypeStruct((nt, SUBLANES, LANES), F32)],
        compiler_params=_cparams("parallel"),
        name="moe_bucket_sort",
    )(x2, comb)


def _moe_bucket_kernel(xs_ref, gs_ref, w1_ref, w3_ref, w2_ref, o_ref, acc_ref):
    g = pl.program_id(0)
    j = pl.program_id(2)
    rows = acc_ref.shape[0]

    @pl.when(j == 0)
    def _():
        acc_ref[...] = jnp.zeros_like(acc_ref)

    gates = gs_ref[0].reshape(rows, LANES)
    lane = lax.broadcasted_iota(jnp.int32, gates.shape, 1)
    ce = jnp.sum(jnp.where(lane == g * EXPERTS_PER_GROUP + j, gates, 0.0), -1, keepdims=True)
    xb = xs_ref[0].reshape(rows, D_MODEL)
    h1 = jnp.dot(xb, w1_ref[0, 0].astype(BF16), preferred_element_type=F32)
    h3 = jnp.dot(xb, w3_ref[0, 0].astype(BF16), preferred_element_type=F32)
    hid = _silu(h1) * h3 * ce
    acc_ref[...] += jnp.dot(hid.astype(BF16), w2_ref[0, 0].astype(BF16),
                            preferred_element_type=F32)

    @pl.when(j == pl.num_programs(2) - 1)
    def _():
        o_ref[0] = acc_ref[...].reshape(o_ref.shape[1:])


def _moe_buckets(xs, gs, w1, w3, w2, layer, tpq):
    _, nt, cap, _ = xs.shape

    def bucket(width):
        return pl.BlockSpec((1, tpq, cap, width), lambda g, q, j: (g, q, 0, 0))

    def expert(rows, cols):
        return pl.BlockSpec((1, 1, rows, cols),
                            lambda g, q, j: (layer, g * EXPERTS_PER_GROUP + j, 0, 0))

    return pl.pallas_call(
        _moe_bucket_kernel,
        grid=(N_GROUPS, nt // tpq, EXPERTS_PER_GROUP),
        in_specs=[bucket(D_MODEL), bucket(LANES),
                  expert(D_MODEL, D_EXPERT), expert(D_MODEL, D_EXPERT), expert(D_EXPERT, D_MODEL)],
        out_specs=bucket(D_MODEL),
        out_shape=jax.ShapeDtypeStruct((N_GROUPS, nt, cap, D_MODEL), F32),
        scratch_shapes=[pltpu.VMEM((tpq * cap, D_MODEL), F32)],
        compiler_params=_cparams("parallel", "parallel", "arbitrary"),
        name="moe_bucket_experts",
    )(xs, gs, w1, w3, w2)


def _moe_unsort_kernel(x_ref, dest_ref, ys_ref, ln_ref, o_ref):
    _, _, cap, _ = ys_ref.shape
    slots = N_GROUPS * cap
    dest = dest_ref[:, 0:1]
    slot = lax.broadcasted_iota(jnp.int32, (1, slots), 1).astype(F32)
    perm_t = jnp.where(dest == slot, 1.0, 0.0).astype(BF16)
    hi, lo = _split_bf16(ys_ref[...].reshape(slots, D_MODEL))
    y = (jnp.dot(perm_t, hi, preferred_element_type=F32)
         + jnp.dot(perm_t, lo, preferred_element_type=F32))
    o_ref[...] = _layer_norm(DN_ALPHA * x_ref[...] + y, ln_ref[0, 0:1, :], ln_ref[0, 1:2, :])


def _moe_unsort(x2, dest, ys, ln, layer, tm):
    m = x2.shape[0]
    _, _, cap, _ = ys.shape
    return pl.pallas_call(
        _moe_unsort_kernel,
        grid=(m // tm,),
        in_specs=[pl.BlockSpec((tm, D_MODEL), lambda i: (i, 0)),
                  pl.BlockSpec((tm, LANES), lambda i: (i, 0)),
                  pl.BlockSpec((N_GROUPS, 1, cap, D_MODEL), lambda i: (0, i, 0, 0)),
                  pl.BlockSpec((1, 2, D_MODEL), lambda i: (layer, 0, 0))],
        out_specs=pl.BlockSpec((tm, D_MODEL), lambda i: (i, 0)),
        out_shape=jax.ShapeDtypeStruct((m, D_MODEL), F32),
        compiler_params=_cparams("parallel"),
        name="moe_unsort_norm",
    )(x2, dest, ys, ln)


def _moe_bucketed(x2, comb, w1, w3, w2, ln, layer, tm, tpq):
    xs, gs, dest, cnt = _moe_sort(x2, comb, tm)
    overflow = jnp.max(cnt[:, 0, :N_GROUPS]) > _bucket_cap(tm)

    def bucketed():
        return _moe_unsort(x2, dest, _moe_buckets(xs, gs, w1, w3, w2, layer, tpq), ln, layer, tm)

    def dense():
        return _moe(x2, comb, w1, w3, w2, ln, layer, tm)

    return lax.cond(overflow, dense, bucketed)


_W_IN_OFFS = np.cumsum((0, GDN_QKV, 512, 4, 4, 256, 256, 512, 512, 512, 512, 4096))


def _w_in_kernel(w_ref, o_ref, ba_ref):
    w = w_ref[0]
    o = _W_IN_OFFS
    seg = lambda i: w[:, int(o[i]):int(o[i + 1])]
    tk = w.shape[0]

    def swap_pairs(x):
        lane = lax.broadcasted_iota(jnp.int32, x.shape, 1)
        n = x.shape[1]
        return jnp.where((lane & 1) == 0, pltpu.roll(x, n - 1, 1), pltpu.roll(x, 1, 1))

    rq, rk = seg(4), seg(5)
    main = jnp.concatenate([seg(0), seg(1), rq, rk, swap_pairs(rq), swap_pairs(rk),
                            seg(6), seg(7), seg(10), seg(8), seg(9)], axis=1)
    o_ref[0] = main.astype(BF16)
    ba = jnp.concatenate([w[:, int(o[2]):int(o[4])], jnp.zeros((tk, LANES - 8), F32)], axis=1)
    ba_ref[0] = ba.astype(BF16)


def _prep_w_in_pallas(w_in, tk=256):
    depth, k, n = w_in.shape
    return pl.pallas_call(
        _w_in_kernel,
        grid=(depth, k // tk),
        in_specs=[pl.BlockSpec((1, tk, n), lambda l, i: (l, i, 0))],
        out_specs=[pl.BlockSpec((1, tk, H_COLS), lambda l, i: (l, i, 0)),
                   pl.BlockSpec((1, tk, LANES), lambda l, i: (l, i, 0))],
        out_shape=[jax.ShapeDtypeStruct((depth, k, H_COLS), BF16),
                   jax.ShapeDtypeStruct((depth, k, LANES), BF16)],
        compiler_params=_cparams("parallel", "parallel"),
        name="w_in_relayout",
    )(w_in)


def _rope_tables(pos):
    inv_freq = 1.0 / (ROPE_BASE ** jnp.linspace(0.0, 1.0, RET_DK // 2, dtype=F32))
    ang = pos.astype(F32)[:, None] * inv_freq
    cos = jnp.repeat(jnp.cos(ang), 2, axis=1)
    sin = jnp.repeat(jnp.sin(ang), 2, axis=1)
    sign = jnp.tile(jnp.array([-1.0, 1.0], F32), RET_DK // 2)
    return jnp.tile(cos, (1, RET_HEADS)), jnp.tile(sin * sign, (1, RET_HEADS))


def _prep_s5(a_re, a_im, log_dt, b_re, b_im, c_re, c_im):
    dt = jnp.exp(log_dt.astype(F32))[..., None]
    mag = jnp.exp(a_re * dt)
    ab_re, ab_im = mag * jnp.cos(a_im * dt), mag * jnp.sin(a_im * dt)
    den = a_re * a_re + a_im * a_im
    coef_re = ((ab_re - 1.0) * a_re + ab_im * a_im) / den
    coef_im = (ab_im * a_re - (ab_re - 1.0) * a_im) / den
    bb_re = coef_re[..., None] * b_re - coef_im[..., None] * b_im
    bb_im = coef_re[..., None] * b_im + coef_im[..., None] * b_re
    eye = jnp.eye(16, dtype=F32)

    def pack_b(bb):
        x = bb.reshape(DEPTH, 2, 16, S5_STATE, S5_GROUP)
        return jnp.einsum('dkgpc,gh->dkgchp', x, eye).reshape(DEPTH, 2, 256, 1024)

    def pack_c(cm):
        x = cm.reshape(DEPTH, 2, 16, S5_GROUP, S5_STATE)
        return jnp.einsum('dkgcp,gh->dkgphc', x, eye).reshape(DEPTH, 2, 1024, 256)

    bd = jnp.concatenate([pack_b(bb_re), pack_b(bb_im)], axis=-1).astype(BF16)
    cd = jnp.concatenate([pack_c(c_re.astype(F32)), -pack_c(c_im.astype(F32))], axis=-2).astype(BF16)
    ar = ab_re.reshape(DEPTH, 1, S5_CH)
    ai = ab_im.reshape(DEPTH, 1, S5_CH)
    return bd, cd, ar, ai


def kernel(x_prompt, x_sample, mem_prompt, state_gdn_conv, state_gdn, state_ret, state_pool,
           state_s5_re, state_s5_im, cache_mem_k, cache_mem_v, w_in, gdn_conv_w, gdn_a_log,
           gdn_dt_bias, gdn_norm_w, pool_w, pool_scale, s5_a_re, s5_a_im, s5_log_dt, s5_b_re,
           s5_b_im, s5_c_re, s5_c_im, s5_d, s5_w_glu, w_branch, w_out, xa_w_q, xa_w_k, xa_w_v,
           xa_w_o, ln_g, ln_b, w_router, b_router, moe_w1, moe_w3, moe_w2):
    bp, seq, _ = x_prompt.shape
    bs, dseq, _ = x_sample.shape
    past = 16384

    w_main, w_ba = _prep_w_in_pallas(w_in)
    gp = jnp.zeros((DEPTH, 2, LANES), F32)
    gp = gp.at[:, 0, 4:8].set(gdn_a_log.astype(F32)).at[:, 1, 4:8].set(gdn_dt_bias.astype(F32))
    nw = gdn_norm_w.astype(F32).reshape(DEPTH, 1, GDN_DK)
    cw = gdn_conv_w.astype(F32)
    pw = pool_w.astype(BF16)
    psc = pool_scale.astype(F32).reshape(DEPTH, 1, BRANCH)
    bd, cd, ar, ai = _prep_s5(s5_a_re.astype(F32), s5_a_im.astype(F32), s5_log_dt,
                              s5_b_re.astype(F32), s5_b_im.astype(F32), s5_c_re, s5_c_im)
    d_skip = s5_d.astype(F32).reshape(DEPTH, 1, BRANCH)
    glu = s5_w_glu.astype(BF16)
    wb = w_branch.astype(BF16)
    wo = w_out.astype(BF16)
    wq, wk, wv, wxo = (w.astype(BF16) for w in (xa_w_q, xa_w_k, xa_w_v, xa_w_o))
    ln = jnp.stack([ln_g.astype(F32), ln_b.astype(F32)], axis=2)
    ln1, ln2, ln3 = ln[:, 0], ln[:, 1], ln[:, 2]
    wr = w_router.T.astype(BF16)
    br = jnp.broadcast_to(b_router.astype(F32)[:, None], (N_EXPERTS, LANES))

    cos_p, sin_p = _rope_tables(jnp.arange(seq))
    cos_s, sin_s = _rope_tables(past + jnp.arange(dseq))
    ns_s = 16
    cos_s, sin_s = jnp.tile(cos_s, (ns_s, 1)), jnp.tile(sin_s, (ns_s, 1))

    zeros = lambda *s: jnp.zeros((1,) + s, F32)
    z_conv, z_gdn = zeros(bp, GDN_CONV - 1, GDN_QKV), zeros(bp, GDN_HEADS, 128, 128)
    z_ret, z_pool, z_s5 = zeros(bp, 256, RET_DV), zeros(bp, POOL_BUF, BRANCH), zeros(bp, S5_CH)
    st_ret = state_ret.reshape(DEPTH, bs, 256, RET_DV)
    st_s5r = state_s5_re.reshape(DEPTH, bs, S5_CH)
    st_s5i = state_s5_im.reshape(DEPTH, bs, S5_CH)
    mem2d = mem_prompt.reshape(bp * MEM_LEN, D_MODEL)
    ck, cv = cache_mem_k, cache_mem_v

    def block(x2d, nseq, length, layer, mem_k, mem_v, layer_mem, states, layer_state, cfg):
        conv0, gdn0, ret0, pool0, s5r0, s5i0 = states
        tm = cfg["tm"]
        h, hba = _in_proj(x2d, w_main, w_ba, layer, tm, 2304)
        h3 = h.reshape(nseq, length, H_COLS)
        ba3 = hba.reshape(nseq, length, LANES)
        br_a, new_gdn = _gdn(h3, ba3, conv0, gdn0, layer_state, cw, gp, nw, layer,
                             cfg["gdn_nb"], cfg["ns"], cfg["gdn_rl"], cfg["gdn_sl"])
        br_b, new_ret = _retention(h3, cfg["cos"], cfg["sin"], ret0, layer_state,
                                   cfg["ns"], cfg["ret_sl"])
        br_c = _pool(h3, pool0, layer_state, pw, psc, layer, cfg["ns"], cfg["pool_sl"], cfg["pos0"])
        br_d, new_re, new_im = _s5(h3, s5r0, s5i0, layer_state, bd, cd, ar, ai, d_skip, glu,
                                   layer, cfg["s5_tl"])
        brs = [br_a, br_b, br_c, br_d.reshape(nseq * length, BRANCH)]
        x1, q = _merge(x2d, h, brs, wb, wo, wq, ln1, layer, cfg["tm_merge"])
        o = _attention(q.reshape(nseq, length, D_MODEL), mem_k, mem_v, layer_mem, cfg["tq"])
        x2, comb = _post(x1, o.reshape(nseq * length, D_MODEL), wxo, ln2, wr, br, layer, tm)
        if cfg["moe_tpq"]:
            x3 = _moe_bucketed(x2, comb, moe_w1, moe_w3, moe_w2, ln3, layer, tm, cfg["moe_tpq"])
        else:
            x3 = _moe(x2, comb, moe_w1, moe_w3, moe_w2, ln3, layer, tm)
        new_conv = h3[:, length - (GDN_CONV - 1):, COL_QKV:COL_QKV + GDN_QKV]
        pool_u = h3[:, :, COL_POOL:COL_POOL + BRANCH]
        return x3, (new_conv, new_gdn, new_ret.reshape(nseq, RET_HEADS, RET_DK, RET_DV), pool_u,
                    new_re.reshape(nseq, S5_GROUPS, S5_STATE), new_im.reshape(nseq, S5_GROUPS, S5_STATE))

    cfg_p = dict(tm=1024, tm_merge=512, ns=1, gdn_nb=1, gdn_rl=256, gdn_sl=64, ret_sl=256,
                 pool_sl=512, s5_tl=128, tq=1024, pos0=0, cos=cos_p, sin=sin_p, moe_tpq=4)
    cfg_s = dict(tm=1024, tm_merge=512, ns=ns_s, gdn_nb=1, gdn_rl=dseq, gdn_sl=dseq, ret_sl=dseq,
                 pool_sl=dseq, s5_tl=dseq, tq=dseq, pos0=past, cos=cos_s, sin=sin_s, moe_tpq=0)

    yp = x_prompt.reshape(bp * seq, D_MODEL)
    ys = x_sample.reshape(bs * dseq, D_MODEL)
    p_out, s_out = [], []
    for l in range(DEPTH):
        mem_k = _matmul(mem2d, wk, l, 1024, 1024)
        mem_v = _matmul(mem2d, wv, l, 1024, 1024)
        mk4 = mem_k.reshape(1, bp, MEM_LEN, D_MODEL)
        mv4 = mem_v.reshape(1, bp, MEM_LEN, D_MODEL)
        yp, st = block(yp, bp, seq, l, mk4, mv4, 0,
                       (z_conv, z_gdn, z_ret, z_pool, z_s5, z_s5), 0, cfg_p)
        conv, gdn, ret, pool_u, s5r, s5i = st
        p_out.append((conv, gdn, ret, pool_u[:, seq - POOL_BUF:], s5r, s5i,
                      mem_k.reshape(bp, MEM_LEN, XA_HEADS, XA_DH),
                      mem_v.reshape(bp, MEM_LEN, XA_HEADS, XA_DH)))
        ys, st = block(ys, bs, dseq, l, ck, cv, l,
                       (state_gdn_conv, state_gdn, st_ret, state_pool, st_s5r, st_s5i), l, cfg_s)
        conv, gdn, ret, pool_u, s5r, s5i = st
        new_pool = jnp.concatenate([state_pool[l][:, dseq:], pool_u], axis=1)
        s_out.append((conv, gdn, ret, new_pool, s5r, s5i))
    p_st = [jnp.stack(t) for t in zip(*p_out)]
    s_st = [jnp.stack(t) for t in zip(*s_out)]
    return (yp.reshape(bp, seq, D_MODEL), ys.reshape(bs, dseq, D_MODEL), *p_st, *s_st)
```

```python
import functools
import math

import jax
import jax.numpy as jnp
import numpy as np
from jax import lax
from jax.experimental import pallas as pl
from jax.experimental.pallas import tpu as pltpu

F32 = jnp.float32
BF16 = jnp.bfloat16

D_MODEL = 1024
DEPTH = 4
BRANCH = 512
N_BRANCH = 4
GDN_HEADS = 4
GDN_DK = 128
GDN_QKV = 1536
GDN_CONV = 4
RET_HEADS = 4
RET_DK = 64
RET_DV = 128
ROPE_BASE = 10000.0
POOL_WINDOWS = (2, 4, 8, 16)
POOL_BUF = 15
S5_GROUPS = 32
S5_GROUP = 16
S5_STATE = 64
S5_CH = S5_GROUPS * S5_STATE
MEM_LEN = 256
XA_HEADS = 4
XA_DH = 256
N_EXPERTS = 16
EXPERTS_PER_GROUP = 4
D_EXPERT = 512
DN_ALPHA = (2.0 * DEPTH) ** 0.25
LN_EPS = 1e-5
RMS_EPS = 1e-6

LANES = 128
SUBLANES = 8
VMEM_LIMIT = 52 * 1024 * 1024

COL_QKV = 0
COL_Z = 1536
COL_RET = 2048
COL_GATE = 4096
COL_POOL = 8192
COL_S5 = 8704
H_COLS = 9216


def _cparams(*sem):
    return pltpu.CompilerParams(dimension_semantics=sem, vmem_limit_bytes=VMEM_LIMIT)


def _bdot(a, b):
    return jnp.dot(a.astype(BF16), b.astype(BF16), preferred_element_type=F32)


def _bdot_nt(a, b):
    return lax.dot_general(a.astype(BF16), b.astype(BF16), (((1,), (1,)), ((), ())),
                           preferred_element_type=F32)


def _silu(x):
    return x * jax.nn.sigmoid(x)


def _layer_norm(x, g, b):
    mu = jnp.mean(x, -1, keepdims=True)
    xc = x - mu
    var = jnp.mean(xc * xc, -1, keepdims=True)
    return xc * lax.rsqrt(var + LN_EPS) * g + b


def _mem_proj_kernel(x_ref, w_ref, o2_ref, o4_ref):
    res = jnp.dot(x_ref[...].astype(BF16), w_ref[0], preferred_element_type=F32)
    o2_ref[...] = res
    for h in range(XA_HEADS):
        o4_ref[0, :, h, :] = res[:, h * XA_DH:(h + 1) * XA_DH]


def _mem_proj(mem2d, w, layer):
    m = mem2d.shape[0]
    nb = m // MEM_LEN
    return pl.pallas_call(
        _mem_proj_kernel,
        grid=(nb,),
        in_specs=[pl.BlockSpec((MEM_LEN, D_MODEL), lambda b: (b, 0)),
                  pl.BlockSpec((1, D_MODEL, D_MODEL), lambda b: (layer, 0, 0))],
        out_specs=[pl.BlockSpec((MEM_LEN, D_MODEL), lambda b: (b, 0)),
                   pl.BlockSpec((1, MEM_LEN, XA_HEADS, XA_DH), lambda b: (b, 0, 0, 0))],
        out_shape=[jax.ShapeDtypeStruct((m, D_MODEL), F32),
                   jax.ShapeDtypeStruct((nb, MEM_LEN, XA_HEADS, XA_DH), F32)],
        compiler_params=_cparams("parallel"),
        name="memory_projection",
    )(mem2d, w)


def _inproj_kernel(x_ref, w_ref, wba_ref, o_ref, ba_ref, xb_ref):
    @pl.when(pl.program_id(1) == 0)
    def _():
        xb_ref[...] = x_ref[...].astype(BF16)
        ba_ref[...] = jnp.dot(xb_ref[...], wba_ref[0], preferred_element_type=F32)

    o_ref[...] = jnp.dot(xb_ref[...], w_ref[0], preferred_element_type=F32)


def _in_proj(x, w, w_ba, layer, tm, tn):
    m, k = x.shape
    n = w.shape[-1]
    return pl.pallas_call(
        _inproj_kernel,
        grid=(m // tm, n // tn),
        in_specs=[pl.BlockSpec((tm, k), lambda i, j: (i, 0)),
                  pl.BlockSpec((1, k, tn), lambda i, j: (layer, 0, j)),
                  pl.BlockSpec((1, k, LANES), lambda i, j: (layer, 0, 0))],
        out_specs=[pl.BlockSpec((tm, tn), lambda i, j: (i, j)),
                   pl.BlockSpec((tm, LANES), lambda i, j: (i, 0))],
        out_shape=[jax.ShapeDtypeStruct((m, n), F32), jax.ShapeDtypeStruct((m, LANES), F32)],
        scratch_shapes=[pltpu.VMEM((tm, k), BF16)],
        compiler_params=_cparams("parallel", "arbitrary"),
        name="in_proj",
    )(x, w, w_ba)


def _branch_out_spec(ns, sl, n_chunks):
    return pl.BlockSpec((ns * sl, BRANCH), lambda b, n: (b * n_chunks + n, 0))


def _seg_ids(c, sl):
    r = lax.broadcasted_iota(jnp.int32, (c, 1), 0)
    sh = int(math.log2(sl))
    return r, r & (sl - 1), r >> sh


def _gdn_kernel(qkv_ref, z_ref, ba_ref, cs_ref, s0_ref, cw_ref, gp_ref, nw_ref,
                o_ref, s_ref, full_ref, *, nb, ns, rl, sl):
    n = pl.program_id(1)

    @pl.when(n == 0)
    def _():
        full_ref[:, 5:8, :] = cs_ref[0]
        s_ref[...] = s0_ref[0]

    for bi in range(nb):
        _gdn_block(qkv_ref, z_ref, ba_ref, cw_ref, gp_ref, nw_ref, o_ref, s_ref, full_ref,
                   bi=bi, ns=ns, rl=rl, sl=sl)


def _gdn_block(qkv_ref, z_ref, ba_ref, cw_ref, gp_ref, nw_ref, o_ref, s_ref, full_ref,
               *, bi, ns, rl, sl):
    c = ns * rl
    nsps = rl // sl
    sq = slice(bi * ns, (bi + 1) * ns)

    u3 = qkv_ref[sq]
    full_ref[sq, 8:8 + rl, :] = u3
    cw = cw_ref[0]
    acc = (cw[3:4] * u3 + cw[2:3] * full_ref[sq, 7:7 + rl, :] + cw[1:2] * full_ref[sq, 6:6 + rl, :]
           + cw[0:1] * full_ref[sq, 5:5 + rl, :])
    full_ref[sq, 5:8, :] = full_ref[sq, 5 + rl:8 + rl, :]
    qkv = _silu(acc).reshape(c, GDN_QKV)

    ba = ba_ref[sq].reshape(c, LANES)
    beta_t = jax.nn.sigmoid(ba)
    xs = ba + gp_ref[0, 1:2, :]
    softplus = jnp.maximum(xs, 0.0) + jnp.log1p(jnp.exp(-jnp.abs(xs)))
    g_t = -jnp.exp(gp_ref[0, 0:1, :]) * softplus

    r, t, seq = _seg_ids(c, sl)
    gc = g_t
    s = 1
    while s < sl:
        gc = gc + jnp.where(t >= s, pltpu.roll(gc, s, 0), 0.0)
        s *= 2
    tot = gc
    s = 1
    while s < sl:
        tot = jnp.where(t + s < sl, pltpu.roll(tot, c - s, 0), tot)
        s *= 2
    rc = tot - gc
    if c < LANES:
        gc_pad = jnp.concatenate([gc, jnp.zeros((LANES - c, LANES), F32)], axis=0)
    else:
        gc_pad = gc
    gc_t = gc_pad.T

    rr = lax.broadcasted_iota(jnp.int32, (c, c), 0)
    cc = lax.broadcasted_iota(jnp.int32, (c, c), 1)
    sh = int(math.log2(sl))
    same = (rr >> sh) == (cc >> sh)
    causal = jnp.logical_and(same, rr >= cc)
    strict = jnp.logical_and(same, rr > cc)
    eye = (rr == cc).astype(F32)

    z = z_ref[sq].reshape(c, BRANCH)
    nw = nw_ref[0]
    heads = range(GDN_HEADS)

    qs, ks, kbs, q_decs, k_dec_ts, decays, rhss = [], [], [], [], [], [], []
    for h in heads:
        q = qkv[:, h * 128:(h + 1) * 128]
        k = qkv[:, 512 + h * 128:512 + (h + 1) * 128]
        v = qkv[:, 1024 + h * 128:1024 + (h + 1) * 128]
        q = q * lax.rsqrt(jnp.sum(q * q, -1, keepdims=True) + RMS_EPS) * (GDN_DK ** -0.5)
        k = k * lax.rsqrt(jnp.sum(k * k, -1, keepdims=True) + RMS_EPS)
        beta = beta_t[:, h:h + 1]
        gcc = gc[:, 4 + h:5 + h]
        gcr = gc_t[4 + h:5 + h, 0:c]
        eg = jnp.exp(gcc)
        kb = k * beta
        qs.append(q)
        ks.append(k)
        kbs.append(kb)
        q_decs.append(q * eg)
        k_dec_ts.append((k * jnp.exp(rc[:, 4 + h:5 + h])).T)
        decays.append(jnp.where(causal, jnp.exp(jnp.where(causal, gcc - gcr, 0.0)), 0.0))
        rhss.append(jnp.concatenate([v * beta, kb * eg], axis=1))
    npows = [-jnp.where(strict, _bdot_nt(kbs[h], ks[h]) * decays[h], 0.0) for h in heads]
    sms = [eye + npows[h] for h in heads]
    if sl > 2:
        npows = [_bdot(npows[h], npows[h]) for h in heads]
        m = 2
        while 2 * m < sl:
            prods = [_bdot(npows[h], jnp.concatenate([npows[h], sms[h]], axis=1)) for h in heads]
            npows = [p[:, :c] for p in prods]
            sms = [sms[h] + prods[h][:, c:] for h in heads]
            m *= 2
        sms = [sms[h] + _bdot(npows[h], sms[h]) for h in heads]
    sols = [_bdot(sms[h], rhss[h]) for h in heads]
    attns = [jnp.where(causal, _bdot_nt(qs[h], ks[h]) * decays[h], 0.0) for h in heads]

    v_parts = [[] for _ in heads]
    o_parts = [[] for _ in heads]
    for si in range(ns):
        sts = [s_ref[bi * ns + si, h] for h in heads]
        for j in range(nsps):
            sg = si * nsps + j
            rows = slice(sg * sl, (sg + 1) * sl)
            for h in heads:
                v_j = sols[h][rows, :128] - _bdot(sols[h][rows, 128:], sts[h])
                o_parts[h].append(_bdot(q_decs[h][rows], sts[h]))
                v_parts[h].append(v_j)
                pieces = [v_j]
                if sg > 0:
                    pieces.insert(0, jnp.zeros((sg * sl, 128), F32))
                if (sg + 1) * sl < c:
                    pieces.append(jnp.zeros((c - (sg + 1) * sl, 128), F32))
                vz = v_j if len(pieces) == 1 else jnp.concatenate(pieces, axis=0)
                last = jnp.exp(tot[sg * sl:sg * sl + 1, 4 + h:5 + h])
                sts[h] = sts[h] * last + _bdot(k_dec_ts[h], vz)
        for h in heads:
            s_ref[bi * ns + si, h] = sts[h]

    outs = []
    for h in heads:
        v_new = v_parts[h][0] if len(v_parts[h]) == 1 else jnp.concatenate(v_parts[h], axis=0)
        o_cross = o_parts[h][0] if len(o_parts[h]) == 1 else jnp.concatenate(o_parts[h], axis=0)
        o = o_cross + _bdot(attns[h], v_new)
        o = o * lax.rsqrt(jnp.mean(o * o, -1, keepdims=True) + RMS_EPS) * nw
        outs.append(o * _silu(z[:, h * 128:(h + 1) * 128]))
    o_ref[bi] = jnp.concatenate(outs, axis=1).astype(o_ref.dtype)


def _gdn(h3, ba3, conv_state, s0, layer_state, cw, gp, nw, layer, nb, ns, rl, sl):
    nseq, length, _ = h3.shape
    assert ns == 1 or rl == length
    nq = nb * ns
    grid = (nseq // nq, length // rl)
    kern = functools.partial(_gdn_kernel, nb=nb, ns=ns, rl=rl, sl=sl)
    branch, new_state = pl.pallas_call(
        kern,
        grid=grid,
        in_specs=[
            pl.BlockSpec((nq, rl, GDN_QKV), lambda b, n: (b, n, COL_QKV // GDN_QKV)),
            pl.BlockSpec((nq, rl, BRANCH), lambda b, n: (b, n, COL_Z // BRANCH)),
            pl.BlockSpec((nq, rl, LANES), lambda b, n: (b, n, 0)),
            pl.BlockSpec((1, nq, GDN_CONV - 1, GDN_QKV), lambda b, n: (layer_state, b, 0, 0)),
            pl.BlockSpec((1, nq, GDN_HEADS, 128, 128), lambda b, n: (layer_state, b, 0, 0, 0)),
            pl.BlockSpec((1, GDN_CONV, GDN_QKV), lambda b, n: (layer, 0, 0)),
            pl.BlockSpec((1, 2, LANES), lambda b, n: (layer, 0, 0)),
            pl.BlockSpec((1, 1, LANES), lambda b, n: (layer, 0, 0)),
        ],
        out_specs=[
            pl.BlockSpec((nb, ns * rl, BRANCH), lambda b, n: (b, n, 0)),
            pl.BlockSpec((nq, GDN_HEADS, 128, 128), lambda b, n: (b, 0, 0, 0)),
        ],
        out_shape=[jax.ShapeDtypeStruct((nseq // ns, ns * length, BRANCH), BF16),
                   jax.ShapeDtypeStruct((nseq, GDN_HEADS, 128, 128), F32)],
        scratch_shapes=[pltpu.VMEM((nq, SUBLANES + rl, GDN_QKV), F32)],
        compiler_params=_cparams("parallel", "arbitrary"),
        name="gated_deltanet",
    )(h3, h3, ba3, conv_state, s0, cw, gp, nw)
    return branch.reshape(nseq * length, BRANCH), new_state


_RET_LOG_GAMMA = tuple(math.log(1.0 - 2.0 ** (-5.0 - h)) for h in range(RET_HEADS))


def _per_head(idx, vals):
    out = jnp.full(idx.shape, vals[3], F32)
    for h in (2, 1, 0):
        out = jnp.where(idx < (h + 1) * RET_DK, vals[h], out)
    return out


def _ret_kernel(hb_ref, cos_ref, sin_ref, s0_ref, o_ref, s_ref, *, ns, sl):
    c = ns * sl
    n = pl.program_id(1)

    @pl.when(n == 0)
    def _():
        s_ref[...] = s0_ref[0]

    hb = hb_ref[...].reshape(c, 2048)
    cos = cos_ref[...]
    sin = sin_ref[...]
    rq = (hb[:, 0:256] * cos + hb[:, 512:768] * sin) * (RET_DK ** -0.5)
    rk = hb[:, 256:512] * cos + hb[:, 768:1024] * sin
    v = hb[:, 1024:1536]
    g = hb[:, 1536:2048]

    r, t, seq = _seg_ids(c, sl)
    tf = t.astype(F32)
    lane = lax.broadcasted_iota(jnp.int32, (1, 256), 1)
    lgl = _per_head(lane, _RET_LOG_GAMMA)
    q_dec = rq * jnp.exp(lgl * (tf + 1.0))
    k_dec = rk * jnp.exp(lgl * (sl - 1.0 - tf))
    k_dec_t = k_dec.T

    rr = lax.broadcasted_iota(jnp.int32, (c, c), 0)
    cc = lax.broadcasted_iota(jnp.int32, (c, c), 1)
    sh = int(math.log2(sl))
    causal = jnp.logical_and((rr >> sh) == (cc >> sh), rr >= cc)
    rel = jnp.maximum(rr - cc, 0).astype(F32)
    cseq = lax.broadcasted_iota(jnp.int32, (1, c), 1) >> sh
    rowi = lax.broadcasted_iota(jnp.int32, (256, 1), 0)
    cd_rows = jnp.exp(_per_head(rowi, _RET_LOG_GAMMA) * float(sl))

    cross = []
    for si in range(ns):
        rows = slice(si * sl, (si + 1) * sl)
        st = s_ref[si]
        parts = []
        for h in range(RET_HEADS):
            mh = jnp.logical_and(lane >= h * RET_DK, lane < (h + 1) * RET_DK)
            parts.append(_bdot(jnp.where(mh, q_dec[rows], 0.0), st))
        cross.append(parts)
        kd = k_dec_t if ns == 1 else jnp.where(cseq == si, k_dec_t, 0.0)
        res = _bdot(kd, v)
        upd = jnp.zeros((256, RET_DV), F32)
        for h in range(RET_HEADS):
            rm = jnp.logical_and(rowi >= h * RET_DK, rowi < (h + 1) * RET_DK)
            upd = upd + jnp.where(rm, res[:, h * 128:(h + 1) * 128], 0.0)
        s_ref[si] = st * cd_rows + upd

    heads = range(RET_HEADS)
    mhs = [jnp.logical_and(lane >= h * RET_DK, lane < (h + 1) * RET_DK) for h in heads]
    scs = [_bdot_nt(jnp.where(mhs[h], rq, 0.0), rk)
           * jnp.where(causal, jnp.exp(_RET_LOG_GAMMA[h] * rel), 0.0) for h in heads]
    os_ = [_bdot(scs[h], v[:, h * 128:(h + 1) * 128]) for h in heads]
    outs = []
    for h in heads:
        oc = cross[0][h] if ns == 1 else jnp.concatenate([cross[si][h] for si in range(ns)], axis=0)
        o = os_[h] + oc
        mu = jnp.mean(o, -1, keepdims=True)
        oc2 = o - mu
        o = oc2 * lax.rsqrt(jnp.mean(oc2 * oc2, -1, keepdims=True) + LN_EPS)
        outs.append(_silu(g[:, h * 128:(h + 1) * 128]) * o)
    o_ref[...] = jnp.concatenate(outs, axis=1).astype(o_ref.dtype)


def _retention(h3, cos_tab, sin_tab, s0, layer_state, ns, sl):
    nseq, length, _ = h3.shape
    c = ns * sl
    grid = (nseq // ns, length // sl)
    kern = functools.partial(_ret_kernel, ns=ns, sl=sl)
    return pl.pallas_call(
        kern,
        grid=grid,
        in_specs=[
            pl.BlockSpec((ns, sl, 2048), lambda b, n: (b, n, COL_RET // 2048)),
            pl.BlockSpec((c, 256), lambda b, n: (n, 0)),
            pl.BlockSpec((c, 256), lambda b, n: (n, 0)),
            pl.BlockSpec((1, ns, 256, RET_DV), lambda b, n: (layer_state, b, 0, 0)),
        ],
        out_specs=[
            _branch_out_spec(ns, sl, grid[1]),
            pl.BlockSpec((ns, 256, RET_DV), lambda b, n: (b, 0, 0)),
        ],
        out_shape=[jax.ShapeDtypeStruct((nseq * length, BRANCH), BF16),
                   jax.ShapeDtypeStruct((nseq, 256, RET_DV), F32)],
        compiler_params=_cparams("parallel", "arbitrary"),
        name="retention",
    )(h3, cos_tab, sin_tab, s0)


def _pool_kernel(u_ref, st_ref, w_ref, sc_ref, o_ref, full_ref, *, ns, sl, pos0, carry):
    c = ns * sl
    n = pl.program_id(1)

    @pl.when(n == 0)
    def _():
        full_ref[:, 1:16, :] = st_ref[0]

    u3 = u_ref[...]
    full_ref[:, 16:16 + sl, :] = u3
    tpos = lax.broadcasted_iota(jnp.int32, (1, sl, 1), 1) + n * sl
    n_avail = (tpos + (pos0 + 1)).astype(F32)
    outs = []
    for gi, w in enumerate(POOL_WINDOWS):
        cols = slice(gi * 128, (gi + 1) * 128)
        ug = u3[:, :, cols]
        acc = ug
        for j in range(1, w):
            acc = acc + full_ref[:, 16 - j:16 - j + sl, cols]
        pooled = acc / jnp.minimum(n_avail, float(w)) - ug
        mixed = _bdot(pooled.reshape(c, 128), w_ref[0, gi])
        outs.append(mixed * sc_ref[0, :, cols])
    if carry:
        full_ref[:, 1:16, :] = full_ref[:, sl + 1:sl + 16, :]
    o_ref[...] = jnp.concatenate(outs, axis=1).astype(o_ref.dtype)


def _pool(h3, state, layer_state, w_pool, pool_scale, layer, ns, sl, pos0):
    nseq, length, _ = h3.shape
    grid = (nseq // ns, length // sl)
    kern = functools.partial(_pool_kernel, ns=ns, sl=sl, pos0=pos0, carry=grid[1] > 1)
    return pl.pallas_call(
        kern,
        grid=grid,
        in_specs=[
            pl.BlockSpec((ns, sl, BRANCH), lambda b, n: (b, n, COL_POOL // BRANCH)),
            pl.BlockSpec((1, ns, POOL_BUF, BRANCH), lambda b, n: (layer_state, b, 0, 0)),
            pl.BlockSpec((1, 4, 128, 128), lambda b, n: (layer, 0, 0, 0)),
            pl.BlockSpec((1, 1, BRANCH), lambda b, n: (layer, 0, 0)),
        ],
        out_specs=_branch_out_spec(ns, sl, grid[1]),
        out_shape=jax.ShapeDtypeStruct((nseq * length, BRANCH), BF16),
        scratch_shapes=[pltpu.VMEM((ns, 16 + sl, BRANCH), F32)],
        compiler_params=_cparams("parallel", "arbitrary"),
        name="multi_pool",
    )(h3, state, w_pool, pool_scale)


S5_SEQS = SUBLANES
S5_SLABS = 2 * S5_CH // LANES


def _s5_pitch(tl):
    return tl + 4


def _s5_kernel(u_ref, hr_ref, hi_ref, bd_ref, cd_ref, ar_ref, ai_ref, d_ref, glu_ref,
               o_ref, or_ref, oi_ref, bu_ref, h_ref, arb_ref, aib_ref, up_ref, *, tl):
    pitch = _s5_pitch(tl)
    n = pl.program_id(1)

    @pl.when(n == 0)
    def _():
        hr = hr_ref[0]
        hi = hi_ref[0]
        for kb in range(2):
            for j in range(8):
                src = slice(kb * 1024 + j * 128, kb * 1024 + (j + 1) * 128)
                h_ref[kb * 16 + j] = hr[:, src]
                h_ref[kb * 16 + 8 + j] = hi[:, src]
        for kb in range(2):
            for j in range(8):
                src = slice(kb * 1024 + j * 128, kb * 1024 + (j + 1) * 128)
                arb_ref[kb * 8 + j] = jnp.broadcast_to(ar_ref[0, :, src], (S5_SEQS, LANES))
                aib_ref[kb * 8 + j] = jnp.broadcast_to(ai_ref[0, :, src], (S5_SEQS, LANES))

        up_ref[...] = jnp.zeros_like(up_ref)

    for s in range(S5_SEQS):
        up_ref[s * pitch:s * pitch + tl, :] = u_ref[s]
    up = up_ref[...]
    upb = up.astype(BF16)
    for kb in range(2):
        bu = jnp.dot(upb[:, kb * 256:(kb + 1) * 256], bd_ref[0, kb],
                     preferred_element_type=F32)
        for j in range(16):
            bu_ref[kb * 16 + j] = bu[:, j * 128:(j + 1) * 128]

    def step(l, carry):
        new = []
        for kb in range(2):
            for j in range(8):
                re = carry[kb * 16 + j]
                im = carry[kb * 16 + 8 + j]
                ar = arb_ref[kb * 8 + j]
                ai = aib_ref[kb * 8 + j]
                idx = pl.ds(l, S5_SEQS, stride=pitch)
                b_re = bu_ref[kb * 16 + j, idx, :]
                b_im = bu_ref[kb * 16 + 8 + j, idx, :]
                n_re = ar * re - ai * im + b_re
                n_im = ar * im + ai * re + b_im
                bu_ref[kb * 16 + j, idx, :] = n_re
                bu_ref[kb * 16 + 8 + j, idx, :] = n_im
                new.append((kb * 16 + j, n_re))
                new.append((kb * 16 + 8 + j, n_im))
        new.sort(key=lambda p: p[0])
        return tuple(p[1] for p in new)

    h0 = tuple(h_ref[i] for i in range(S5_SLABS))
    hf = lax.fori_loop(0, tl, step, h0, unroll=2)
    for i in range(S5_SLABS):
        h_ref[i] = hf[i]

    @pl.when(n == pl.num_programs(1) - 1)
    def _():
        for kb in range(2):
            for j in range(8):
                dst = slice(kb * 1024 + j * 128, kb * 1024 + (j + 1) * 128)
                or_ref[:, dst] = hf[kb * 16 + j]
                oi_ref[:, dst] = hf[kb * 16 + 8 + j]

    ys = []
    for kb in range(2):
        hs = jnp.concatenate([bu_ref[kb * 16 + j].astype(BF16) for j in range(16)], axis=1)
        ys.append(jnp.dot(hs, cd_ref[0, kb], preferred_element_type=F32))
    y = jnp.concatenate(ys, axis=1) + d_ref[0] * up
    act = jax.nn.gelu(y)
    out = act * jax.nn.sigmoid(_bdot(act, glu_ref[0]))
    for s in range(S5_SEQS):
        o_ref[s] = out[s * pitch:s * pitch + tl, :]


def _s5(h3, h_re, h_im, layer_state, bd, cd, ar, ai, d_skip, w_glu, layer, tl):
    nseq, length, _ = h3.shape
    grid = (nseq // S5_SEQS, length // tl)
    kern = functools.partial(_s5_kernel, tl=tl)
    rows = S5_SEQS * _s5_pitch(tl)
    return pl.pallas_call(
        kern,
        grid=grid,
        in_specs=[
            pl.BlockSpec((S5_SEQS, tl, BRANCH), lambda b, n: (b, n, COL_S5 // BRANCH)),
            pl.BlockSpec((1, S5_SEQS, S5_CH), lambda b, n: (layer_state, b, 0)),
            pl.BlockSpec((1, S5_SEQS, S5_CH), lambda b, n: (layer_state, b, 0)),
            pl.BlockSpec((1, 2, 256, 2048), lambda b, n: (layer, 0, 0, 0)),
            pl.BlockSpec((1, 2, 2048, 256), lambda b, n: (layer, 0, 0, 0)),
            pl.BlockSpec((1, 1, S5_CH), lambda b, n: (layer, 0, 0)),
            pl.BlockSpec((1, 1, S5_CH), lambda b, n: (layer, 0, 0)),
            pl.BlockSpec((1, 1, BRANCH), lambda b, n: (layer, 0, 0)),
            pl.BlockSpec((1, BRANCH, BRANCH), lambda b, n: (layer, 0, 0)),
        ],
        out_specs=[
            pl.BlockSpec((S5_SEQS, tl, BRANCH), lambda b, n: (b, n, 0)),
            pl.BlockSpec((S5_SEQS, S5_CH), lambda b, n: (b, 0)),
            pl.BlockSpec((S5_SEQS, S5_CH), lambda b, n: (b, 0)),
        ],
        out_shape=[jax.ShapeDtypeStruct((nseq, length, BRANCH), F32),
                   jax.ShapeDtypeStruct((nseq, S5_CH), F32),
                   jax.ShapeDtypeStruct((nseq, S5_CH), F32)],
        scratch_shapes=[pltpu.VMEM((S5_SLABS, rows, LANES), F32),
                        pltpu.VMEM((S5_SLABS, S5_SEQS, LANES), F32),
                        pltpu.VMEM((S5_SLABS // 2, S5_SEQS, LANES), F32),
                        pltpu.VMEM((S5_SLABS // 2, S5_SEQS, LANES), F32),
                        pltpu.VMEM((rows, BRANCH), F32)],
        compiler_params=_cparams("parallel", "arbitrary"),
        name="s5_ssm",
    )(h3, h_re, h_im, bd, cd, ar, ai, d_skip, w_glu)


def _merge_kernel(x_ref, ba_ref, bb_ref, bc_ref, bd_ref, g_ref, wb_ref, wo_ref, wq_ref, ln_ref,
                  x1_ref, q_ref):
    mixed = None
    for i, br in enumerate((ba_ref, bb_ref, bc_ref, bd_ref)):
        proj = jnp.dot(br[...].astype(BF16), wb_ref[0, i], preferred_element_type=F32)
        term = jax.nn.sigmoid(g_ref[:, i * D_MODEL:(i + 1) * D_MODEL]) * proj
        mixed = term if mixed is None else mixed + term
    y = jnp.dot(mixed.astype(BF16), wo_ref[0], preferred_element_type=F32)
    x1 = _layer_norm(DN_ALPHA * x_ref[...] + y, ln_ref[0, 0:1, :], ln_ref[0, 1:2, :])
    x1_ref[...] = x1
    q_ref[...] = jnp.dot(x1.astype(BF16), wq_ref[0], preferred_element_type=F32)


def _merge(x, h, brs, wb, wo, wq, ln, layer, tm):
    m = x.shape[0]
    tok = lambda w: pl.BlockSpec((tm, w), lambda i: (i, 0))
    return pl.pallas_call(
        _merge_kernel,
        grid=(m // tm,),
        in_specs=[tok(D_MODEL), tok(BRANCH), tok(BRANCH), tok(BRANCH), tok(BRANCH),
                  pl.BlockSpec((tm, 4096), lambda i: (i, COL_GATE // 4096)),
                  pl.BlockSpec((1, N_BRANCH, BRANCH, D_MODEL), lambda i: (layer, 0, 0, 0)),
                  pl.BlockSpec((1, D_MODEL, D_MODEL), lambda i: (layer, 0, 0)),
                  pl.BlockSpec((1, D_MODEL, D_MODEL), lambda i: (layer, 0, 0)),
                  pl.BlockSpec((1, 2, D_MODEL), lambda i: (layer, 0, 0))],
        out_specs=[tok(D_MODEL), tok(D_MODEL)],
        out_shape=[jax.ShapeDtypeStruct((m, D_MODEL), F32)] * 2,
        compiler_params=_cparams("parallel"),
        name="branch_merge",
    )(x, *brs, h, wb, wo, wq, ln)


def _attn_kernel(q_ref, k_ref, v_ref, o_ref, *, heads_split):
    q = q_ref[0]
    heads = range(XA_HEADS)
    cols = [slice(h * XA_DH, (h + 1) * XA_DH) for h in heads]
    if heads_split:
        tq = q.shape[0]
        k2 = k_ref[0, 0].reshape(MEM_LEN * XA_HEADS, XA_DH)
        v2 = v_ref[0, 0].reshape(MEM_LEN * XA_HEADS, XA_DH)
        qs = jnp.concatenate([q[:, cols[h]] for h in heads], axis=0)
        colh = lax.broadcasted_iota(jnp.int32, (1, MEM_LEN * XA_HEADS), 1) & (XA_HEADS - 1)
        rowh = lax.broadcasted_iota(jnp.int32, (XA_HEADS * tq, 1), 0) >> int(math.log2(tq))
        s = jnp.where(colh == rowh, _bdot_nt(qs, k2) * (XA_DH ** -0.5), -1e30)
        e = jnp.exp(s - jnp.max(s, -1, keepdims=True))
        o = _bdot(e / jnp.sum(e, -1, keepdims=True), v2)
        o_ref[0] = jnp.concatenate([o[h * tq:(h + 1) * tq] for h in heads], axis=1)
    else:
        ss = [_bdot_nt(q[:, cols[h]], k_ref[0, 0, :, cols[h]]) * (XA_DH ** -0.5) for h in heads]
        es = [jnp.exp(s - jnp.max(s, -1, keepdims=True)) for s in ss]
        ps = [e / jnp.sum(e, -1, keepdims=True) for e in es]
        o_ref[0] = jnp.concatenate([_bdot(ps[h], v_ref[0, 0, :, cols[h]]) for h in heads], axis=1)


def _attention(q3, mem_k, mem_v, layer_mem, tq):
    nseq, length, _ = q3.shape
    heads_split = mem_k.ndim == 5
    if heads_split:
        mem = pl.BlockSpec((1, 1, MEM_LEN, XA_HEADS, XA_DH), lambda b, i: (layer_mem, b, 0, 0, 0))
    else:
        mem = pl.BlockSpec((1, 1, MEM_LEN, D_MODEL), lambda b, i: (layer_mem, b, 0, 0))
    return pl.pallas_call(
        functools.partial(_attn_kernel, heads_split=heads_split),
        grid=(nseq, length // tq),
        in_specs=[pl.BlockSpec((1, tq, D_MODEL), lambda b, i: (b, i, 0)), mem, mem],
        out_specs=pl.BlockSpec((1, tq, D_MODEL), lambda b, i: (b, i, 0)),
        out_shape=jax.ShapeDtypeStruct((nseq, length, D_MODEL), F32),
        compiler_params=_cparams("parallel", "arbitrary"),
        name="memory_attention",
    )(q3, mem_k, mem_v)


def _post_kernel(x1_ref, o_ref, wo_ref, ln_ref, wr_ref, br_ref, x2_ref, comb_ref):
    y = jnp.dot(o_ref[...].astype(BF16), wo_ref[0], preferred_element_type=F32)
    x2 = _layer_norm(DN_ALPHA * x1_ref[...] + y, ln_ref[0, 0:1, :], ln_ref[0, 1:2, :])
    x2_ref[...] = x2
    tm = x2.shape[0]
    logits = _bdot_nt(wr_ref[...], x2) + br_ref[:, 0:1]
    e = jnp.exp(logits - jnp.max(logits, 0, keepdims=True))
    p = e / jnp.sum(e, 0, keepdims=True)
    rowi = lax.broadcasted_iota(jnp.int32, (N_EXPERTS, 1), 0)
    rowf = rowi.astype(F32)
    grp = rowi >> 2
    best = None
    for gidx in range(N_GROUPS):
        gm = jnp.max(p[gidx * EXPERTS_PER_GROUP:(gidx + 1) * EXPERTS_PER_GROUP], 0, keepdims=True)
        if best is None:
            best, gi = gm, jnp.zeros(gm.shape, jnp.int32)
        else:
            upd = gm > best
            gi = jnp.where(upd, gidx, gi)
            best = jnp.where(upd, gm, best)
    cand = jnp.where(grp == gi, p, -1.0)
    m1 = jnp.max(cand, 0, keepdims=True)
    e1 = jnp.min(jnp.where(cand == m1, rowf, 1e9), 0, keepdims=True)
    cand2 = jnp.where(rowf == e1, -2.0, cand)
    m2 = jnp.max(cand2, 0, keepdims=True)
    e2 = jnp.min(jnp.where(cand2 == m2, rowf, 1e9), 0, keepdims=True)
    tot = m1 + m2
    comb_t = jnp.where(rowf == e1, m1 / tot, 0.0) + jnp.where(rowf == e2, m2 / tot, 0.0)
    sub = lax.broadcasted_iota(jnp.int32, (SUBLANES, 1), 0)
    grp_rows = jnp.where(sub == 0, gi.astype(F32), 0.0)
    rest = jnp.zeros((LANES - N_EXPERTS - SUBLANES, tm), F32)
    comb_ref[...] = jnp.concatenate([comb_t, grp_rows, rest], axis=0).T


def _post(x1, o, wo, ln, wr, br, layer, tm):
    m = x1.shape[0]
    tok = lambda w: pl.BlockSpec((tm, w), lambda i: (i, 0))
    return pl.pallas_call(
        _post_kernel,
        grid=(m // tm,),
        in_specs=[tok(D_MODEL), tok(D_MODEL),
                  pl.BlockSpec((1, D_MODEL, D_MODEL), lambda i: (layer, 0, 0)),
                  pl.BlockSpec((1, 2, D_MODEL), lambda i: (layer, 0, 0)),
                  pl.BlockSpec((N_EXPERTS, D_MODEL), lambda i: (0, 0)),
                  pl.BlockSpec((N_EXPERTS, LANES), lambda i: (0, 0))],
        out_specs=[tok(D_MODEL), tok(LANES)],
        out_shape=[jax.ShapeDtypeStruct((m, D_MODEL), F32), jax.ShapeDtypeStruct((m, LANES), F32)],
        compiler_params=_cparams("parallel"),
        name="attn_out_router",
    )(x1, o, wo, ln, wr, br)


def _moe_kernel(x_ref, comb_ref, w1_ref, w3_ref, w2_ref, ln_ref, o_ref, acc_ref, xb_ref):
    e = pl.program_id(1)

    @pl.when(e == 0)
    def _():
        acc_ref[...] = jnp.zeros_like(acc_ref)
        xb_ref[...] = x_ref[...].astype(BF16)

    comb = comb_ref[...]
    lane = lax.broadcasted_iota(jnp.int32, comb.shape, 1)
    ce = jnp.sum(jnp.where(lane == e, comb, 0.0), -1, keepdims=True)
    xb = xb_ref[...]
    h1 = jnp.dot(xb, w1_ref[0, 0].astype(BF16), preferred_element_type=F32)
    h3 = jnp.dot(xb, w3_ref[0, 0].astype(BF16), preferred_element_type=F32)
    hid = _silu(h1) * h3 * ce
    acc_ref[...] += jnp.dot(hid.astype(BF16), w2_ref[0, 0].astype(BF16),
                            preferred_element_type=F32)

    @pl.when(e == pl.num_programs(1) - 1)
    def _():
        o_ref[...] = _layer_norm(DN_ALPHA * x_ref[...] + acc_ref[...],
                                 ln_ref[0, 0:1, :], ln_ref[0, 1:2, :])


def _moe(x2, comb, w1, w3, w2, ln, layer, tm):
    m = x2.shape[0]
    w_up = pl.BlockSpec((1, 1, D_MODEL, D_EXPERT), lambda i, e: (layer, e, 0, 0))
    return pl.pallas_call(
        _moe_kernel,
        grid=(m // tm, N_EXPERTS),
        in_specs=[pl.BlockSpec((tm, D_MODEL), lambda i, e: (i, 0)),
                  pl.BlockSpec((tm, LANES), lambda i, e: (i, 0)),
                  w_up, w_up,
                  pl.BlockSpec((1, 1, D_EXPERT, D_MODEL), lambda i, e: (layer, e, 0, 0)),
                  pl.BlockSpec((1, 2, D_MODEL), lambda i, e: (layer, 0, 0))],
        out_specs=pl.BlockSpec((tm, D_MODEL), lambda i, e: (i, 0)),
        out_shape=jax.ShapeDtypeStruct((m, D_MODEL), F32),
        scratch_shapes=[pltpu.VMEM((tm, D_MODEL), F32), pltpu.VMEM((tm, D_MODEL), BF16)],
        compiler_params=_cparams("parallel", "arbitrary"),
        name="moe_experts",
    )(x2, comb, w1, w3, w2, ln)


N_GROUPS = N_EXPERTS // EXPERTS_PER_GROUP
GROUP_LANE = N_EXPERTS


def _bucket_cap(tm):
    return tm // N_GROUPS + tm // 16


def _split_bf16(x):
    hi = x.astype(BF16)
    return hi, (x - hi.astype(F32)).astype(BF16)


def _moe_sort_kernel(x_ref, comb_ref, xs_ref, gs_ref, dest_ref, cnt_ref, *, cap):
    tm = x_ref.shape[0]
    slots = N_GROUPS * cap
    comb = comb_ref[...]
    lanef = lax.broadcasted_iota(jnp.int32, comb.shape, 1).astype(F32)
    gid = comb[:, GROUP_LANE:GROUP_LANE + 1]
    onehot = jnp.where(lanef == gid, 1.0, 0.0)
    rr = lax.broadcasted_iota(jnp.int32, (tm, tm), 0)
    cc = lax.broadcasted_iota(jnp.int32, (tm, tm), 1)
    tril = jnp.where(rr >= cc, 1.0, 0.0).astype(BF16)
    cs = jnp.dot(tril, onehot.astype(BF16), preferred_element_type=F32)
    rank = jnp.sum(onehot * cs, -1, keepdims=True) - 1.0
    dest = gid * float(cap) + rank
    cnt_ref[0] = jnp.broadcast_to(cs[tm - 1:tm, :], (SUBLANES, LANES))
    dest_b = jnp.broadcast_to(dest, (tm, LANES))
    dest_ref[...] = dest_b
    dest_row = dest_b.T[0:1, :]
    srow = lax.broadcasted_iota(jnp.int32, (slots, 1), 0).astype(F32)
    perm = jnp.where(srow == dest_row, 1.0, 0.0).astype(BF16)
    xs = jnp.dot(perm, x_ref[...].astype(BF16), preferred_element_type=F32).astype(xs_ref.dtype)
    hi, lo = _split_bf16(comb)
    gs = (jnp.dot(perm, hi, preferred_element_type=F32)
          + jnp.dot(perm, lo, preferred_element_type=F32))
    for g in range(N_GROUPS):
        xs_ref[g, 0] = xs[g * cap:(g + 1) * cap, :]
        gs_ref[g, 0] = gs[g * cap:(g + 1) * cap, :]


def _moe_sort(x2, comb, tm):
    m = x2.shape[0]
    cap = _bucket_cap(tm)
    nt = m // tm
    return pl.pallas_call(
        functools.partial(_moe_sort_kernel, cap=cap),
        grid=(nt,),
        in_specs=[pl.BlockSpec((tm, D_MODEL), lambda i: (i, 0)),
                  pl.BlockSpec((tm, LANES), lambda i: (i, 0))],
        out_specs=[pl.BlockSpec((N_GROUPS, 1, cap, D_MODEL), lambda i: (0, i, 0, 0)),
                   pl.BlockSpec((N_GROUPS, 1, cap, LANES), lambda i: (0, i, 0, 0)),
                   pl.BlockSpec((tm, LANES), lambda i: (i, 0)),
                   pl.BlockSpec((1, SUBLANES, LANES), lambda i: (i, 0, 0))],
        out_shape=[jax.ShapeDtypeStruct((N_GROUPS, nt, cap, D_MODEL), BF16),
                   jax.ShapeDtypeStruct((N_GROUPS, nt, cap, LANES), F32),
                   jax.ShapeDtypeStruct((m, LANES), F32),
                   jax.ShapeDtypeStruct((nt, SUBLANES, LANES), F32)],
        compiler_params=_cparams("parallel"),
        name="moe_bucket_sort",
    )(x2, comb)


def _moe_bucket_kernel(xs_ref, gs_ref, w1_ref, w3_ref, w2_ref, o_ref, acc_ref):
    g = pl.program_id(0)
    j = pl.program_id(2)
    rows = acc_ref.shape[0]

    @pl.when(j == 0)
    def _():
        acc_ref[...] = jnp.zeros_like(acc_ref)

    gates = gs_ref[0].reshape(rows, LANES)
    lane = lax.broadcasted_iota(jnp.int32, gates.shape, 1)
    ce = jnp.sum(jnp.where(lane == g * EXPERTS_PER_GROUP + j, gates, 0.0), -1, keepdims=True)
    xb = xs_ref[0].reshape(rows, D_MODEL)
    h1 = jnp.dot(xb, w1_ref[0, 0].astype(BF16), preferred_element_type=F32)
    h3 = jnp.dot(xb, w3_ref[0, 0].astype(BF16), preferred_element_type=F32)
    hid = _silu(h1) * h3 * ce
    acc_ref[...] += jnp.dot(hid.astype(BF16), w2_ref[0, 0].astype(BF16),
                            preferred_element_type=F32)

    @pl.when(j == pl.num_programs(2) - 1)
    def _():
        o_ref[0] = acc_ref[...].reshape(o_ref.shape[1:])


def _moe_buckets(xs, gs, w1, w3, w2, layer, tpq):
    _, nt, cap, _ = xs.shape

    def bucket(width):
        return pl.BlockSpec((1, tpq, cap, width), lambda g, q, j: (g, q, 0, 0))

    def expert(rows, cols):
        return pl.BlockSpec((1, 1, rows, cols),
                            lambda g, q, j: (layer, g * EXPERTS_PER_GROUP + j, 0, 0))

    return pl.pallas_call(
        _moe_bucket_kernel,
        grid=(N_GROUPS, nt // tpq, EXPERTS_PER_GROUP),
        in_specs=[bucket(D_MODEL), bucket(LANES),
                  expert(D_MODEL, D_EXPERT), expert(D_MODEL, D_EXPERT), expert(D_EXPERT, D_MODEL)],
        out_specs=bucket(D_MODEL),
        out_shape=jax.ShapeDtypeStruct((N_GROUPS, nt, cap, D_MODEL), F32),
        scratch_shapes=[pltpu.VMEM((tpq * cap, D_MODEL), F32)],
        compiler_params=_cparams("parallel", "parallel", "arbitrary"),
        name="moe_bucket_experts",
    )(xs, gs, w1, w3, w2)


def _moe_unsort_kernel(x_ref, dest_ref, ys_ref, ln_ref, o_ref):
    _, _, cap, _ = ys_ref.shape
    slots = N_GROUPS * cap
    dest = dest_ref[:, 0:1]
    slot = lax.broadcasted_iota(jnp.int32, (1, slots), 1).astype(F32)
    perm_t = jnp.where(dest == slot, 1.0, 0.0).astype(BF16)
    hi, lo = _split_bf16(ys_ref[...].reshape(slots, D_MODEL))
    y = (jnp.dot(perm_t, hi, preferred_element_type=F32)
         + jnp.dot(perm_t, lo, preferred_element_type=F32))
    o_ref[...] = _layer_norm(DN_ALPHA * x_ref[...] + y, ln_ref[0, 0:1, :], ln_ref[0, 1:2, :])


def _moe_unsort(x2, dest, ys, ln, layer, tm):
    m = x2.shape[0]
    _, _, cap, _ = ys.shape
    return pl.pallas_call(
        _moe_unsort_kernel,
        grid=(m // tm,),
        in_specs=[pl.BlockSpec((tm, D_MODEL), lambda i: (i, 0)),
                  pl.BlockSpec((tm, LANES), lambda i: (i, 0)),
                  pl.BlockSpec((N_GROUPS, 1, cap, D_MODEL), lambda i: (0, i, 0, 0)),
                  pl.BlockSpec((1, 2, D_MODEL), lambda i: (layer, 0, 0))],
        out_specs=pl.BlockSpec((tm, D_MODEL), lambda i: (i, 0)),
        out_shape=jax.ShapeDtypeStruct((m, D_MODEL), F32),
        compiler_params=_cparams("parallel"),
        name="moe_unsort_norm",
    )(x2, dest, ys, ln)


def _moe_bucketed(x2, comb, w1, w3, w2, ln, layer, tm, tpq):
    xs, gs, dest, cnt = _moe_sort(x2, comb, tm)
    overflow = jnp.max(cnt[:, 0, :N_GROUPS]) > _bucket_cap(tm)

    def bucketed():
        return _moe_unsort(x2, dest, _moe_buckets(xs, gs, w1, w3, w2, layer, tpq), ln, layer, tm)

    def dense():
        return _moe(x2, comb, w1, w3, w2, ln, layer, tm)

    return lax.cond(overflow, dense, bucketed)


_W_IN_OFFS = np.cumsum((0, GDN_QKV, 512, 4, 4, 256, 256, 512, 512, 512, 512, 4096))


def _w_in_kernel(wt_ref, o_ref, ba_ref):
    o = _W_IN_OFFS
    tk = wt_ref.shape[2]

    def seg_t(i):
        return wt_ref[0, int(o[i]):int(o[i + 1]), :]

    def swap_pairs_t(x):
        row = lax.broadcasted_iota(jnp.int32, x.shape, 0)
        n = x.shape[0]
        return jnp.where((row & 1) == 0, pltpu.roll(x, n - 1, 0), pltpu.roll(x, 1, 0))

    rq, rk = seg_t(4), seg_t(5)
    pieces = [seg_t(0), seg_t(1), rq, rk, swap_pairs_t(rq), swap_pairs_t(rk),
              seg_t(6), seg_t(7), seg_t(10), seg_t(8), seg_t(9)]
    col = 0
    for p in pieces:
        n = p.shape[0]
        o_ref[0, :, col:col + n] = p.T.astype(BF16)
        col += n
    ba_t = jnp.concatenate([wt_ref[0, int(o[2]):int(o[4]), :], jnp.zeros((LANES - 8, tk), F32)],
                           axis=0)
    ba_ref[0] = ba_t.T.astype(BF16)


def _prep_w_in_pallas(w_in, tk=256):
    depth, k, n = w_in.shape
    w_t = jnp.swapaxes(w_in, 1, 2)
    return pl.pallas_call(
        _w_in_kernel,
        grid=(depth, k // tk),
        in_specs=[pl.BlockSpec((1, n, tk), lambda l, i: (l, 0, i))],
        out_specs=[pl.BlockSpec((1, tk, H_COLS), lambda l, i: (l, i, 0)),
                   pl.BlockSpec((1, tk, LANES), lambda l, i: (l, i, 0))],
        out_shape=[jax.ShapeDtypeStruct((depth, k, H_COLS), BF16),
                   jax.ShapeDtypeStruct((depth, k, LANES), BF16)],
        compiler_params=_cparams("parallel", "parallel"),
        name="w_in_relayout",
    )(w_t)


def _rope_tables(pos):
    inv_freq = 1.0 / (ROPE_BASE ** jnp.linspace(0.0, 1.0, RET_DK // 2, dtype=F32))
    ang = pos.astype(F32)[:, None] * inv_freq
    cos = jnp.repeat(jnp.cos(ang), 2, axis=1)
    sin = jnp.repeat(jnp.sin(ang), 2, axis=1)
    sign = jnp.tile(jnp.array([-1.0, 1.0], F32), RET_DK // 2)
    return jnp.tile(cos, (1, RET_HEADS)), jnp.tile(sin * sign, (1, RET_HEADS))


def _prep_s5(a_re, a_im, log_dt, b_re, b_im, c_re, c_im):
    dt = jnp.exp(log_dt.astype(F32))[..., None]
    mag = jnp.exp(a_re * dt)
    ab_re, ab_im = mag * jnp.cos(a_im * dt), mag * jnp.sin(a_im * dt)
    den = a_re * a_re + a_im * a_im
    coef_re = ((ab_re - 1.0) * a_re + ab_im * a_im) / den
    coef_im = (ab_im * a_re - (ab_re - 1.0) * a_im) / den
    bb_re = coef_re[..., None] * b_re - coef_im[..., None] * b_im
    bb_im = coef_re[..., None] * b_im + coef_im[..., None] * b_re
    eye = jnp.eye(16, dtype=F32)

    def pack_b(bb):
        x = bb.reshape(DEPTH, 2, 16, S5_STATE, S5_GROUP)
        return jnp.einsum('dkgpc,gh->dkgchp', x, eye).reshape(DEPTH, 2, 256, 1024)

    def pack_c(cm):
        x = cm.reshape(DEPTH, 2, 16, S5_GROUP, S5_STATE)
        return jnp.einsum('dkgcp,gh->dkgphc', x, eye).reshape(DEPTH, 2, 1024, 256)

    bd = jnp.concatenate([pack_b(bb_re), pack_b(bb_im)], axis=-1).astype(BF16)
    cd = jnp.concatenate([pack_c(c_re.astype(F32)), -pack_c(c_im.astype(F32))], axis=-2).astype(BF16)
    ar = ab_re.reshape(DEPTH, 1, S5_CH)
    ai = ab_im.reshape(DEPTH, 1, S5_CH)
    return bd, cd, ar, ai


def kernel(x_prompt, x_sample, mem_prompt, state_gdn_conv, state_gdn, state_ret, state_pool,
           state_s5_re, state_s5_im, cache_mem_k, cache_mem_v, w_in, gdn_conv_w, gdn_a_log,
           gdn_dt_bias, gdn_norm_w, pool_w, pool_scale, s5_a_re, s5_a_im, s5_log_dt, s5_b_re,
           s5_b_im, s5_c_re, s5_c_im, s5_d, s5_w_glu, w_branch, w_out, xa_w_q, xa_w_k, xa_w_v,
           xa_w_o, ln_g, ln_b, w_router, b_router, moe_w1, moe_w3, moe_w2):
    bp, seq, _ = x_prompt.shape
    bs, dseq, _ = x_sample.shape
    past = 16384

    w_main, w_ba = _prep_w_in_pallas(w_in)
    gp = jnp.zeros((DEPTH, 2, LANES), F32)
    gp = gp.at[:, 0, 4:8].set(gdn_a_log.astype(F32)).at[:, 1, 4:8].set(gdn_dt_bias.astype(F32))
    nw = gdn_norm_w.astype(F32).reshape(DEPTH, 1, GDN_DK)
    cw = gdn_conv_w.astype(F32)
    pw = pool_w.astype(BF16)
    psc = pool_scale.astype(F32).reshape(DEPTH, 1, BRANCH)
    bd, cd, ar, ai = _prep_s5(s5_a_re.astype(F32), s5_a_im.astype(F32), s5_log_dt,
                              s5_b_re.astype(F32), s5_b_im.astype(F32), s5_c_re, s5_c_im)
    d_skip = s5_d.astype(F32).reshape(DEPTH, 1, BRANCH)
    glu = s5_w_glu.astype(BF16)
    wb = w_branch.astype(BF16)
    wo = w_out.astype(BF16)
    wq, wk, wv, wxo = (w.astype(BF16) for w in (xa_w_q, xa_w_k, xa_w_v, xa_w_o))
    ln = jnp.stack([ln_g.astype(F32), ln_b.astype(F32)], axis=2)
    ln1, ln2, ln3 = ln[:, 0], ln[:, 1], ln[:, 2]
    wr = w_router.T.astype(BF16)
    br = jnp.broadcast_to(b_router.astype(F32)[:, None], (N_EXPERTS, LANES))

    cos_p, sin_p = _rope_tables(jnp.arange(seq))
    cos_s, sin_s = _rope_tables(past + jnp.arange(dseq))
    ns_s = 16
    cos_s, sin_s = jnp.tile(cos_s, (ns_s, 1)), jnp.tile(sin_s, (ns_s, 1))

    zeros = lambda *s: jnp.zeros((1,) + s, F32)
    z_conv, z_gdn = zeros(bp, GDN_CONV - 1, GDN_QKV), zeros(bp, GDN_HEADS, 128, 128)
    z_ret, z_pool, z_s5 = zeros(bp, 256, RET_DV), zeros(bp, POOL_BUF, BRANCH), zeros(bp, S5_CH)
    st_ret = state_ret.reshape(DEPTH, bs, 256, RET_DV)
    st_s5r = state_s5_re.reshape(DEPTH, bs, S5_CH)
    st_s5i = state_s5_im.reshape(DEPTH, bs, S5_CH)
    mem2d = mem_prompt.reshape(bp * MEM_LEN, D_MODEL)
    ck, cv = cache_mem_k, cache_mem_v

    def block(x2d, nseq, length, layer, mem_k, mem_v, layer_mem, states, layer_state, cfg):
        conv0, gdn0, ret0, pool0, s5r0, s5i0 = states
        tm = cfg["tm"]
        h, hba = _in_proj(x2d, w_main, w_ba, layer, tm, 2304)
        h3 = h.reshape(nseq, length, H_COLS)
        ba3 = hba.reshape(nseq, length, LANES)
        br_a, new_gdn = _gdn(h3, ba3, conv0, gdn0, layer_state, cw, gp, nw, layer,
                             cfg["gdn_nb"], cfg["ns"], cfg["gdn_rl"], cfg["gdn_sl"])
        br_b, new_ret = _retention(h3, cfg["cos"], cfg["sin"], ret0, layer_state,
                                   cfg["ns"], cfg["ret_sl"])
        br_c = _pool(h3, pool0, layer_state, pw, psc, layer, cfg["ns"], cfg["pool_sl"], cfg["pos0"])
        br_d, new_re, new_im = _s5(h3, s5r0, s5i0, layer_state, bd, cd, ar, ai, d_skip, glu,
                                   layer, cfg["s5_tl"])
        brs = [br_a, br_b, br_c, br_d.reshape(nseq * length, BRANCH)]
        x1, q = _merge(x2d, h, brs, wb, wo, wq, ln1, layer, cfg["tm_merge"])
        o = _attention(q.reshape(nseq, length, D_MODEL), mem_k, mem_v, layer_mem, cfg["tq"])
        x2, comb = _post(x1, o.reshape(nseq * length, D_MODEL), wxo, ln2, wr, br, layer, tm)
        if cfg["moe_tpq"]:
            x3 = _moe_bucketed(x2, comb, moe_w1, moe_w3, moe_w2, ln3, layer, tm, cfg["moe_tpq"])
        else:
            x3 = _moe(x2, comb, moe_w1, moe_w3, moe_w2, ln3, layer, tm)
        new_conv = h3[:, length - (GDN_CONV - 1):, COL_QKV:COL_QKV + GDN_QKV]
        pool_u = h3[:, :, COL_POOL:COL_POOL + BRANCH]
        return x3, (new_conv, new_gdn, new_ret.reshape(nseq, RET_HEADS, RET_DK, RET_DV), pool_u,
                    new_re.reshape(nseq, S5_GROUPS, S5_STATE), new_im.reshape(nseq, S5_GROUPS, S5_STATE))

    cfg_p = dict(tm=1024, tm_merge=512, ns=1, gdn_nb=1, gdn_rl=256, gdn_sl=64, ret_sl=256,
                 pool_sl=512, s5_tl=128, tq=1024, pos0=0, cos=cos_p, sin=sin_p, moe_tpq=4)
    cfg_s = dict(tm=1024, tm_merge=512, ns=ns_s, gdn_nb=1, gdn_rl=dseq, gdn_sl=dseq, ret_sl=dseq,
                 pool_sl=dseq, s5_tl=dseq, tq=dseq, pos0=past, cos=cos_s, sin=sin_s, moe_tpq=0)

    yp = x_prompt.reshape(bp * seq, D_MODEL)
    ys = x_sample.reshape(bs * dseq, D_MODEL)
    p_out, s_out = [], []
    for l in range(DEPTH):
        mem_k, mem_k_out = _mem_proj(mem2d, wk, l)
        mem_v, mem_v_out = _mem_proj(mem2d, wv, l)
        mk4 = mem_k.reshape(1, bp, MEM_LEN, D_MODEL)
        mv4 = mem_v.reshape(1, bp, MEM_LEN, D_MODEL)
        yp, st = block(yp, bp, seq, l, mk4, mv4, 0,
                       (z_conv, z_gdn, z_ret, z_pool, z_s5, z_s5), 0, cfg_p)
        conv, gdn, ret, pool_u, s5r, s5i = st
        p_out.append((conv, gdn, ret, pool_u[:, seq - POOL_BUF:], s5r, s5i, mem_k_out, mem_v_out))
        ys, st = block(ys, bs, dseq, l, ck, cv, l,
                       (state_gdn_conv, state_gdn, st_ret, state_pool, st_s5r, st_s5i), l, cfg_s)
        conv, gdn, ret, pool_u, s5r, s5i = st
        new_pool = jnp.concatenate([state_pool[l][:, dseq:], pool_u], axis=1)
        s_out.append((conv, gdn, ret, new_pool, s5r, s5i))
    p_st = [jnp.stack(t) for t in zip(*p_out)]
    s_st = [jnp.stack(t) for t in zip(*s_out)]
    return (yp.reshape(bp, seq, D_MODEL), ys.reshape(bs, dseq, D_MODEL), *p_st, *s_st)
```

```python
import functools
import math

import jax
import jax.numpy as jnp
import numpy as np
from jax import lax
from jax.experimental import pallas as pl
from jax.experimental.pallas import tpu as pltpu

F32 = jnp.float32
BF16 = jnp.bfloat16

D_MODEL = 1024
DEPTH = 4
BRANCH = 512
N_BRANCH = 4
GDN_HEADS = 4
GDN_DK = 128
GDN_QKV = 1536
GDN_CONV = 4
RET_HEADS = 4
RET_DK = 64
RET_DV = 128
ROPE_BASE = 10000.0
POOL_WINDOWS = (2, 4, 8, 16)
POOL_BUF = 15
S5_GROUPS = 32
S5_GROUP = 16
S5_STATE = 64
S5_CH = S5_GROUPS * S5_STATE
MEM_LEN = 256
XA_HEADS = 4
XA_DH = 256
N_EXPERTS = 16
EXPERTS_PER_GROUP = 4
D_EXPERT = 512
DN_ALPHA = (2.0 * DEPTH) ** 0.25
LN_EPS = 1e-5
RMS_EPS = 1e-6

LANES = 128
SUBLANES = 8
VMEM_LIMIT = 52 * 1024 * 1024

COL_QKV = 0
COL_Z = 1536
COL_RET = 2048
COL_GATE = 4096
COL_POOL = 8192
COL_S5 = 8704
H_COLS = 9216


def _cparams(*sem):
    return pltpu.CompilerParams(dimension_semantics=sem, vmem_limit_bytes=VMEM_LIMIT)


def _bdot(a, b):
    return jnp.dot(a.astype(BF16), b.astype(BF16), preferred_element_type=F32)


def _bdot_nt(a, b):
    return lax.dot_general(a.astype(BF16), b.astype(BF16), (((1,), (1,)), ((), ())),
                           preferred_element_type=F32)


def _silu(x):
    return x * jax.nn.sigmoid(x)


def _layer_norm(x, g, b):
    mu = jnp.mean(x, -1, keepdims=True)
    xc = x - mu
    var = jnp.mean(xc * xc, -1, keepdims=True)
    return xc * lax.rsqrt(var + LN_EPS) * g + b


def _mem_proj_kernel(x_ref, w_ref, o2_ref, o4_ref):
    res = jnp.dot(x_ref[...].astype(BF16), w_ref[0], preferred_element_type=F32)
    o2_ref[...] = res
    for h in range(XA_HEADS):
        o4_ref[0, :, h, :] = res[:, h * XA_DH:(h + 1) * XA_DH]


def _mem_proj(mem2d, w, layer):
    m = mem2d.shape[0]
    nb = m // MEM_LEN
    return pl.pallas_call(
        _mem_proj_kernel,
        grid=(nb,),
        in_specs=[pl.BlockSpec((MEM_LEN, D_MODEL), lambda b: (b, 0)),
                  pl.BlockSpec((1, D_MODEL, D_MODEL), lambda b: (layer, 0, 0))],
        out_specs=[pl.BlockSpec((MEM_LEN, D_MODEL), lambda b: (b, 0)),
                   pl.BlockSpec((1, MEM_LEN, XA_HEADS, XA_DH), lambda b: (b, 0, 0, 0))],
        out_shape=[jax.ShapeDtypeStruct((m, D_MODEL), F32),
                   jax.ShapeDtypeStruct((nb, MEM_LEN, XA_HEADS, XA_DH), F32)],
        compiler_params=_cparams("parallel"),
        name="memory_projection",
    )(mem2d, w)


def _inproj_kernel(x_ref, w_ref, wba_ref, o_ref, ba_ref, xb_ref):
    @pl.when(pl.program_id(1) == 0)
    def _():
        xb_ref[...] = x_ref[...].astype(BF16)
        ba_ref[...] = jnp.dot(xb_ref[...], wba_ref[0], preferred_element_type=F32)

    o_ref[...] = jnp.dot(xb_ref[...], w_ref[0], preferred_element_type=F32)


def _in_proj(x, w, w_ba, layer, tm, tn):
    m, k = x.shape
    n = w.shape[-1]
    return pl.pallas_call(
        _inproj_kernel,
        grid=(m // tm, n // tn),
        in_specs=[pl.BlockSpec((tm, k), lambda i, j: (i, 0)),
                  pl.BlockSpec((1, k, tn), lambda i, j: (layer, 0, j)),
                  pl.BlockSpec((1, k, LANES), lambda i, j: (layer, 0, 0))],
        out_specs=[pl.BlockSpec((tm, tn), lambda i, j: (i, j)),
                   pl.BlockSpec((tm, LANES), lambda i, j: (i, 0))],
        out_shape=[jax.ShapeDtypeStruct((m, n), F32), jax.ShapeDtypeStruct((m, LANES), F32)],
        scratch_shapes=[pltpu.VMEM((tm, k), BF16)],
        compiler_params=_cparams("parallel", "arbitrary"),
        name="in_proj",
    )(x, w, w_ba)


def _branch_out_spec(ns, sl, n_chunks):
    return pl.BlockSpec((ns * sl, BRANCH), lambda b, n: (b * n_chunks + n, 0))


def _seg_ids(c, sl):
    r = lax.broadcasted_iota(jnp.int32, (c, 1), 0)
    sh = int(math.log2(sl))
    return r, r & (sl - 1), r >> sh


def _gdn_kernel(qkv_ref, z_ref, ba_ref, cs_ref, s0_ref, cw_ref, gp_ref, nw_ref, *rest,
                nb, ns, rl, sl):
    o_ref, s4_ref, full_ref = rest[-3:]
    s_ref = s4_ref.at[0]
    n = pl.program_id(1)

    @pl.when(n == 0)
    def _():
        full_ref[:, 5:8, :] = cs_ref[0]
        s_ref[...] = s0_ref[0]

    for bi in range(nb):
        _gdn_block(qkv_ref, z_ref, ba_ref, cw_ref, gp_ref, nw_ref, o_ref, s_ref, full_ref,
                   bi=bi, ns=ns, rl=rl, sl=sl)


def _gdn_block(qkv_ref, z_ref, ba_ref, cw_ref, gp_ref, nw_ref, o_ref, s_ref, full_ref,
               *, bi, ns, rl, sl):
    c = ns * rl
    nsps = rl // sl
    sq = slice(bi * ns, (bi + 1) * ns)

    u3 = qkv_ref[sq]
    full_ref[sq, 8:8 + rl, :] = u3
    cw = cw_ref[0]
    acc = (cw[3:4] * u3 + cw[2:3] * full_ref[sq, 7:7 + rl, :] + cw[1:2] * full_ref[sq, 6:6 + rl, :]
           + cw[0:1] * full_ref[sq, 5:5 + rl, :])
    full_ref[sq, 5:8, :] = full_ref[sq, 5 + rl:8 + rl, :]
    qkv = _silu(acc).reshape(c, GDN_QKV)

    ba = ba_ref[sq].reshape(c, LANES)
    beta_t = jax.nn.sigmoid(ba)
    xs = ba + gp_ref[0, 1:2, :]
    softplus = jnp.maximum(xs, 0.0) + jnp.log1p(jnp.exp(-jnp.abs(xs)))
    g_t = -jnp.exp(gp_ref[0, 0:1, :]) * softplus

    r, t, seq = _seg_ids(c, sl)
    gc = g_t
    s = 1
    while s < sl:
        gc = gc + jnp.where(t >= s, pltpu.roll(gc, s, 0), 0.0)
        s *= 2
    tot = gc
    s = 1
    while s < sl:
        tot = jnp.where(t + s < sl, pltpu.roll(tot, c - s, 0), tot)
        s *= 2
    rc = tot - gc
    if c < LANES:
        gc_pad = jnp.concatenate([gc, jnp.zeros((LANES - c, LANES), F32)], axis=0)
    else:
        gc_pad = gc
    gc_t = gc_pad.T

    rr = lax.broadcasted_iota(jnp.int32, (c, c), 0)
    cc = lax.broadcasted_iota(jnp.int32, (c, c), 1)
    sh = int(math.log2(sl))
    same = (rr >> sh) == (cc >> sh)
    causal = jnp.logical_and(same, rr >= cc)
    strict = jnp.logical_and(same, rr > cc)
    eye = (rr == cc).astype(F32)

    z = z_ref[sq].reshape(c, BRANCH)
    nw = nw_ref[0]
    heads = range(GDN_HEADS)

    qs, ks, kbs, q_decs, k_dec_ts, decays, rhss = [], [], [], [], [], [], []
    for h in heads:
        q = qkv[:, h * 128:(h + 1) * 128]
        k = qkv[:, 512 + h * 128:512 + (h + 1) * 128]
        v = qkv[:, 1024 + h * 128:1024 + (h + 1) * 128]
        q = q * lax.rsqrt(jnp.sum(q * q, -1, keepdims=True) + RMS_EPS) * (GDN_DK ** -0.5)
        k = k * lax.rsqrt(jnp.sum(k * k, -1, keepdims=True) + RMS_EPS)
        beta = beta_t[:, h:h + 1]
        gcc = gc[:, 4 + h:5 + h]
        gcr = gc_t[4 + h:5 + h, 0:c]
        eg = jnp.exp(gcc)
        kb = k * beta
        qs.append(q)
        ks.append(k)
        kbs.append(kb)
        q_decs.append(q * eg)
        k_dec_ts.append((k * jnp.exp(rc[:, 4 + h:5 + h])).T)
        decays.append(jnp.where(causal, jnp.exp(jnp.where(causal, gcc - gcr, 0.0)), 0.0))
        rhss.append(jnp.concatenate([v * beta, kb * eg], axis=1))
    npows = [-jnp.where(strict, _bdot_nt(kbs[h], ks[h]) * decays[h], 0.0) for h in heads]
    sms = [eye + npows[h] for h in heads]
    if sl > 2:
        npows = [_bdot(npows[h], npows[h]) for h in heads]
        m = 2
        while 2 * m < sl:
            prods = [_bdot(npows[h], jnp.concatenate([npows[h], sms[h]], axis=1)) for h in heads]
            npows = [p[:, :c] for p in prods]
            sms = [sms[h] + prods[h][:, c:] for h in heads]
            m *= 2
        sms = [sms[h] + _bdot(npows[h], sms[h]) for h in heads]
    sols = [_bdot(sms[h], rhss[h]) for h in heads]
    attns = [jnp.where(causal, _bdot_nt(qs[h], ks[h]) * decays[h], 0.0) for h in heads]

    v_parts = [[] for _ in heads]
    o_parts = [[] for _ in heads]
    for si in range(ns):
        sts = [s_ref[bi * ns + si, h] for h in heads]
        for j in range(nsps):
            sg = si * nsps + j
            rows = slice(sg * sl, (sg + 1) * sl)
            for h in heads:
                v_j = sols[h][rows, :128] - _bdot(sols[h][rows, 128:], sts[h])
                o_parts[h].append(_bdot(q_decs[h][rows], sts[h]))
                v_parts[h].append(v_j)
                pieces = [v_j]
                if sg > 0:
                    pieces.insert(0, jnp.zeros((sg * sl, 128), F32))
                if (sg + 1) * sl < c:
                    pieces.append(jnp.zeros((c - (sg + 1) * sl, 128), F32))
                vz = v_j if len(pieces) == 1 else jnp.concatenate(pieces, axis=0)
                last = jnp.exp(tot[sg * sl:sg * sl + 1, 4 + h:5 + h])
                sts[h] = sts[h] * last + _bdot(k_dec_ts[h], vz)
        for h in heads:
            s_ref[bi * ns + si, h] = sts[h]

    outs = []
    for h in heads:
        v_new = v_parts[h][0] if len(v_parts[h]) == 1 else jnp.concatenate(v_parts[h], axis=0)
        o_cross = o_parts[h][0] if len(o_parts[h]) == 1 else jnp.concatenate(o_parts[h], axis=0)
        o = o_cross + _bdot(attns[h], v_new)
        o = o * lax.rsqrt(jnp.mean(o * o, -1, keepdims=True) + RMS_EPS) * nw
        outs.append(o * _silu(z[:, h * 128:(h + 1) * 128]))
    o_ref[bi] = jnp.concatenate(outs, axis=1).astype(o_ref.dtype)


def _gdn(h3, ba3, conv_state, s0, layer_state, cw, gp, nw, layer, nb, ns, rl, sl, stacked=None):
    nseq, length, _ = h3.shape
    assert ns == 1 or rl == length
    nq = nb * ns
    grid = (nseq // nq, length // rl)
    kern = functools.partial(_gdn_kernel, nb=nb, ns=ns, rl=rl, sl=sl)
    extra_specs = [] if stacked is None else [pl.BlockSpec(memory_space=pl.ANY)]
    extra_args = [] if stacked is None else [stacked]
    branch, new_state = pl.pallas_call(
        kern,
        grid=grid,
        input_output_aliases={} if stacked is None else {8: 1},
        in_specs=extra_specs[:0] + [
            pl.BlockSpec((nq, rl, GDN_QKV), lambda b, n: (b, n, COL_QKV // GDN_QKV)),
            pl.BlockSpec((nq, rl, BRANCH), lambda b, n: (b, n, COL_Z // BRANCH)),
            pl.BlockSpec((nq, rl, LANES), lambda b, n: (b, n, 0)),
            pl.BlockSpec((1, nq, GDN_CONV - 1, GDN_QKV), lambda b, n: (layer_state, b, 0, 0)),
            pl.BlockSpec((1, nq, GDN_HEADS, 128, 128), lambda b, n: (layer_state, b, 0, 0, 0)),
            pl.BlockSpec((1, GDN_CONV, GDN_QKV), lambda b, n: (layer, 0, 0)),
            pl.BlockSpec((1, 2, LANES), lambda b, n: (layer, 0, 0)),
            pl.BlockSpec((1, 1, LANES), lambda b, n: (layer, 0, 0)),
        ] + extra_specs,
        out_specs=[
            pl.BlockSpec((nb, ns * rl, BRANCH), lambda b, n: (b, n, 0)),
            pl.BlockSpec((1, nq, GDN_HEADS, 128, 128), lambda b, n: (layer, b, 0, 0, 0)),
        ],
        out_shape=[jax.ShapeDtypeStruct((nseq // ns, ns * length, BRANCH), BF16),
                   jax.ShapeDtypeStruct((DEPTH, nseq, GDN_HEADS, 128, 128), F32)],
        scratch_shapes=[pltpu.VMEM((nq, SUBLANES + rl, GDN_QKV), F32)],
        compiler_params=_cparams("parallel", "arbitrary"),
        name="gated_deltanet",
    )(h3, h3, ba3, conv_state, s0, cw, gp, nw, *extra_args)
    return branch.reshape(nseq * length, BRANCH), new_state


_RET_LOG_GAMMA = tuple(math.log(1.0 - 2.0 ** (-5.0 - h)) for h in range(RET_HEADS))


def _per_head(idx, vals):
    out = jnp.full(idx.shape, vals[3], F32)
    for h in (2, 1, 0):
        out = jnp.where(idx < (h + 1) * RET_DK, vals[h], out)
    return out


def _ret_kernel(hb_ref, cos_ref, sin_ref, s0_ref, o_ref, s_ref, *, ns, sl):
    c = ns * sl
    n = pl.program_id(1)

    @pl.when(n == 0)
    def _():
        s_ref[...] = s0_ref[0]

    hb = hb_ref[...].reshape(c, 2048)
    cos = cos_ref[...]
    sin = sin_ref[...]
    rq = (hb[:, 0:256] * cos + hb[:, 512:768] * sin) * (RET_DK ** -0.5)
    rk = hb[:, 256:512] * cos + hb[:, 768:1024] * sin
    v = hb[:, 1024:1536]
    g = hb[:, 1536:2048]

    r, t, seq = _seg_ids(c, sl)
    tf = t.astype(F32)
    lane = lax.broadcasted_iota(jnp.int32, (1, 256), 1)
    lgl = _per_head(lane, _RET_LOG_GAMMA)
    q_dec = rq * jnp.exp(lgl * (tf + 1.0))
    k_dec = rk * jnp.exp(lgl * (sl - 1.0 - tf))
    k_dec_t = k_dec.T

    rr = lax.broadcasted_iota(jnp.int32, (c, c), 0)
    cc = lax.broadcasted_iota(jnp.int32, (c, c), 1)
    sh = int(math.log2(sl))
    causal = jnp.logical_and((rr >> sh) == (cc >> sh), rr >= cc)
    rel = jnp.maximum(rr - cc, 0).astype(F32)
    cseq = lax.broadcasted_iota(jnp.int32, (1, c), 1) >> sh
    rowi = lax.broadcasted_iota(jnp.int32, (256, 1), 0)
    cd_rows = jnp.exp(_per_head(rowi, _RET_LOG_GAMMA) * float(sl))

    cross = []
    for si in range(ns):
        rows = slice(si * sl, (si + 1) * sl)
        st = s_ref[si]
        parts = []
        for h in range(RET_HEADS):
            mh = jnp.logical_and(lane >= h * RET_DK, lane < (h + 1) * RET_DK)
            parts.append(_bdot(jnp.where(mh, q_dec[rows], 0.0), st))
        cross.append(parts)
        kd = k_dec_t if ns == 1 else jnp.where(cseq == si, k_dec_t, 0.0)
        res = _bdot(kd, v)
        upd = jnp.zeros((256, RET_DV), F32)
        for h in range(RET_HEADS):
            rm = jnp.logical_and(rowi >= h * RET_DK, rowi < (h + 1) * RET_DK)
            upd = upd + jnp.where(rm, res[:, h * 128:(h + 1) * 128], 0.0)
        s_ref[si] = st * cd_rows + upd

    heads = range(RET_HEADS)
    mhs = [jnp.logical_and(lane >= h * RET_DK, lane < (h + 1) * RET_DK) for h in heads]
    scs = [_bdot_nt(jnp.where(mhs[h], rq, 0.0), rk)
           * jnp.where(causal, jnp.exp(_RET_LOG_GAMMA[h] * rel), 0.0) for h in heads]
    os_ = [_bdot(scs[h], v[:, h * 128:(h + 1) * 128]) for h in heads]
    outs = []
    for h in heads:
        oc = cross[0][h] if ns == 1 else jnp.concatenate([cross[si][h] for si in range(ns)], axis=0)
        o = os_[h] + oc
        mu = jnp.mean(o, -1, keepdims=True)
        oc2 = o - mu
        o = oc2 * lax.rsqrt(jnp.mean(oc2 * oc2, -1, keepdims=True) + LN_EPS)
        outs.append(_silu(g[:, h * 128:(h + 1) * 128]) * o)
    o_ref[...] = jnp.concatenate(outs, axis=1).astype(o_ref.dtype)


def _retention(h3, cos_tab, sin_tab, s0, layer_state, ns, sl):
    nseq, length, _ = h3.shape
    c = ns * sl
    grid = (nseq // ns, length // sl)
    kern = functools.partial(_ret_kernel, ns=ns, sl=sl)
    return pl.pallas_call(
        kern,
        grid=grid,
        in_specs=[
            pl.BlockSpec((ns, sl, 2048), lambda b, n: (b, n, COL_RET // 2048)),
            pl.BlockSpec((c, 256), lambda b, n: (n, 0)),
            pl.BlockSpec((c, 256), lambda b, n: (n, 0)),
            pl.BlockSpec((1, ns, 256, RET_DV), lambda b, n: (layer_state, b, 0, 0)),
        ],
        out_specs=[
            _branch_out_spec(ns, sl, grid[1]),
            pl.BlockSpec((ns, 256, RET_DV), lambda b, n: (b, 0, 0)),
        ],
        out_shape=[jax.ShapeDtypeStruct((nseq * length, BRANCH), BF16),
                   jax.ShapeDtypeStruct((nseq, 256, RET_DV), F32)],
        compiler_params=_cparams("parallel", "arbitrary"),
        name="retention",
    )(h3, cos_tab, sin_tab, s0)


def _pool_kernel(u_ref, st_ref, w_ref, sc_ref, o_ref, full_ref, *, ns, sl, pos0, carry):
    c = ns * sl
    n = pl.program_id(1)

    @pl.when(n == 0)
    def _():
        full_ref[:, 1:16, :] = st_ref[0]

    u3 = u_ref[...]
    full_ref[:, 16:16 + sl, :] = u3
    tpos = lax.broadcasted_iota(jnp.int32, (1, sl, 1), 1) + n * sl
    n_avail = (tpos + (pos0 + 1)).astype(F32)
    outs = []
    for gi, w in enumerate(POOL_WINDOWS):
        cols = slice(gi * 128, (gi + 1) * 128)
        ug = u3[:, :, cols]
        acc = ug
        for j in range(1, w):
            acc = acc + full_ref[:, 16 - j:16 - j + sl, cols]
        pooled = acc / jnp.minimum(n_avail, float(w)) - ug
        mixed = _bdot(pooled.reshape(c, 128), w_ref[0, gi])
        outs.append(mixed * sc_ref[0, :, cols])
    if carry:
        full_ref[:, 1:16, :] = full_ref[:, sl + 1:sl + 16, :]
    o_ref[...] = jnp.concatenate(outs, axis=1).astype(o_ref.dtype)


def _pool(h3, state, layer_state, w_pool, pool_scale, layer, ns, sl, pos0):
    nseq, length, _ = h3.shape
    grid = (nseq // ns, length // sl)
    kern = functools.partial(_pool_kernel, ns=ns, sl=sl, pos0=pos0, carry=grid[1] > 1)
    return pl.pallas_call(
        kern,
        grid=grid,
        in_specs=[
            pl.BlockSpec((ns, sl, BRANCH), lambda b, n: (b, n, COL_POOL // BRANCH)),
            pl.BlockSpec((1, ns, POOL_BUF, BRANCH), lambda b, n: (layer_state, b, 0, 0)),
            pl.BlockSpec((1, 4, 128, 128), lambda b, n: (layer, 0, 0, 0)),
            pl.BlockSpec((1, 1, BRANCH), lambda b, n: (layer, 0, 0)),
        ],
        out_specs=_branch_out_spec(ns, sl, grid[1]),
        out_shape=jax.ShapeDtypeStruct((nseq * length, BRANCH), BF16),
        scratch_shapes=[pltpu.VMEM((ns, 16 + sl, BRANCH), F32)],
        compiler_params=_cparams("parallel", "arbitrary"),
        name="multi_pool",
    )(h3, state, w_pool, pool_scale)


S5_SEQS = SUBLANES
S5_SLABS = 2 * S5_CH // LANES


def _s5_pitch(tl):
    return tl + 4


def _s5_kernel(u_ref, hr_ref, hi_ref, bd_ref, cd_ref, ar_ref, ai_ref, d_ref, glu_ref,
               o_ref, or_ref, oi_ref, bu_ref, h_ref, arb_ref, aib_ref, up_ref, *, tl):
    pitch = _s5_pitch(tl)
    n = pl.program_id(1)

    @pl.when(n == 0)
    def _():
        hr = hr_ref[0]
        hi = hi_ref[0]
        for kb in range(2):
            for j in range(8):
                src = slice(kb * 1024 + j * 128, kb * 1024 + (j + 1) * 128)
                h_ref[kb * 16 + j] = hr[:, src]
                h_ref[kb * 16 + 8 + j] = hi[:, src]
        for kb in range(2):
            for j in range(8):
                src = slice(kb * 1024 + j * 128, kb * 1024 + (j + 1) * 128)
                arb_ref[kb * 8 + j] = jnp.broadcast_to(ar_ref[0, :, src], (S5_SEQS, LANES))
                aib_ref[kb * 8 + j] = jnp.broadcast_to(ai_ref[0, :, src], (S5_SEQS, LANES))

        up_ref[...] = jnp.zeros_like(up_ref)

    for s in range(S5_SEQS):
        up_ref[s * pitch:s * pitch + tl, :] = u_ref[s]
    up = up_ref[...]
    upb = up.astype(BF16)
    for kb in range(2):
        bu = jnp.dot(upb[:, kb * 256:(kb + 1) * 256], bd_ref[0, kb],
                     preferred_element_type=F32)
        for j in range(16):
            bu_ref[kb * 16 + j] = bu[:, j * 128:(j + 1) * 128]

    def step(l, carry):
        new = []
        for kb in range(2):
            for j in range(8):
                re = carry[kb * 16 + j]
                im = carry[kb * 16 + 8 + j]
                ar = arb_ref[kb * 8 + j]
                ai = aib_ref[kb * 8 + j]
                idx = pl.ds(l, S5_SEQS, stride=pitch)
                b_re = bu_ref[kb * 16 + j, idx, :]
                b_im = bu_ref[kb * 16 + 8 + j, idx, :]
                n_re = ar * re - ai * im + b_re
                n_im = ar * im + ai * re + b_im
                bu_ref[kb * 16 + j, idx, :] = n_re
                bu_ref[kb * 16 + 8 + j, idx, :] = n_im
                new.append((kb * 16 + j, n_re))
                new.append((kb * 16 + 8 + j, n_im))
        new.sort(key=lambda p: p[0])
        return tuple(p[1] for p in new)

    h0 = tuple(h_ref[i] for i in range(S5_SLABS))
    hf = lax.fori_loop(0, tl, step, h0, unroll=2)
    for i in range(S5_SLABS):
        h_ref[i] = hf[i]

    @pl.when(n == pl.num_programs(1) - 1)
    def _():
        for kb in range(2):
            for j in range(8):
                dst = slice(kb * 1024 + j * 128, kb * 1024 + (j + 1) * 128)
                or_ref[:, dst] = hf[kb * 16 + j]
                oi_ref[:, dst] = hf[kb * 16 + 8 + j]

    ys = []
    for kb in range(2):
        hs = jnp.concatenate([bu_ref[kb * 16 + j].astype(BF16) for j in range(16)], axis=1)
        ys.append(jnp.dot(hs, cd_ref[0, kb], preferred_element_type=F32))
    y = jnp.concatenate(ys, axis=1) + d_ref[0] * up
    act = jax.nn.gelu(y)
    out = act * jax.nn.sigmoid(_bdot(act, glu_ref[0]))
    for s in range(S5_SEQS):
        o_ref[s] = out[s * pitch:s * pitch + tl, :]


def _s5(h3, h_re, h_im, layer_state, bd, cd, ar, ai, d_skip, w_glu, layer, tl):
    nseq, length, _ = h3.shape
    grid = (nseq // S5_SEQS, length // tl)
    kern = functools.partial(_s5_kernel, tl=tl)
    rows = S5_SEQS * _s5_pitch(tl)
    return pl.pallas_call(
        kern,
        grid=grid,
        in_specs=[
            pl.BlockSpec((S5_SEQS, tl, BRANCH), lambda b, n: (b, n, COL_S5 // BRANCH)),
            pl.BlockSpec((1, S5_SEQS, S5_CH), lambda b, n: (layer_state, b, 0)),
            pl.BlockSpec((1, S5_SEQS, S5_CH), lambda b, n: (layer_state, b, 0)),
            pl.BlockSpec((1, 2, 256, 2048), lambda b, n: (layer, 0, 0, 0)),
            pl.BlockSpec((1, 2, 2048, 256), lambda b, n: (layer, 0, 0, 0)),
            pl.BlockSpec((1, 1, S5_CH), lambda b, n: (layer, 0, 0)),
            pl.BlockSpec((1, 1, S5_CH), lambda b, n: (layer, 0, 0)),
            pl.BlockSpec((1, 1, BRANCH), lambda b, n: (layer, 0, 0)),
            pl.BlockSpec((1, BRANCH, BRANCH), lambda b, n: (layer, 0, 0)),
        ],
        out_specs=[
            pl.BlockSpec((S5_SEQS, tl, BRANCH), lambda b, n: (b, n, 0)),
            pl.BlockSpec((S5_SEQS, S5_CH), lambda b, n: (b, 0)),
            pl.BlockSpec((S5_SEQS, S5_CH), lambda b, n: (b, 0)),
        ],
        out_shape=[jax.ShapeDtypeStruct((nseq, length, BRANCH), F32),
                   jax.ShapeDtypeStruct((nseq, S5_CH), F32),
                   jax.ShapeDtypeStruct((nseq, S5_CH), F32)],
        scratch_shapes=[pltpu.VMEM((S5_SLABS, rows, LANES), F32),
                        pltpu.VMEM((S5_SLABS, S5_SEQS, LANES), F32),
                        pltpu.VMEM((S5_SLABS // 2, S5_SEQS, LANES), F32),
                        pltpu.VMEM((S5_SLABS // 2, S5_SEQS, LANES), F32),
                        pltpu.VMEM((rows, BRANCH), F32)],
        compiler_params=_cparams("parallel", "arbitrary"),
        name="s5_ssm",
    )(h3, h_re, h_im, bd, cd, ar, ai, d_skip, w_glu)


def _merge_kernel(x_ref, ba_ref, bb_ref, bc_ref, bd_ref, g_ref, wb_ref, wo_ref, wq_ref, ln_ref,
                  x1_ref, q_ref):
    mixed = None
    for i, br in enumerate((ba_ref, bb_ref, bc_ref, bd_ref)):
        proj = jnp.dot(br[...].astype(BF16), wb_ref[0, i], preferred_element_type=F32)
        term = jax.nn.sigmoid(g_ref[:, i * D_MODEL:(i + 1) * D_MODEL]) * proj
        mixed = term if mixed is None else mixed + term
    y = jnp.dot(mixed.astype(BF16), wo_ref[0], preferred_element_type=F32)
    x1 = _layer_norm(DN_ALPHA * x_ref[...] + y, ln_ref[0, 0:1, :], ln_ref[0, 1:2, :])
    x1_ref[...] = x1
    q_ref[...] = jnp.dot(x1.astype(BF16), wq_ref[0], preferred_element_type=F32)


def _merge(x, h, brs, wb, wo, wq, ln, layer, tm):
    m = x.shape[0]
    tok = lambda w: pl.BlockSpec((tm, w), lambda i: (i, 0))
    return pl.pallas_call(
        _merge_kernel,
        grid=(m // tm,),
        in_specs=[tok(D_MODEL), tok(BRANCH), tok(BRANCH), tok(BRANCH), tok(BRANCH),
                  pl.BlockSpec((tm, 4096), lambda i: (i, COL_GATE // 4096)),
                  pl.BlockSpec((1, N_BRANCH, BRANCH, D_MODEL), lambda i: (layer, 0, 0, 0)),
                  pl.BlockSpec((1, D_MODEL, D_MODEL), lambda i: (layer, 0, 0)),
                  pl.BlockSpec((1, D_MODEL, D_MODEL), lambda i: (layer, 0, 0)),
                  pl.BlockSpec((1, 2, D_MODEL), lambda i: (layer, 0, 0))],
        out_specs=[tok(D_MODEL), tok(D_MODEL)],
        out_shape=[jax.ShapeDtypeStruct((m, D_MODEL), F32)] * 2,
        compiler_params=_cparams("parallel"),
        name="branch_merge",
    )(x, *brs, h, wb, wo, wq, ln)


def _attn_kernel(q_ref, k_ref, v_ref, o_ref, *, heads_split):
    q = q_ref[0]
    heads = range(XA_HEADS)
    cols = [slice(h * XA_DH, (h + 1) * XA_DH) for h in heads]
    if heads_split:
        tq = q.shape[0]
        k2 = k_ref[0, 0].reshape(MEM_LEN * XA_HEADS, XA_DH)
        v2 = v_ref[0, 0].reshape(MEM_LEN * XA_HEADS, XA_DH)
        qs = jnp.concatenate([q[:, cols[h]] for h in heads], axis=0)
        colh = lax.broadcasted_iota(jnp.int32, (1, MEM_LEN * XA_HEADS), 1) & (XA_HEADS - 1)
        rowh = lax.broadcasted_iota(jnp.int32, (XA_HEADS * tq, 1), 0) >> int(math.log2(tq))
        s = jnp.where(colh == rowh, _bdot_nt(qs, k2) * (XA_DH ** -0.5), -1e30)
        e = jnp.exp(s - jnp.max(s, -1, keepdims=True))
        o = _bdot(e / jnp.sum(e, -1, keepdims=True), v2)
        o_ref[0] = jnp.concatenate([o[h * tq:(h + 1) * tq] for h in heads], axis=1)
    else:
        ss = [_bdot_nt(q[:, cols[h]], k_ref[0, 0, :, cols[h]]) * (XA_DH ** -0.5) for h in heads]
        es = [jnp.exp(s - jnp.max(s, -1, keepdims=True)) for s in ss]
        ps = [e / jnp.sum(e, -1, keepdims=True) for e in es]
        o_ref[0] = jnp.concatenate([_bdot(ps[h], v_ref[0, 0, :, cols[h]]) for h in heads], axis=1)


def _attention(q3, mem_k, mem_v, layer_mem, tq):
    nseq, length, _ = q3.shape
    heads_split = mem_k.ndim == 5
    if heads_split:
        mem = pl.BlockSpec((1, 1, MEM_LEN, XA_HEADS, XA_DH), lambda b, i: (layer_mem, b, 0, 0, 0))
    else:
        mem = pl.BlockSpec((1, 1, MEM_LEN, D_MODEL), lambda b, i: (layer_mem, b, 0, 0))
    return pl.pallas_call(
        functools.partial(_attn_kernel, heads_split=heads_split),
        grid=(nseq, length // tq),
        in_specs=[pl.BlockSpec((1, tq, D_MODEL), lambda b, i: (b, i, 0)), mem, mem],
        out_specs=pl.BlockSpec((1, tq, D_MODEL), lambda b, i: (b, i, 0)),
        out_shape=jax.ShapeDtypeStruct((nseq, length, D_MODEL), F32),
        compiler_params=_cparams("parallel", "arbitrary"),
        name="memory_attention",
    )(q3, mem_k, mem_v)


def _post_kernel(x1_ref, o_ref, wo_ref, ln_ref, wr_ref, br_ref, x2_ref, comb_ref):
    y = jnp.dot(o_ref[...].astype(BF16), wo_ref[0], preferred_element_type=F32)
    x2 = _layer_norm(DN_ALPHA * x1_ref[...] + y, ln_ref[0, 0:1, :], ln_ref[0, 1:2, :])
    x2_ref[...] = x2
    tm = x2.shape[0]
    logits = _bdot_nt(wr_ref[...], x2) + br_ref[:, 0:1]
    e = jnp.exp(logits - jnp.max(logits, 0, keepdims=True))
    p = e / jnp.sum(e, 0, keepdims=True)
    rowi = lax.broadcasted_iota(jnp.int32, (N_EXPERTS, 1), 0)
    rowf = rowi.astype(F32)
    grp = rowi >> 2
    best = None
    for gidx in range(N_GROUPS):
        gm = jnp.max(p[gidx * EXPERTS_PER_GROUP:(gidx + 1) * EXPERTS_PER_GROUP], 0, keepdims=True)
        if best is None:
            best, gi = gm, jnp.zeros(gm.shape, jnp.int32)
        else:
            upd = gm > best
            gi = jnp.where(upd, gidx, gi)
            best = jnp.where(upd, gm, best)
    cand = jnp.where(grp == gi, p, -1.0)
    m1 = jnp.max(cand, 0, keepdims=True)
    e1 = jnp.min(jnp.where(cand == m1, rowf, 1e9), 0, keepdims=True)
    cand2 = jnp.where(rowf == e1, -2.0, cand)
    m2 = jnp.max(cand2, 0, keepdims=True)
    e2 = jnp.min(jnp.where(cand2 == m2, rowf, 1e9), 0, keepdims=True)
    tot = m1 + m2
    comb_t = jnp.where(rowf == e1, m1 / tot, 0.0) + jnp.where(rowf == e2, m2 / tot, 0.0)
    sub = lax.broadcasted_iota(jnp.int32, (SUBLANES, 1), 0)
    grp_rows = jnp.where(sub == 0, gi.astype(F32), 0.0)
    rest = jnp.zeros((LANES - N_EXPERTS - SUBLANES, tm), F32)
    comb_ref[...] = jnp.concatenate([comb_t, grp_rows, rest], axis=0).T


def _post(x1, o, wo, ln, wr, br, layer, tm):
    m = x1.shape[0]
    tok = lambda w: pl.BlockSpec((tm, w), lambda i: (i, 0))
    return pl.pallas_call(
        _post_kernel,
        grid=(m // tm,),
        in_specs=[tok(D_MODEL), tok(D_MODEL),
                  pl.BlockSpec((1, D_MODEL, D_MODEL), lambda i: (layer, 0, 0)),
                  pl.BlockSpec((1, 2, D_MODEL), lambda i: (layer, 0, 0)),
                  pl.BlockSpec((N_EXPERTS, D_MODEL), lambda i: (0, 0)),
                  pl.BlockSpec((N_EXPERTS, LANES), lambda i: (0, 0))],
        out_specs=[tok(D_MODEL), tok(LANES)],
        out_shape=[jax.ShapeDtypeStruct((m, D_MODEL), F32), jax.ShapeDtypeStruct((m, LANES), F32)],
        compiler_params=_cparams("parallel"),
        name="attn_out_router",
    )(x1, o, wo, ln, wr, br)


def _moe_kernel(x_ref, comb_ref, w1_ref, w3_ref, w2_ref, ln_ref, o_ref, acc_ref, xb_ref):
    e = pl.program_id(1)

    @pl.when(e == 0)
    def _():
        acc_ref[...] = jnp.zeros_like(acc_ref)
        xb_ref[...] = x_ref[...].astype(BF16)

    comb = comb_ref[...]
    lane = lax.broadcasted_iota(jnp.int32, comb.shape, 1)
    ce = jnp.sum(jnp.where(lane == e, comb, 0.0), -1, keepdims=True)
    xb = xb_ref[...]
    h1 = jnp.dot(xb, w1_ref[0, 0].astype(BF16), preferred_element_type=F32)
    h3 = jnp.dot(xb, w3_ref[0, 0].astype(BF16), preferred_element_type=F32)
    hid = _silu(h1) * h3 * ce
    acc_ref[...] += jnp.dot(hid.astype(BF16), w2_ref[0, 0].astype(BF16),
                            preferred_element_type=F32)

    @pl.when(e == pl.num_programs(1) - 1)
    def _():
        o_ref[...] = _layer_norm(DN_ALPHA * x_ref[...] + acc_ref[...],
                                 ln_ref[0, 0:1, :], ln_ref[0, 1:2, :])


def _moe(x2, comb, w1, w3, w2, ln, layer, tm):
    m = x2.shape[0]
    w_up = pl.BlockSpec((1, 1, D_MODEL, D_EXPERT), lambda i, e: (layer, e, 0, 0))
    return pl.pallas_call(
        _moe_kernel,
        grid=(m // tm, N_EXPERTS),
        in_specs=[pl.BlockSpec((tm, D_MODEL), lambda i, e: (i, 0)),
                  pl.BlockSpec((tm, LANES), lambda i, e: (i, 0)),
                  w_up, w_up,
                  pl.BlockSpec((1, 1, D_EXPERT, D_MODEL), lambda i, e: (layer, e, 0, 0)),
                  pl.BlockSpec((1, 2, D_MODEL), lambda i, e: (layer, 0, 0))],
        out_specs=pl.BlockSpec((tm, D_MODEL), lambda i, e: (i, 0)),
        out_shape=jax.ShapeDtypeStruct((m, D_MODEL), F32),
        scratch_shapes=[pltpu.VMEM((tm, D_MODEL), F32), pltpu.VMEM((tm, D_MODEL), BF16)],
        compiler_params=_cparams("parallel", "arbitrary"),
        name="moe_experts",
    )(x2, comb, w1, w3, w2, ln)


N_GROUPS = N_EXPERTS // EXPERTS_PER_GROUP
GROUP_LANE = N_EXPERTS


def _bucket_cap(tm):
    return tm // N_GROUPS + tm // 16


def _split_bf16(x):
    hi = x.astype(BF16)
    return hi, (x - hi.astype(F32)).astype(BF16)


def _moe_sort_kernel(x_ref, comb_ref, xs_ref, gs_ref, dest_ref, cnt_ref, *, cap):
    tm = x_ref.shape[0]
    slots = N_GROUPS * cap
    comb = comb_ref[...]
    lanef = lax.broadcasted_iota(jnp.int32, comb.shape, 1).astype(F32)
    gid = comb[:, GROUP_LANE:GROUP_LANE + 1]
    onehot = jnp.where(lanef == gid, 1.0, 0.0)
    rr = lax.broadcasted_iota(jnp.int32, (tm, tm), 0)
    cc = lax.broadcasted_iota(jnp.int32, (tm, tm), 1)
    tril = jnp.where(rr >= cc, 1.0, 0.0).astype(BF16)
    cs = jnp.dot(tril, onehot.astype(BF16), preferred_element_type=F32)
    rank = jnp.sum(onehot * cs, -1, keepdims=True) - 1.0
    dest = gid * float(cap) + rank
    cnt_ref[0] = jnp.broadcast_to(cs[tm - 1:tm, :], (SUBLANES, LANES))
    dest_b = jnp.broadcast_to(dest, (tm, LANES))
    dest_ref[...] = dest_b
    dest_row = dest_b.T[0:1, :]
    srow = lax.broadcasted_iota(jnp.int32, (slots, 1), 0).astype(F32)
    perm = jnp.where(srow == dest_row, 1.0, 0.0).astype(BF16)
    xs = jnp.dot(perm, x_ref[...].astype(BF16), preferred_element_type=F32).astype(xs_ref.dtype)
    hi, lo = _split_bf16(comb)
    gs = (jnp.dot(perm, hi, preferred_element_type=F32)
          + jnp.dot(perm, lo, preferred_element_type=F32))
    for g in range(N_GROUPS):
        xs_ref[g, 0] = xs[g * cap:(g + 1) * cap, :]
        gs_ref[g, 0] = gs[g * cap:(g + 1) * cap, :]


def _moe_sort(x2, comb, tm):
    m = x2.shape[0]
    cap = _bucket_cap(tm)
    nt = m // tm
    return pl.pallas_call(
        functools.partial(_moe_sort_kernel, cap=cap),
        grid=(nt,),
        in_specs=[pl.BlockSpec((tm, D_MODEL), lambda i: (i, 0)),
                  pl.BlockSpec((tm, LANES), lambda i: (i, 0))],
        out_specs=[pl.BlockSpec((N_GROUPS, 1, cap, D_MODEL), lambda i: (0, i, 0, 0)),
                   pl.BlockSpec((N_GROUPS, 1, cap, LANES), lambda i: (0, i, 0, 0)),
                   pl.BlockSpec((tm, LANES), lambda i: (i, 0)),
                   pl.BlockSpec((1, SUBLANES, LANES), lambda i: (i, 0, 0))],
        out_shape=[jax.ShapeDtypeStruct((N_GROUPS, nt, cap, D_MODEL), BF16),
                   jax.ShapeDtypeStruct((N_GROUPS, nt, cap, LANES), F32),
                   jax.ShapeDtypeStruct((m, LANES), F32),
                   jax.ShapeDtypeStruct((nt, SUBLANES, LANES), F32)],
        compiler_params=_cparams("parallel"),
        name="moe_bucket_sort",
    )(x2, comb)


def _moe_bucket_kernel(xs_ref, gs_ref, w1_ref, w3_ref, w2_ref, o_ref, acc_ref):
    g = pl.program_id(0)
    j = pl.program_id(2)
    rows = acc_ref.shape[0]

    @pl.when(j == 0)
    def _():
        acc_ref[...] = jnp.zeros_like(acc_ref)

    gates = gs_ref[0].reshape(rows, LANES)
    lane = lax.broadcasted_iota(jnp.int32, gates.shape, 1)
    ce = jnp.sum(jnp.where(lane == g * EXPERTS_PER_GROUP + j, gates, 0.0), -1, keepdims=True)
    xb = xs_ref[0].reshape(rows, D_MODEL)
    h1 = jnp.dot(xb, w1_ref[0, 0].astype(BF16), preferred_element_type=F32)
    h3 = jnp.dot(xb, w3_ref[0, 0].astype(BF16), preferred_element_type=F32)
    hid = _silu(h1) * h3 * ce
    acc_ref[...] += jnp.dot(hid.astype(BF16), w2_ref[0, 0].astype(BF16),
                            preferred_element_type=F32)

    @pl.when(j == pl.num_programs(2) - 1)
    def _():
        o_ref[0] = acc_ref[...].reshape(o_ref.shape[1:])


def _moe_buckets(xs, gs, w1, w3, w2, layer, tpq):
    _, nt, cap, _ = xs.shape

    def bucket(width):
        return pl.BlockSpec((1, tpq, cap, width), lambda g, q, j: (g, q, 0, 0))

    def expert(rows, cols):
        return pl.BlockSpec((1, 1, rows, cols),
                            lambda g, q, j: (layer, g * EXPERTS_PER_GROUP + j, 0, 0))

    return pl.pallas_call(
        _moe_bucket_kernel,
        grid=(N_GROUPS, nt // tpq, EXPERTS_PER_GROUP),
        in_specs=[bucket(D_MODEL), bucket(LANES),
                  expert(D_MODEL, D_EXPERT), expert(D_MODEL, D_EXPERT), expert(D_EXPERT, D_MODEL)],
        out_specs=bucket(D_MODEL),
        out_shape=jax.ShapeDtypeStruct((N_GROUPS, nt, cap, D_MODEL), F32),
        scratch_shapes=[pltpu.VMEM((tpq * cap, D_MODEL), F32)],
        compiler_params=_cparams("parallel", "parallel", "arbitrary"),
        name="moe_bucket_experts",
    )(xs, gs, w1, w3, w2)


def _moe_unsort_kernel(x_ref, dest_ref, ys_ref, ln_ref, o_ref):
    _, _, cap, _ = ys_ref.shape
    slots = N_GROUPS * cap
    dest = dest_ref[:, 0:1]
    slot = lax.broadcasted_iota(jnp.int32, (1, slots), 1).astype(F32)
    perm_t = jnp.where(dest == slot, 1.0, 0.0).astype(BF16)
    hi, lo = _split_bf16(ys_ref[...].reshape(slots, D_MODEL))
    y = (jnp.dot(perm_t, hi, preferred_element_type=F32)
         + jnp.dot(perm_t, lo, preferred_element_type=F32))
    o_ref[...] = _layer_norm(DN_ALPHA * x_ref[...] + y, ln_ref[0, 0:1, :], ln_ref[0, 1:2, :])


def _moe_unsort(x2, dest, ys, ln, layer, tm):
    m = x2.shape[0]
    _, _, cap, _ = ys.shape
    return pl.pallas_call(
        _moe_unsort_kernel,
        grid=(m // tm,),
        in_specs=[pl.BlockSpec((tm, D_MODEL), lambda i: (i, 0)),
                  pl.BlockSpec((tm, LANES), lambda i: (i, 0)),
                  pl.BlockSpec((N_GROUPS, 1, cap, D_MODEL), lambda i: (0, i, 0, 0)),
                  pl.BlockSpec((1, 2, D_MODEL), lambda i: (layer, 0, 0))],
        out_specs=pl.BlockSpec((tm, D_MODEL), lambda i: (i, 0)),
        out_shape=jax.ShapeDtypeStruct((m, D_MODEL), F32),
        compiler_params=_cparams("parallel"),
        name="moe_unsort_norm",
    )(x2, dest, ys, ln)


def _moe_bucketed(x2, comb, w1, w3, w2, ln, layer, tm, tpq):
    xs, gs, dest, cnt = _moe_sort(x2, comb, tm)
    overflow = jnp.max(cnt[:, 0, :N_GROUPS]) > _bucket_cap(tm)

    def bucketed():
        return _moe_unsort(x2, dest, _moe_buckets(xs, gs, w1, w3, w2, layer, tpq), ln, layer, tm)

    def dense():
        return _moe(x2, comb, w1, w3, w2, ln, layer, tm)

    return lax.cond(overflow, dense, bucketed)


_W_IN_OFFS = np.cumsum((0, GDN_QKV, 512, 4, 4, 256, 256, 512, 512, 512, 512, 4096))


def _w_in_kernel(wt_ref, o_ref, ba_ref):
    o = _W_IN_OFFS
    tk = wt_ref.shape[2]

    def seg_t(i):
        return wt_ref[0, int(o[i]):int(o[i + 1]), :]

    def swap_pairs_t(x):
        row = lax.broadcasted_iota(jnp.int32, x.shape, 0)
        n = x.shape[0]
        return jnp.where((row & 1) == 0, pltpu.roll(x, n - 1, 0), pltpu.roll(x, 1, 0))

    rq, rk = seg_t(4), seg_t(5)
    pieces = [seg_t(0), seg_t(1), rq, rk, swap_pairs_t(rq), swap_pairs_t(rk),
              seg_t(6), seg_t(7), seg_t(10), seg_t(8), seg_t(9)]
    col = 0
    for p in pieces:
        n = p.shape[0]
        o_ref[0, :, col:col + n] = p.T.astype(BF16)
        col += n
    ba_t = jnp.concatenate([wt_ref[0, int(o[2]):int(o[4]), :], jnp.zeros((LANES - 8, tk), F32)],
                           axis=0)
    ba_ref[0] = ba_t.T.astype(BF16)


def _prep_w_in_pallas(w_in, tk=256):
    depth, k, n = w_in.shape
    w_t = jnp.swapaxes(w_in, 1, 2)
    return pl.pallas_call(
        _w_in_kernel,
        grid=(depth, k // tk),
        in_specs=[pl.BlockSpec((1, n, tk), lambda l, i: (l, 0, i))],
        out_specs=[pl.BlockSpec((1, tk, H_COLS), lambda l, i: (l, i, 0)),
                   pl.BlockSpec((1, tk, LANES), lambda l, i: (l, i, 0))],
        out_shape=[jax.ShapeDtypeStruct((depth, k, H_COLS), BF16),
                   jax.ShapeDtypeStruct((depth, k, LANES), BF16)],
        compiler_params=_cparams("parallel", "parallel"),
        name="w_in_relayout",
    )(w_t)


def _rope_tables(pos):
    inv_freq = 1.0 / (ROPE_BASE ** jnp.linspace(0.0, 1.0, RET_DK // 2, dtype=F32))
    ang = pos.astype(F32)[:, None] * inv_freq
    cos = jnp.repeat(jnp.cos(ang), 2, axis=1)
    sin = jnp.repeat(jnp.sin(ang), 2, axis=1)
    sign = jnp.tile(jnp.array([-1.0, 1.0], F32), RET_DK // 2)
    return jnp.tile(cos, (1, RET_HEADS)), jnp.tile(sin * sign, (1, RET_HEADS))


def _prep_s5(a_re, a_im, log_dt, b_re, b_im, c_re, c_im):
    dt = jnp.exp(log_dt.astype(F32))[..., None]
    mag = jnp.exp(a_re * dt)
    ab_re, ab_im = mag * jnp.cos(a_im * dt), mag * jnp.sin(a_im * dt)
    den = a_re * a_re + a_im * a_im
    coef_re = ((ab_re - 1.0) * a_re + ab_im * a_im) / den
    coef_im = (ab_im * a_re - (ab_re - 1.0) * a_im) / den
    bb_re = coef_re[..., None] * b_re - coef_im[..., None] * b_im
    bb_im = coef_re[..., None] * b_im + coef_im[..., None] * b_re
    eye = jnp.eye(16, dtype=F32)

    def pack_b(bb):
        x = bb.reshape(DEPTH, 2, 16, S5_STATE, S5_GROUP)
        return jnp.einsum('dkgpc,gh->dkgchp', x, eye).reshape(DEPTH, 2, 256, 1024)

    def pack_c(cm):
        x = cm.reshape(DEPTH, 2, 16, S5_GROUP, S5_STATE)
        return jnp.einsum('dkgcp,gh->dkgphc', x, eye).reshape(DEPTH, 2, 1024, 256)

    bd = jnp.concatenate([pack_b(bb_re), pack_b(bb_im)], axis=-1).astype(BF16)
    cd = jnp.concatenate([pack_c(c_re.astype(F32)), -pack_c(c_im.astype(F32))], axis=-2).astype(BF16)
    ar = ab_re.reshape(DEPTH, 1, S5_CH)
    ai = ab_im.reshape(DEPTH, 1, S5_CH)
    return bd, cd, ar, ai


def kernel(x_prompt, x_sample, mem_prompt, state_gdn_conv, state_gdn, state_ret, state_pool,
           state_s5_re, state_s5_im, cache_mem_k, cache_mem_v, w_in, gdn_conv_w, gdn_a_log,
           gdn_dt_bias, gdn_norm_w, pool_w, pool_scale, s5_a_re, s5_a_im, s5_log_dt, s5_b_re,
           s5_b_im, s5_c_re, s5_c_im, s5_d, s5_w_glu, w_branch, w_out, xa_w_q, xa_w_k, xa_w_v,
           xa_w_o, ln_g, ln_b, w_router, b_router, moe_w1, moe_w3, moe_w2):
    bp, seq, _ = x_prompt.shape
    bs, dseq, _ = x_sample.shape
    past = 16384

    w_main, w_ba = _prep_w_in_pallas(w_in)
    gp = jnp.zeros((DEPTH, 2, LANES), F32)
    gp = gp.at[:, 0, 4:8].set(gdn_a_log.astype(F32)).at[:, 1, 4:8].set(gdn_dt_bias.astype(F32))
    nw = gdn_norm_w.astype(F32).reshape(DEPTH, 1, GDN_DK)
    cw = gdn_conv_w.astype(F32)
    pw = pool_w.astype(BF16)
    psc = pool_scale.astype(F32).reshape(DEPTH, 1, BRANCH)
    bd, cd, ar, ai = _prep_s5(s5_a_re.astype(F32), s5_a_im.astype(F32), s5_log_dt,
                              s5_b_re.astype(F32), s5_b_im.astype(F32), s5_c_re, s5_c_im)
    d_skip = s5_d.astype(F32).reshape(DEPTH, 1, BRANCH)
    glu = s5_w_glu.astype(BF16)
    wb = w_branch.astype(BF16)
    wo = w_out.astype(BF16)
    wq, wk, wv, wxo = (w.astype(BF16) for w in (xa_w_q, xa_w_k, xa_w_v, xa_w_o))
    ln = jnp.stack([ln_g.astype(F32), ln_b.astype(F32)], axis=2)
    ln1, ln2, ln3 = ln[:, 0], ln[:, 1], ln[:, 2]
    wr = w_router.T.astype(BF16)
    br = jnp.broadcast_to(b_router.astype(F32)[:, None], (N_EXPERTS, LANES))

    cos_p, sin_p = _rope_tables(jnp.arange(seq))
    cos_s, sin_s = _rope_tables(past + jnp.arange(dseq))
    ns_s = 16
    cos_s, sin_s = jnp.tile(cos_s, (ns_s, 1)), jnp.tile(sin_s, (ns_s, 1))

    zeros = lambda *s: jnp.zeros((1,) + s, F32)
    z_conv, z_gdn = zeros(bp, GDN_CONV - 1, GDN_QKV), zeros(bp, GDN_HEADS, 128, 128)
    z_ret, z_pool, z_s5 = zeros(bp, 256, RET_DV), zeros(bp, POOL_BUF, BRANCH), zeros(bp, S5_CH)
    st_ret = state_ret.reshape(DEPTH, bs, 256, RET_DV)
    st_s5r = state_s5_re.reshape(DEPTH, bs, S5_CH)
    st_s5i = state_s5_im.reshape(DEPTH, bs, S5_CH)
    mem2d = mem_prompt.reshape(bp * MEM_LEN, D_MODEL)
    ck, cv = cache_mem_k, cache_mem_v

    def block(x2d, nseq, length, layer, mem_k, mem_v, layer_mem, states, layer_state, cfg,
              gdn_stack):
        conv0, gdn0, ret0, pool0, s5r0, s5i0 = states
        tm = cfg["tm"]
        h, hba = _in_proj(x2d, w_main, w_ba, layer, tm, 2304)
        h3 = h.reshape(nseq, length, H_COLS)
        ba3 = hba.reshape(nseq, length, LANES)
        br_a, new_gdn = _gdn(h3, ba3, conv0, gdn0, layer_state, cw, gp, nw, layer,
                             cfg["gdn_nb"], cfg["ns"], cfg["gdn_rl"], cfg["gdn_sl"], gdn_stack)
        br_b, new_ret = _retention(h3, cfg["cos"], cfg["sin"], ret0, layer_state,
                                   cfg["ns"], cfg["ret_sl"])
        br_c = _pool(h3, pool0, layer_state, pw, psc, layer, cfg["ns"], cfg["pool_sl"], cfg["pos0"])
        br_d, new_re, new_im = _s5(h3, s5r0, s5i0, layer_state, bd, cd, ar, ai, d_skip, glu,
                                   layer, cfg["s5_tl"])
        brs = [br_a, br_b, br_c, br_d.reshape(nseq * length, BRANCH)]
        x1, q = _merge(x2d, h, brs, wb, wo, wq, ln1, layer, cfg["tm_merge"])
        o = _attention(q.reshape(nseq, length, D_MODEL), mem_k, mem_v, layer_mem, cfg["tq"])
        x2, comb = _post(x1, o.reshape(nseq * length, D_MODEL), wxo, ln2, wr, br, layer, tm)
        if cfg["moe_tpq"]:
            x3 = _moe_bucketed(x2, comb, moe_w1, moe_w3, moe_w2, ln3, layer, tm, cfg["moe_tpq"])
        else:
            x3 = _moe(x2, comb, moe_w1, moe_w3, moe_w2, ln3, layer, tm)
        new_conv = h3[:, length - (GDN_CONV - 1):, COL_QKV:COL_QKV + GDN_QKV]
        pool_u = h3[:, :, COL_POOL:COL_POOL + BRANCH]
        return x3, (new_conv, new_gdn, new_ret.reshape(nseq, RET_HEADS, RET_DK, RET_DV), pool_u,
                    new_re.reshape(nseq, S5_GROUPS, S5_STATE), new_im.reshape(nseq, S5_GROUPS, S5_STATE))

    cfg_p = dict(tm=1024, tm_merge=512, ns=1, gdn_nb=1, gdn_rl=256, gdn_sl=64, ret_sl=256,
                 pool_sl=512, s5_tl=128, tq=1024, pos0=0, cos=cos_p, sin=sin_p, moe_tpq=4)
    cfg_s = dict(tm=1024, tm_merge=512, ns=ns_s, gdn_nb=1, gdn_rl=dseq, gdn_sl=dseq, ret_sl=dseq,
                 pool_sl=dseq, s5_tl=dseq, tq=dseq, pos0=past, cos=cos_s, sin=sin_s, moe_tpq=0)

    yp = x_prompt.reshape(bp * seq, D_MODEL)
    ys = x_sample.reshape(bs * dseq, D_MODEL)
    p_out, s_out = [], []
    p_gdn = s_gdn = None
    for l in range(DEPTH):
        mem_k, mem_k_out = _mem_proj(mem2d, wk, l)
        mem_v, mem_v_out = _mem_proj(mem2d, wv, l)
        mk4 = mem_k.reshape(1, bp, MEM_LEN, D_MODEL)
        mv4 = mem_v.reshape(1, bp, MEM_LEN, D_MODEL)
        yp, st = block(yp, bp, seq, l, mk4, mv4, 0,
                       (z_conv, z_gdn, z_ret, z_pool, z_s5, z_s5), 0, cfg_p, p_gdn)
        conv, p_gdn, ret, pool_u, s5r, s5i = st
        p_out.append((conv, ret, pool_u[:, seq - POOL_BUF:], s5r, s5i, mem_k_out, mem_v_out))
        ys, st = block(ys, bs, dseq, l, ck, cv, l,
                       (state_gdn_conv, state_gdn, st_ret, state_pool, st_s5r, st_s5i), l, cfg_s,
                       s_gdn)
        conv, s_gdn, ret, pool_u, s5r, s5i = st
        new_pool = jnp.concatenate([state_pool[l][:, dseq:], pool_u], axis=1)
        s_out.append((conv, ret, new_pool, s5r, s5i))
    p_st = [jnp.stack(t) for t in zip(*p_out)]
    s_st = [jnp.stack(t) for t in zip(*s_out)]
    p_st.insert(1, p_gdn)
    s_st.insert(1, s_gdn)
    return (yp.reshape(bp, seq, D_MODEL), ys.reshape(bs, dseq, D_MODEL), *p_st, *s_st)
```
